```python
import functools
import jax, jax.numpy as jnp
from jax import lax
import numpy as np

D_MODEL = 2048
BATCH = 1
SEQ = 16384
DEPTH = 1
DEC_BATCH = 32
DEC_SEQ = 1
PAST_LEN = 16384
PAGE_SIZE = 128

ATT_HEADS = 8
HEAD_DIM = 128
ATT_WIDTH = ATT_HEADS * HEAD_DIM
DILATED_PATTERNS = ((128, 1), (512, 4), (2048, 16))
WINDOW = max(w for w, _ in DILATED_PATTERNS)
Q_BLOCK = 128
HG_HEADS = 8
HG_DK = 128
HG_DV = (D_MODEL - ATT_WIDTH) // HG_HEADS
HG_KWIDTH = HG_HEADS * HG_DK
HG_VWIDTH = HG_HEADS * HG_DV
HG_CHUNK = 64
MIX_WIDTH = ATT_WIDTH + HG_VWIDTH
IN_WIDTH = 3 * ATT_WIDTH + 2 * HG_KWIDTH + 2 * HG_VWIDTH
IN_SPLITS = (ATT_WIDTH, 2 * ATT_WIDTH, 3 * ATT_WIDTH,
             3 * ATT_WIDTH + HG_KWIDTH, 3 * ATT_WIDTH + 2 * HG_KWIDTH,
             3 * ATT_WIDTH + 2 * HG_KWIDTH + HG_VWIDTH)
D_FF = -(-(8 * D_MODEL) // (3 * 256)) * 256
EPS = 1e-6

kernel_name = 'hymba_dilated_attn_hgrn2_decode_step'


def rmsnorm(x, g):
    xf = x.astype(jnp.float32)
    y = xf * lax.rsqrt(jnp.mean(xf * xf, axis=-1, keepdims=True) + EPS)
    return (y * g.astype(jnp.float32)).astype(x.dtype)


def dilated_attention(q, k_all, v_all, q_idx):
    scale = HEAD_DIM ** -0.5
    lses, outs = [], []
    for window, dil in DILATED_PATTERNS:
        n_keys = window // dil + 1
        idx = q_idx[:, None] - dil * jnp.arange(n_keys, dtype=jnp.int32)[None, :]
        valid = idx >= 0
        idx_c = jnp.maximum(idx, 0)
        kg = jnp.take(k_all, idx_c, axis=1)
        vg = jnp.take(v_all, idx_c, axis=1)
        s = jnp.einsum('bqhd,bqjhd->bhqj', q, kg).astype(jnp.float32) * scale
        s = jnp.where(valid[None, None], s, -jnp.inf)
        lse = jax.nn.logsumexp(s, axis=-1)
        p = jnp.exp(s - lse[..., None])
        o = jnp.einsum('bhqj,bqjhd->bqhd', p.astype(v_all.dtype), vg)
        lses.append(lse)
        outs.append(o.astype(jnp.float32))
    w = jax.nn.softmax(jnp.stack(lses, 0), axis=0)
    w = jnp.transpose(w, (0, 1, 3, 2))[..., None]
    return jnp.sum(w * jnp.stack(outs, 0), axis=0).astype(q.dtype)


def prompt_attend(q, k, v):
    b, t, h, dh = q.shape
    nb = t // Q_BLOCK
    qb = jnp.transpose(q.reshape(b, nb, Q_BLOCK, h, dh), (1, 0, 2, 3, 4))
    pos = jnp.arange(t, dtype=jnp.int32).reshape(nb, Q_BLOCK)
    o = lax.map(lambda a: dilated_attention(a[0], k, v, a[1]), (qb, pos))
    o = jnp.transpose(o, (1, 0, 2, 3, 4)).reshape(b, t, h, dh)
    keep = min(WINDOW, t)
    return o, (k[:, t - keep:], v[:, t - keep:])


def sample_attend(q, k, v, cache_k, cache_v):
    past = cache_k.shape[1]
    k_all = jnp.concatenate([cache_k.astype(k.dtype), k], axis=1)
    v_all = jnp.concatenate([cache_v.astype(v.dtype), v], axis=1)
    q_idx = past + jnp.arange(q.shape[1], dtype=jnp.int32)
    return dilated_attention(q, k_all, v_all, q_idx), (k, v)


def hgrn_chunked(q, k, v, logf, s0, chunk):
    b_, t = q.shape[:2]
    n = t // chunk

    def to_chunks(a):
        return jnp.moveaxis(a.astype(jnp.float32).reshape(b_, n, chunk, *a.shape[2:]), 1, 0)

    causal = jnp.tril(jnp.ones((chunk, chunk), dtype=bool))[None, :, :, None, None]

    def step(s, xs):
        qc, kc, vc, gc = xs
        cum = jnp.cumsum(gc, axis=1)
        o_inter = jnp.einsum('bthk,bhkv->bthv', qc * jnp.exp(cum), s)
        decay = jnp.exp(jnp.where(causal, cum[:, :, None] - cum[:, None, :], -jnp.inf))
        a = jnp.einsum('bthk,bshk,btshk->bhts', qc, kc, decay)
        o_intra = jnp.einsum('bhts,bshv->bthv', a, vc)
        last = cum[:, -1]
        k_dec = kc * jnp.exp(last[:, None] - cum)
        s_new = jnp.exp(last)[..., None] * s + jnp.einsum('bshk,bshv->bhkv', k_dec, vc)
        return s_new, o_inter + o_intra

    s_fin, o = lax.scan(step, s0.astype(jnp.float32), (to_chunks(q), to_chunks(k), to_chunks(v), to_chunks(logf)))
    o = jnp.moveaxis(o, 0, 1).reshape(b_, t, *v.shape[2:])
    return o, s_fin


def decoder_layer(x, lb, n_pre_mix, w_in, attn_gain, hg_gain, w_out, n_post_mix,
                  n_pre_ffn, w_gate, w_up, w_down, n_post_ffn, attend, recur):
    b, t, _ = x.shape
    h = rmsnorm(x, n_pre_mix)
    proj = jnp.einsum('btd,de->bte', h, w_in)
    q_a, k_a, v_a, q_h, f_h, i_h, g_h = jnp.split(proj, IN_SPLITS, axis=-1)
    heads = lambda a, nh: a.reshape(b, t, nh, -1)
    o_att, kv_new = attend(heads(q_a, ATT_HEADS), heads(k_a, ATT_HEADS), heads(v_a, ATT_HEADS))
    f = lb + (1.0 - lb) * jax.nn.sigmoid(heads(f_h, HG_HEADS).astype(jnp.float32))
    qh = jax.nn.silu(heads(q_h, HG_HEADS).astype(jnp.float32))
    o_hg, s_new = recur(qh, 1.0 - f, heads(i_h, HG_HEADS).astype(jnp.float32), jnp.log(f))
    o_att = rmsnorm(o_att.reshape(b, t, ATT_WIDTH), attn_gain)
    o_hg = rmsnorm(o_hg, hg_gain) * jax.nn.silu(heads(g_h, HG_HEADS).astype(jnp.float32))
    merged = jnp.concatenate([o_att.astype(x.dtype), o_hg.reshape(b, t, HG_VWIDTH).astype(x.dtype)], axis=-1)
    mix = jnp.einsum('bte,ed->btd', merged, w_out)
    x = x + rmsnorm(mix, n_post_mix)
    h = rmsnorm(x, n_pre_ffn)
    ff = jax.nn.silu(jnp.einsum('btd,df->btf', h, w_gate)) * jnp.einsum('btd,df->btf', h, w_up)
    ff = jnp.einsum('btf,fd->btd', ff, w_down)
    x = x + rmsnorm(ff, n_post_ffn)
    return x, kv_new, s_new


def setup_inputs(seed: int = 0) -> dict:
    key = jax.random.key(seed)
    ks = jax.random.split(key, 20)
    f32 = jnp.float32
    nrm = lambda k, shape, s: jax.random.normal(k, shape, f32) * s
    gain = lambda k, shape: 1.0 + 0.1 * jax.random.normal(k, shape, f32)
    past_rows = min(WINDOW, PAST_LEN)
    return {
        'x_prompt': nrm(ks[0], (BATCH, SEQ, D_MODEL), 1.0),
        'x_sample': nrm(ks[1], (DEC_BATCH, DEC_SEQ, D_MODEL), 1.0),
        'cache_win_k': nrm(ks[2], (DEPTH, DEC_BATCH, past_rows, ATT_HEADS, HEAD_DIM), 1.0),
        'cache_win_v': nrm(ks[3], (DEPTH, DEC_BATCH, past_rows, ATT_HEADS, HEAD_DIM), 1.0),
        'state_hgrn': nrm(ks[4], (DEPTH, DEC_BATCH, HG_HEADS, HG_DK, HG_DV), 0.3),
        'norm_pre_mix': gain(ks[5], (DEPTH, D_MODEL)),
        'w_in': nrm(ks[6], (DEPTH, D_MODEL, IN_WIDTH), D_MODEL ** -0.5),
        'hg_lb_logits': nrm(ks[7], (DEPTH + 1, HG_KWIDTH), 0.5),
        'attn_out_gain': gain(ks[8], (DEPTH, ATT_WIDTH)),
        'hg_norm_gain': gain(ks[9], (DEPTH, HG_DV)),
        'w_out': nrm(ks[10], (DEPTH, MIX_WIDTH, D_MODEL), MIX_WIDTH ** -0.5),
        'norm_post_mix': gain(ks[11], (DEPTH, D_MODEL)),
        'norm_pre_ffn': gain(ks[12], (DEPTH, D_MODEL)),
        'w_gate': nrm(ks[13], (DEPTH, D_MODEL, D_FF), D_MODEL ** -0.5),
        'w_up': nrm(ks[14], (DEPTH, D_MODEL, D_FF), D_MODEL ** -0.5),
        'w_down': nrm(ks[15], (DEPTH, D_FF, D_MODEL), D_FF ** -0.5),
        'norm_post_ffn': gain(ks[16], (DEPTH, D_MODEL)),
    }


def reference(x_prompt, x_sample, cache_win_k, cache_win_v, state_hgrn, norm_pre_mix, w_in,
              hg_lb_logits, attn_out_gain, hg_norm_gain, w_out, norm_post_mix, norm_pre_ffn,
              w_gate, w_up, w_down, norm_post_ffn):
    lb_all = jnp.cumsum(jax.nn.softmax(hg_lb_logits.astype(jnp.float32), axis=0), axis=0)
    yp, ys = x_prompt, x_sample
    kp, vp, sp, ksm, vsm, ssm = [], [], [], [], [], []
    for l in range(DEPTH):
        lb = lb_all[l].reshape(HG_HEADS, HG_DK)
        params = (norm_pre_mix[l], w_in[l], attn_out_gain[l], hg_norm_gain[l], w_out[l],
                  norm_post_mix[l], norm_pre_ffn[l], w_gate[l], w_up[l], w_down[l], norm_post_ffn[l])
        s0 = jnp.zeros((x_prompt.shape[0], HG_HEADS, HG_DK, HG_DV), jnp.float32)
        yp, (k1, v1), s1 = decoder_layer(
            yp, lb, *params, attend=prompt_attend,
            recur=functools.partial(hgrn_chunked, s0=s0, chunk=HG_CHUNK))
        ys, (k2, v2), s2 = decoder_layer(
            ys, lb, *params,
            attend=functools.partial(sample_attend, cache_k=cache_win_k[l], cache_v=cache_win_v[l]),
            recur=functools.partial(hgrn_chunked, s0=state_hgrn[l], chunk=x_sample.shape[1]))
        kp.append(k1); vp.append(v1); sp.append(s1)
        ksm.append(k2); vsm.append(v2); ssm.append(s2)
    return (yp, ys, jnp.stack(kp), jnp.stack(vp), jnp.stack(sp), jnp.stack(ksm), jnp.stack(vsm), jnp.stack(ssm))
```

```python
import functools

import jax
import jax.numpy as jnp
import numpy as np
from jax import lax
from jax.experimental import pallas as pl
from jax.experimental.pallas import tpu as pltpu

F32 = jnp.float32
BF16 = jnp.bfloat16

EPS = 1e-6
HEAD_DIM = 128
N_HEADS = 8
GROUP_WIDTH = N_HEADS * HEAD_DIM
N_PROJ_GROUPS = 7
PATTERNS = ((128, 1), (512, 4), (2048, 16))
BAND = 128
NEG_BIG = -1e30
HG_BLOCK = 16
VMEM_LIMIT = 52 * 1024 * 1024


def _params(*sem):
    return pltpu.CompilerParams(dimension_semantics=sem, vmem_limit_bytes=VMEM_LIMIT)


def _rms(x, gain):
    ms = jnp.mean(x * x, axis=-1, keepdims=True)
    return x * lax.rsqrt(ms + EPS) * gain


def _dot_nt(a, b):
    return lax.dot_general(a, b, (((1,), (1,)), ((), ())), preferred_element_type=F32)


def _dot_tn(a, b):
    return lax.dot_general(a, b, (((0,), (0,)), ((), ())), preferred_element_type=F32)


def _inproj_kernel(dils, x_ref, g_ref, w_ref, o_ref, *rest):
    dec_refs, h_scr = rest[:len(dils)], rest[len(dils)]
    j = pl.program_id(1)
    tm = x_ref.shape[0]

    @pl.when(j == 0)
    def _():
        h_scr[...] = _rms(x_ref[...], g_ref[...]).astype(BF16)

    res = jnp.dot(h_scr[...], w_ref[...], preferred_element_type=F32)
    o_ref[...] = res

    if dils:
        head_scr = rest[len(dils) + 1]

        @pl.when(j < 3)
        def _():
            for h in range(N_HEADS):
                head_scr[h] = res[:, h * HEAD_DIM:(h + 1) * HEAD_DIM]
            for dil, dec_ref in zip(dils, dec_refs):
                for r in range(dil):
                    for h in range(N_HEADS):
                        dec_ref[r, :, h * HEAD_DIM:(h + 1) * HEAD_DIM] = (
                            head_scr[h, pl.ds(r, tm // dil, stride=dil), :].astype(BF16))


def _inproj(x, gain, w_bf16, tm, dils=()):
    t, d = x.shape
    n = w_bf16.shape[1]
    tn = GROUP_WIDTH
    out_specs = [pl.BlockSpec((tm, tn), lambda i, j: (i, j))]
    out_shape = [jax.ShapeDtypeStruct((t, n), F32)]
    for dil in dils:
        assert tm % (16 * dil) == 0
        out_specs.append(pl.BlockSpec((dil, tm // dil, tn), lambda i, j: (0, i, jnp.minimum(j, 2))))
        out_shape.append(jax.ShapeDtypeStruct((dil, t // dil, 3 * tn), BF16))
    return pl.pallas_call(
        functools.partial(_inproj_kernel, tuple(dils)),
        grid=(t // tm, n // tn),
        in_specs=[
            pl.BlockSpec((tm, d), lambda i, j: (i, 0)),
            pl.BlockSpec((1, d), lambda i, j: (0, 0)),
            pl.BlockSpec((d, tn), lambda i, j: (0, j)),
        ],
        out_specs=out_specs,
        out_shape=out_shape,
        scratch_shapes=[pltpu.VMEM((tm, d), BF16)]
        + ([pltpu.VMEM((N_HEADS, tm, HEAD_DIM), F32)] if dils else []),
        compiler_params=_params("parallel", "arbitrary"),
        name="inproj",
    )(x, gain.reshape(1, d), w_bf16)


def _band_attn_kernel(dil, q_ref, kp_ref, kc_ref, vp_ref, vc_ref, o_ref, lse_ref, o_scr):
    i = pl.program_id(0)
    r = pl.program_id(1)
    bq = q_ref.shape[0]
    row = lax.broadcasted_iota(jnp.int32, (bq, 2 * bq), 0)
    col = lax.broadcasted_iota(jnp.int32, (bq, 2 * bq), 1)
    dist = row + bq - col
    lo = jnp.where(i == 0, bq, 0)
    valid = (dist >= 0) & (dist <= BAND) & (col >= lo)
    lane = lax.broadcasted_iota(jnp.int32, (bq, HEAD_DIM), 1)
    scale = HEAD_DIM ** -0.5
    lse_tile = jnp.zeros((bq, HEAD_DIM), F32)
    for h in range(N_HEADS):
        sl = slice(h * HEAD_DIM, (h + 1) * HEAD_DIM)
        q = q_ref[:, sl].astype(BF16)
        k = jnp.concatenate([kp_ref[:, sl], kc_ref[:, sl]], axis=0).astype(BF16)
        v = jnp.concatenate([vp_ref[:, sl], vc_ref[:, sl]], axis=0).astype(BF16)
        s = _dot_nt(q, k) * scale
        s = jnp.where(valid, s, NEG_BIG)
        m = jnp.max(s, axis=-1, keepdims=True)
        p = jnp.exp(s - m)
        l = jnp.sum(p, axis=-1, keepdims=True)
        o = jnp.dot(p.astype(BF16), v, preferred_element_type=F32)
        o_scr[h] = o * (1.0 / l)
        lse_tile = jnp.where(lane == h, m + jnp.log(l), lse_tile)
    if dil == 1:
        o_ref[...] = o_scr[...]
        lse_ref[...] = lse_tile
    else:
        for rr in range(dil):
            @pl.when(r == rr)
            def _():
                for h in range(N_HEADS):
                    o_ref[h, pl.ds(rr, bq, stride=dil), :] = o_scr[h]
                lse_ref[pl.ds(rr, bq, stride=dil), :] = lse_tile


def _band_attn(qkv, dil):
    rows = qkv.shape[1]
    t = rows * dil
    bq = BAND
    assert qkv.shape[0] == dil and rows % bq == 0
    blk = (None, bq, GROUP_WIDTH)
    cur = lambda g: (lambda i, r: (r, i, g))
    prev = lambda g: (lambda i, r: (r, jnp.maximum(i - 1, 0), g))
    return pl.pallas_call(
        functools.partial(_band_attn_kernel, dil),
        grid=(rows // bq, dil),
        in_specs=[
            pl.BlockSpec(blk, cur(0)),
            pl.BlockSpec(blk, prev(1)),
            pl.BlockSpec(blk, cur(1)),
            pl.BlockSpec(blk, prev(2)),
            pl.BlockSpec(blk, cur(2)),
        ],
        out_specs=[
            pl.BlockSpec((N_HEADS, dil * bq, HEAD_DIM), lambda i, r: (0, i, 0)),
            pl.BlockSpec((dil * bq, HEAD_DIM), lambda i, r: (i, 0)),
        ],
        out_shape=[
            jax.ShapeDtypeStruct((N_HEADS, t, HEAD_DIM), F32),
            jax.ShapeDtypeStruct((t, HEAD_DIM), F32),
        ],
        scratch_shapes=[pltpu.VMEM((N_HEADS, bq, HEAD_DIM), F32)],
        compiler_params=_params("parallel", "arbitrary"),
        name=f"band_attn_d{dil}",
    )(qkv, qkv, qkv, qkv, qkv)


def _sample_attn_kernel(q_ref, kn_ref, vn_ref, *rest):
    n_pat = len(PATTERNS)
    cache_refs = rest[: 2 * n_pat]
    o_ref = rest[2 * n_pat]
    scale = HEAD_DIM ** -0.5
    q = q_ref[0]
    kn = kn_ref[0]
    vn = vn_ref[0]
    s_new = jnp.sum(q * kn, axis=-1, keepdims=True) * scale
    outs, lses = [], []
    for p in range(n_pat):
        kc = cache_refs[2 * p][...]
        vc = cache_refs[2 * p + 1][...]
        s = jnp.sum(kc * q[None], axis=-1, keepdims=True) * scale
        m = jnp.maximum(jnp.max(s, axis=0), s_new)
        e = jnp.exp(s - m[None])
        e_new = jnp.exp(s_new - m)
        l = jnp.sum(e, axis=0) + e_new
        o = jnp.sum(e * vc, axis=0) + e_new * vn
        outs.append(o * (1.0 / l))
        lses.append(m + jnp.log(l))
    top = functools.reduce(jnp.maximum, lses)
    ws = [jnp.exp(x - top) for x in lses]
    den = functools.reduce(lambda a, b: a + b, ws)
    o_ref[0] = functools.reduce(lambda a, b: a + b, [w * o for w, o in zip(ws, outs)]) * (1.0 / den)


def _sample_attn(proj_s, cache_k, cache_v):
    b = proj_s.shape[0]
    past = cache_k.shape[1]
    assert past == max(w for w, _ in PATTERNS), "every dilated key must lie inside the cached window"
    head_blk = (1, N_HEADS, HEAD_DIM)
    in_specs = [pl.BlockSpec(head_blk, lambda i: (i, 0, 0)) for _ in range(3)]
    args = [proj_s[:, g * GROUP_WIDTH:(g + 1) * GROUP_WIDTH].reshape(b, N_HEADS, HEAD_DIM)
            for g in range(3)]
    for _, dil in PATTERNS:
        rows = past // dil
        last_blk = rows // BAND - 1
        for c in (cache_k, cache_v):
            args.append(c.reshape(b, rows, dil, N_HEADS, HEAD_DIM))
            in_specs.append(pl.BlockSpec((None, BAND, None, N_HEADS, HEAD_DIM),
                                         lambda i, lb=last_blk: (i, lb, 0, 0, 0)))
    out = pl.pallas_call(
        _sample_attn_kernel,
        grid=(b,),
        in_specs=in_specs,
        out_specs=pl.BlockSpec(head_blk, lambda i: (i, 0, 0)),
        out_shape=jax.ShapeDtypeStruct((b, N_HEADS, HEAD_DIM), F32),
        compiler_params=_params("parallel"),
        name="sample_attn",
    )(*args)
    return out.reshape(b, GROUP_WIDTH)


def _lower_bound(logits, layer):
    top = jnp.max(logits, axis=0, keepdims=True)
    e = jnp.exp(logits - top)
    return jnp.sum(e[: layer + 1], axis=0, keepdims=True) / jnp.sum(e, axis=0, keepdims=True)


def _split3(x):
    a = x.astype(BF16)
    r = x - a.astype(F32)
    b = r.astype(BF16)
    c = (r - b.astype(F32)).astype(BF16)
    return a, b, c


def _hgrn_kernel(layer, q_ref, f_ref, v_ref, logit_ref, tri_ref, o_ref, state_ref,
                 st_scr, qq_scr, kk_scr, cum_scr, qt_scr, kd_scr):
    step = pl.program_id(0)
    rows = q_ref.shape[0]

    @pl.when(step == 0)
    def _():
        st_scr[...] = jnp.zeros_like(st_scr)

    lb = _lower_bound(logit_ref[...], layer)
    f = lb + (1.0 - lb) * jax.nn.sigmoid(f_ref[...])
    g = jnp.log(f)
    qq = jax.nn.silu(q_ref[...])
    kk = 1.0 - f
    tri = tri_ref[...]
    sums = functools.reduce(
        lambda a, b: a + b,
        [jnp.dot(tri, piece, preferred_element_type=F32) for piece in _split3(g)])
    cum = sums[:rows]
    rev = sums[rows:]
    cum_scr[...] = cum
    qq_scr[...] = qq
    kk_scr[...] = kk
    qt_scr[...] = qq * jnp.exp(cum)
    kd_scr[...] = kk * jnp.exp(rev)

    half = HG_BLOCK // 2
    t_loc = lax.broadcasted_iota(jnp.int32, (half, HEAD_DIM), 0)

    def block(b, carry):
        base = pl.multiple_of(b * HG_BLOCK, HG_BLOCK)
        for h in range(N_HEADS):
            sl = slice(h * HEAD_DIM, (h + 1) * HEAD_DIM)
            st = st_scr[h]
            inter = _dot_nt(qt_scr[pl.ds(base, HG_BLOCK), sl].astype(BF16), st.astype(BF16))
            q_b = qq_scr[pl.ds(base, HG_BLOCK), sl]
            k_b = kk_scr[pl.ds(base, HG_BLOCK), sl]
            c_b = cum_scr[pl.ds(base, HG_BLOCK), sl]
            v_b = v_ref[pl.ds(base, HG_BLOCK), sl]
            for part in range(2):
                rows_t = slice(part * half, (part + 1) * half)
                q_t = q_b[rows_t]
                c_t = c_b[rows_t]
                acc = inter[rows_t]
                for s in range((part + 1) * half):
                    arg = c_t - c_b[s:s + 1]
                    if s >= part * half:
                        arg = jnp.where(t_loc >= s - part * half, arg, NEG_BIG)
                    a = jnp.sum(q_t * k_b[s:s + 1] * jnp.exp(arg), axis=-1, keepdims=True)
                    acc = acc + a * v_b[s:s + 1]
                o_ref[pl.ds(base + part * half, half), sl] = acc
            decay = jnp.exp(c_b[HG_BLOCK - 1:HG_BLOCK])
            delta = _dot_tn(v_b.astype(BF16), kd_scr[pl.ds(base, HG_BLOCK), sl].astype(BF16))
            st_scr[h] = st * decay + delta
        return carry

    lax.fori_loop(0, rows // HG_BLOCK, block, 0)

    @pl.when(step == pl.num_programs(0) - 1)
    def _():
        for h in range(N_HEADS):
            state_ref[h] = st_scr[h].T


def _block_tri(rows):
    idx = np.arange(rows)
    same = (idx[:, None] // HG_BLOCK) == (idx[None, :] // HG_BLOCK)
    lower = same & (idx[None, :] <= idx[:, None])
    upper = same & (idx[None, :] > idx[:, None])
    return jnp.asarray(np.concatenate([lower, upper], axis=0), dtype=BF16)


def _hgrn_prompt(proj, logits, layer, rows):
    t = proj.shape[0]
    assert t % rows == 0 and rows % HG_BLOCK == 0
    col = lambda g: pl.BlockSpec((rows, GROUP_WIDTH), lambda i, g=g: (i, g))
    n_l = logits.shape[0]
    return pl.pallas_call(
        functools.partial(_hgrn_kernel, layer),
        grid=(t // rows,),
        in_specs=[
            col(3), col(4), col(5),
            pl.BlockSpec((n_l, GROUP_WIDTH), lambda i: (0, 0)),
            pl.BlockSpec((2 * rows, rows), lambda i: (0, 0)),
        ],
        out_specs=[
            pl.BlockSpec((rows, GROUP_WIDTH), lambda i: (i, 0)),
            pl.BlockSpec((N_HEADS, HEAD_DIM, HEAD_DIM), lambda i: (0, 0, 0)),
        ],
        out_shape=[
            jax.ShapeDtypeStruct((t, GROUP_WIDTH), F32),
            jax.ShapeDtypeStruct((N_HEADS, HEAD_DIM, HEAD_DIM), F32),
        ],
        scratch_shapes=[pltpu.VMEM((N_HEADS, HEAD_DIM, HEAD_DIM), F32)]
        + [pltpu.VMEM((rows, GROUP_WIDTH), F32) for _ in range(5)],
        compiler_params=_params("arbitrary"),
        name="hgrn_prompt",
    )(proj, proj, proj, logits, _block_tri(rows))


def _hgrn_step_kernel(layer, q_ref, f_ref, v_ref, logit_ref, s_ref, o_ref, snew_ref):
    lb = _lower_bound(logit_ref[...], layer)
    f = lb + (1.0 - lb) * jax.nn.sigmoid(f_ref[0])
    qq = jax.nn.silu(q_ref[0])
    vv = v_ref[0]
    eye = (lax.broadcasted_iota(jnp.int32, (HEAD_DIM, HEAD_DIM), 0)
           == lax.broadcasted_iota(jnp.int32, (HEAD_DIM, HEAD_DIM), 1))

    def column(row):
        return jnp.sum(jnp.where(eye, jnp.broadcast_to(row, (HEAD_DIM, HEAD_DIM)), 0.0),
                       axis=-1, keepdims=True)

    for h in range(N_HEADS):
        sl = slice(h * HEAD_DIM, (h + 1) * HEAD_DIM)
        f_col = column(f[:, sl])
        q_col = column(qq[:, sl])
        s_new = f_col * s_ref[0, h] + (1.0 - f_col) * vv[:, sl]
        snew_ref[0, h] = s_new
        o_ref[0, :, sl] = jnp.sum(q_col * s_new, axis=0, keepdims=True)


def _hgrn_step(proj_s, logits, layer, state):
    b = proj_s.shape[0]
    p3 = proj_s.reshape(b, 1, N_PROJ_GROUPS * GROUP_WIDTH)
    row = (1, 1, GROUP_WIDTH)
    n_l = logits.shape[0]
    st_blk = (1, N_HEADS, HEAD_DIM, HEAD_DIM)
    o, s_new = pl.pallas_call(
        functools.partial(_hgrn_step_kernel, layer),
        grid=(b,),
        in_specs=[pl.BlockSpec(row, lambda i, g=g: (i, 0, g)) for g in (3, 4, 5)]
        + [pl.BlockSpec((n_l, GROUP_WIDTH), lambda i: (0, 0)),
           pl.BlockSpec(st_blk, lambda i: (i, 0, 0, 0))],
        out_specs=[pl.BlockSpec(row, lambda i: (i, 0, 0)),
                   pl.BlockSpec(st_blk, lambda i: (i, 0, 0, 0))],
        out_shape=[jax.ShapeDtypeStruct((b, 1, GROUP_WIDTH), F32),
                   jax.ShapeDtypeStruct(state.shape, F32)],
        compiler_params=_params("parallel"),
        name="hgrn_step",
    )(p3, p3, p3, logits, state)
    return o.reshape(b, GROUP_WIDTH), s_new


def _merge_kernel(n_pat, *refs):
    o_refs = refs[:n_pat]
    lse_refs = refs[n_pat:2 * n_pat] if n_pat > 1 else ()
    k = 2 * n_pat if n_pat > 1 else n_pat
    (hg_ref, gate_ref, x_ref, again_ref, hgain_ref, wout_ref, npost_ref, npre_ref,
     x1_ref, h2_ref) = refs[k:]
    tm = x_ref.shape[0]

    if n_pat > 1:
        lses = [r[...] for r in lse_refs]
        top = functools.reduce(jnp.maximum, lses)
        ws = [jnp.exp(x - top) for x in lses]
        inv = 1.0 / functools.reduce(lambda a, b: a + b, ws)
        ws = [w * inv for w in ws]

    att, hg = [], []
    sq = jnp.zeros((tm, 1), F32)
    for h in range(N_HEADS):
        sl = slice(h * HEAD_DIM, (h + 1) * HEAD_DIM)
        if n_pat > 1:
            a = functools.reduce(
                lambda x, y: x + y,
                [w[:, h:h + 1] * r[h] for w, r in zip(ws, o_refs)])
        else:
            a = o_refs[0][:, sl]
        att.append(a)
        sq = sq + jnp.sum(a * a, axis=-1, keepdims=True)
        g = gate_ref[:, sl]
        hg.append((_rms(hg_ref[:, sl], hgain_ref[...]) * (g * jax.nn.sigmoid(g))).astype(BF16))
    inv_rms = lax.rsqrt(sq * (1.0 / GROUP_WIDTH) + EPS)
    att = [(a * inv_rms * again_ref[:, h * HEAD_DIM:(h + 1) * HEAD_DIM]).astype(BF16)
           for h, a in enumerate(att)]
    merged = jnp.concatenate(att + hg, axis=-1)
    mix = jnp.dot(merged, wout_ref[...], preferred_element_type=F32)
    x1 = x_ref[...] + _rms(mix, npost_ref[...])
    x1_ref[...] = x1
    h2_ref[...] = _rms(x1, npre_ref[...]).astype(BF16)


def _merge(o_list, lse_list, o_hg, proj, x, attn_gain, hg_gain, w_out_bf16, n_post, n_pre, tm):
    t, d = x.shape
    n_pat = len(o_list)
    row = lambda w: pl.BlockSpec((tm, w), lambda i: (i, 0))
    full = lambda a: pl.BlockSpec(a.shape, lambda i: (0,) * a.ndim)
    args = list(o_list)
    if n_pat > 1:
        in_specs = [pl.BlockSpec((N_HEADS, tm, HEAD_DIM), lambda i: (0, i, 0)) for _ in o_list]
        args += list(lse_list)
        in_specs += [row(HEAD_DIM) for _ in lse_list]
    else:
        in_specs = [row(GROUP_WIDTH)]
    small = [attn_gain.reshape(1, -1), hg_gain.reshape(1, -1), w_out_bf16,
             n_post.reshape(1, -1), n_pre.reshape(1, -1)]
    args += [o_hg, proj, x] + small
    in_specs += [row(GROUP_WIDTH), pl.BlockSpec((tm, GROUP_WIDTH), lambda i: (i, 6)), row(d)]
    in_specs += [full(a) for a in small]
    return pl.pallas_call(
        functools.partial(_merge_kernel, n_pat),
        grid=(t // tm,),
        in_specs=in_specs,
        out_specs=[row(d), row(d)],
        out_shape=[jax.ShapeDtypeStruct((t, d), F32), jax.ShapeDtypeStruct((t, d), BF16)],
        compiler_params=_params("parallel"),
        name="merge_outproj",
    )(*args)


def _ffn_kernel(h_ref, x1_ref, wg_ref, wu_ref, wd_ref, npost_ref, y_ref, acc_scr):
    j = pl.program_id(1)
    h = h_ref[...]
    gate = jnp.dot(h, wg_ref[...], preferred_element_type=F32)
    up = jnp.dot(h, wu_ref[...], preferred_element_type=F32)
    act = (gate * jax.nn.sigmoid(gate) * up).astype(BF16)
    part = jnp.dot(act, wd_ref[...], preferred_element_type=F32)

    @pl.when(j == 0)
    def _():
        acc_scr[...] = part

    @pl.when(j > 0)
    def _():
        acc_scr[...] += part

    @pl.when(j == pl.num_programs(1) - 1)
    def _():
        y_ref[...] = x1_ref[...] + _rms(acc_scr[...], npost_ref[...])


def _ffn(h2, x1, wg, wu, wd, n_post, tm, tf):
    t, d = x1.shape
    dff = wg.shape[1]
    assert dff % tf == 0
    return pl.pallas_call(
        _ffn_kernel,
        grid=(t // tm, dff // tf),
        in_specs=[
            pl.BlockSpec((tm, d), lambda i, j: (i, 0)),
            pl.BlockSpec((tm, d), lambda i, j: (i, 0)),
            pl.BlockSpec((d, tf), lambda i, j: (0, j)),
            pl.BlockSpec((d, tf), lambda i, j: (0, j)),
            pl.BlockSpec((tf, d), lambda i, j: (j, 0)),
            pl.BlockSpec((1, d), lambda i, j: (0, 0)),
        ],
        out_specs=pl.BlockSpec((tm, d), lambda i, j: (i, 0)),
        out_shape=jax.ShapeDtypeStruct((t, d), F32),
        scratch_shapes=[pltpu.VMEM((tm, d), F32)],
        compiler_params=_params("parallel", "arbitrary"),
        name="ffn",
    )(h2, x1, wg, wu, wd, n_post.reshape(1, d))


def _row_tile(t, want):
    tm = min(t, want)
    assert t % tm == 0
    return tm


def kernel(x_prompt, x_sample, cache_win_k, cache_win_v, state_hgrn, norm_pre_mix, w_in,
           hg_lb_logits, attn_out_gain, hg_norm_gain, w_out, norm_post_mix, norm_pre_ffn,
           w_gate, w_up, w_down, norm_post_ffn):
    depth = w_in.shape[0]
    bp, t, d = x_prompt.shape
    bs, ts, _ = x_sample.shape
    assert bp == 1 and ts == 1
    keep = min(max(w for w, _ in PATTERNS), t)
    logits = hg_lb_logits.astype(F32)

    yp = x_prompt.reshape(t, d)
    ys = x_sample.reshape(bs, d)
    outs = [[] for _ in range(6)]
    for l in range(depth):
        w_in_b, w_out_b = w_in[l].astype(BF16), w_out[l].astype(BF16)
        wg_b, wu_b, wd_b = w_gate[l].astype(BF16), w_up[l].astype(BF16), w_down[l].astype(BF16)

        dils = tuple(dil for _, dil in PATTERNS if dil > 1)
        proj, *dec = _inproj(yp, norm_pre_mix[l], w_in_b, _row_tile(t, 512), dils)
        qkv = {1: proj.reshape(1, t, -1), **dict(zip(dils, dec))}
        o_list, lse_list = zip(*[_band_attn(qkv[dil], dil) for _, dil in PATTERNS])
        o_hg, s_fin = _hgrn_prompt(proj, logits, l, _row_tile(t, 256))
        x1, h2 = _merge(o_list, lse_list, o_hg, proj, yp, attn_out_gain[l], hg_norm_gain[l],
                        w_out_b, norm_post_mix[l], norm_pre_ffn[l], _row_tile(t, 256))
        yp = _ffn(h2, x1, wg_b, wu_b, wd_b, norm_post_ffn[l], _row_tile(t, 512), 512)
        outs[0].append(proj[t - keep:, GROUP_WIDTH:2 * GROUP_WIDTH].reshape(1, keep, N_HEADS, HEAD_DIM))
        outs[1].append(proj[t - keep:, 2 * GROUP_WIDTH:3 * GROUP_WIDTH].reshape(1, keep, N_HEADS, HEAD_DIM))
        outs[2].append(s_fin.reshape(1, N_HEADS, HEAD_DIM, HEAD_DIM))

        (proj_s,) = _inproj(ys, norm_pre_mix[l], w_in_b, bs)
        o_att_s = _sample_attn(proj_s, cache_win_k[l], cache_win_v[l])
        o_hg_s, s_new = _hgrn_step(proj_s, logits, l, state_hgrn[l])
        x1s, h2s = _merge([o_att_s], [], o_hg_s, proj_s, ys, attn_out_gain[l], hg_norm_gain[l],
                          w_out_b, norm_post_mix[l], norm_pre_ffn[l], bs)
        ys = _ffn(h2s, x1s, wg_b, wu_b, wd_b, norm_post_ffn[l], bs, 512)
        outs[3].append(proj_s[:, GROUP_WIDTH:2 * GROUP_WIDTH].reshape(bs, 1, N_HEADS, HEAD_DIM))
        outs[4].append(proj_s[:, 2 * GROUP_WIDTH:3 * GROUP_WIDTH].reshape(bs, 1, N_HEADS, HEAD_DIM))
        outs[5].append(s_new)

    st = lambda xs: jnp.stack(xs)
    return (yp.reshape(1, t, d), ys.reshape(bs, 1, d), st(outs[0]), st(outs[1]), st(outs[2]),
            st(outs[3]), st(outs[4]), st(outs[5]))
```

```python
import functools

import jax
import jax.numpy as jnp
import numpy as np
from jax import lax
from jax.experimental import pallas as pl
from jax.experimental.pallas import tpu as pltpu

F32 = jnp.float32
BF16 = jnp.bfloat16

EPS = 1e-6
HEAD_DIM = 128
N_HEADS = 8
GROUP_WIDTH = N_HEADS * HEAD_DIM
N_PROJ_GROUPS = 7
PATTERNS = ((128, 1), (512, 4), (2048, 16))
BAND = 128
NEG_BIG = -1e30
LOG2E = 1.4426950408889634
HG_BLOCK = 16
MXU_COLS = 256
FFN_CHUNK = MXU_COLS
VMEM_LIMIT = 52 * 1024 * 1024


def _params(*sem):
    return pltpu.CompilerParams(dimension_semantics=sem, vmem_limit_bytes=VMEM_LIMIT)


def _rms(x, gain):
    ms = jnp.mean(x * x, axis=-1, keepdims=True)
    return x * lax.rsqrt(ms + EPS) * gain


def _dot_nt(a, b):
    return lax.dot_general(a, b, (((1,), (1,)), ((), ())), preferred_element_type=F32)


def _dot_tn(a, b):
    return lax.dot_general(a, b, (((0,), (0,)), ((), ())), preferred_element_type=F32)


def _inproj_kernel(dils, x_ref, g_ref, w_ref, o_ref, *rest):
    dec_refs, h_scr = rest[:len(dils)], rest[len(dils)]
    j = pl.program_id(1)
    tm = x_ref.shape[0]

    @pl.when(j == 0)
    def _():
        h_scr[...] = _rms(x_ref[...], g_ref[...]).astype(BF16)

    if not dils:
        o_ref[...] = jnp.dot(h_scr[...], w_ref[...], preferred_element_type=F32)
        return

    head_scr = rest[len(dils) + 1]

    @pl.when(j >= 3)
    def _():
        o_ref[...] = jnp.dot(h_scr[...], w_ref[...], preferred_element_type=F32)

    @pl.when(j < 3)
    def _():
        h = h_scr[...]
        for c in range(0, w_ref.shape[1], MXU_COLS):
            res = jnp.dot(h, w_ref[:, c:c + MXU_COLS], preferred_element_type=F32)
            o_ref[:, c:c + MXU_COLS] = res
            for hh in range(c // HEAD_DIM, (c + MXU_COLS) // HEAD_DIM):
                sl = slice(hh * HEAD_DIM, (hh + 1) * HEAD_DIM)
                head_res = res[:, hh * HEAD_DIM - c:(hh + 1) * HEAD_DIM - c]
                head_scr[hh] = head_res
                for dil, dec_ref in zip(dils, dec_refs):
                    if dil == 1:
                        dec_ref[0, :, sl] = head_res.astype(BF16)
                        continue
                    for r in range(dil):
                        dec_ref[r, :, sl] = head_scr[hh, pl.ds(r, tm // dil, stride=dil), :].astype(BF16)


def _inproj(x, gain, w_bf16, tm, dils=()):
    t, d = x.shape
    n = w_bf16.shape[1]
    tn = GROUP_WIDTH
    out_specs = [pl.BlockSpec((tm, tn), lambda i, j: (i, j))]
    out_shape = [jax.ShapeDtypeStruct((t, n), F32)]
    for dil in dils:
        assert tm % (16 * dil) == 0
        out_specs.append(pl.BlockSpec((dil, tm // dil, tn), lambda i, j: (0, i, jnp.minimum(j, 2))))
        out_shape.append(jax.ShapeDtypeStruct((dil, t // dil, 3 * tn), BF16))
    return pl.pallas_call(
        functools.partial(_inproj_kernel, tuple(dils)),
        grid=(t // tm, n // tn),
        in_specs=[
            pl.BlockSpec((tm, d), lambda i, j: (i, 0)),
            pl.BlockSpec((1, d), lambda i, j: (0, 0)),
            pl.BlockSpec((d, tn), lambda i, j: (0, j)),
        ],
        out_specs=out_specs,
        out_shape=out_shape,
        scratch_shapes=[pltpu.VMEM((tm, d), BF16)]
        + ([pltpu.VMEM((N_HEADS, tm, HEAD_DIM), F32)] if dils else []),
        compiler_params=_params("parallel", "arbitrary"),
        name="inproj",
    )(x, gain.reshape(1, d), w_bf16)


def _band_attn_kernel(dil, q_ref, k_ref, v_ref, o_ref, lse_ref, o_scr, kprev_scr, vprev_scr):
    i = pl.program_id(0)
    r = pl.program_id(1)
    bq = q_ref.shape[0]

    @pl.when(i == 0)
    def _():
        kprev_scr[r] = jnp.zeros(kprev_scr.shape[1:], kprev_scr.dtype)
        vprev_scr[r] = jnp.zeros(vprev_scr.shape[1:], vprev_scr.dtype)

    row = lax.broadcasted_iota(jnp.int32, (bq, 2 * bq), 0)
    col = lax.broadcasted_iota(jnp.int32, (bq, 2 * bq), 1)
    dist = row + bq - col
    lo = jnp.where(i == 0, bq, 0)
    valid = (dist >= 0) & (dist <= BAND) & (col >= lo)
    lane = lax.broadcasted_iota(jnp.int32, (bq, HEAD_DIM), 1)
    scale = HEAD_DIM ** -0.5
    lse_tile = jnp.zeros((bq, HEAD_DIM), F32)
    for h in range(N_HEADS):
        sl = slice(h * HEAD_DIM, (h + 1) * HEAD_DIM)
        q = q_ref[:, sl]
        k = jnp.concatenate([kprev_scr[r, :, sl], k_ref[:, sl]], axis=0)
        v = jnp.concatenate([vprev_scr[r, :, sl], v_ref[:, sl]], axis=0)
        s = _dot_nt(q, k) * scale
        s = jnp.where(valid, s, NEG_BIG)
        m = jnp.max(s, axis=-1, keepdims=True)
        p = jnp.exp(s - m)
        l = jnp.sum(p, axis=-1, keepdims=True)
        o = jnp.dot(p.astype(BF16), v, preferred_element_type=F32)
        o_scr[h] = o * (1.0 / l)
        lse_tile = jnp.where(lane == h, m + jnp.log(l), lse_tile)
    kprev_scr[r] = k_ref[...]
    vprev_scr[r] = v_ref[...]
    if dil == 1:
        o_ref[...] = o_scr[...]
        lse_ref[...] = lse_tile
    else:
        for rr in range(dil):
            @pl.when(r == rr)
            def _():
                for h in range(N_HEADS):
                    o_ref[h, pl.ds(rr, bq, stride=dil), :] = o_scr[h]
                lse_ref[pl.ds(rr, bq, stride=dil), :] = lse_tile


def _band_attn(qkv, dil):
    rows = qkv.shape[1]
    t = rows * dil
    bq = BAND
    assert qkv.shape[0] == dil and rows % bq == 0
    blk = (None, bq, GROUP_WIDTH)
    cur = lambda g: (lambda i, r: (r, i, g))
    carry = pltpu.VMEM((dil, bq, GROUP_WIDTH), qkv.dtype)
    return pl.pallas_call(
        functools.partial(_band_attn_kernel, dil),
        grid=(rows // bq, dil),
        in_specs=[pl.BlockSpec(blk, cur(g)) for g in range(3)],
        out_specs=[
            pl.BlockSpec((N_HEADS, dil * bq, HEAD_DIM), lambda i, r: (0, i, 0)),
            pl.BlockSpec((dil * bq, HEAD_DIM), lambda i, r: (i, 0)),
        ],
        out_shape=[
            jax.ShapeDtypeStruct((N_HEADS, t, HEAD_DIM), F32),
            jax.ShapeDtypeStruct((t, HEAD_DIM), F32),
        ],
        scratch_shapes=[pltpu.VMEM((N_HEADS, bq, HEAD_DIM), F32), carry, carry],
        compiler_params=_params("arbitrary", "arbitrary"),
        name=f"band_attn_d{dil}",
    )(qkv, qkv, qkv)


def _sample_attn_kernel(q_ref, kn_ref, vn_ref, *rest):
    n_pat = len(PATTERNS)
    cache_refs = rest[: 2 * n_pat]
    o_ref = rest[2 * n_pat]
    scale = HEAD_DIM ** -0.5
    q = q_ref[0]
    kn = kn_ref[0]
    vn = vn_ref[0]
    s_new = jnp.sum(q * kn, axis=-1, keepdims=True) * scale
    outs, lses = [], []
    for p in range(n_pat):
        kc = cache_refs[2 * p][...]
        vc = cache_refs[2 * p + 1][...]
        s = jnp.sum(kc * q[None], axis=-1, keepdims=True) * scale
        m = jnp.maximum(jnp.max(s, axis=0), s_new)
        e = jnp.exp(s - m[None])
        e_new = jnp.exp(s_new - m)
        l = jnp.sum(e, axis=0) + e_new
        o = jnp.sum(e * vc, axis=0) + e_new * vn
        outs.append(o * (1.0 / l))
        lses.append(m + jnp.log(l))
    top = functools.reduce(jnp.maximum, lses)
    ws = [jnp.exp(x - top) for x in lses]
    den = functools.reduce(lambda a, b: a + b, ws)
    o_ref[0] = functools.reduce(lambda a, b: a + b, [w * o for w, o in zip(ws, outs)]) * (1.0 / den)


def _sample_attn(proj_s, cache_k, cache_v):
    b = proj_s.shape[0]
    past = cache_k.shape[1]
    assert past == max(w for w, _ in PATTERNS), "every dilated key must lie inside the cached window"
    head_blk = (1, N_HEADS, HEAD_DIM)
    in_specs = [pl.BlockSpec(head_blk, lambda i: (i, 0, 0)) for _ in range(3)]
    args = [proj_s[:, g * GROUP_WIDTH:(g + 1) * GROUP_WIDTH].reshape(b, N_HEADS, HEAD_DIM)
            for g in range(3)]
    for _, dil in PATTERNS:
        rows = past // dil
        last_blk = rows // BAND - 1
        for c in (cache_k, cache_v):
            args.append(c.reshape(b, rows, dil, N_HEADS, HEAD_DIM))
            in_specs.append(pl.BlockSpec((None, BAND, None, N_HEADS, HEAD_DIM),
                                         lambda i, lb=last_blk: (i, lb, 0, 0, 0)))
    out = pl.pallas_call(
        _sample_attn_kernel,
        grid=(b,),
        in_specs=in_specs,
        out_specs=pl.BlockSpec(head_blk, lambda i: (i, 0, 0)),
        out_shape=jax.ShapeDtypeStruct((b, N_HEADS, HEAD_DIM), F32),
        compiler_params=_params("parallel"),
        name="sample_attn",
    )(*args)
    return out.reshape(b, GROUP_WIDTH)


def _lower_bound(logits, layer):
    top = jnp.max(logits, axis=0, keepdims=True)
    e = jnp.exp(logits - top)
    return jnp.sum(e[: layer + 1], axis=0, keepdims=True) / jnp.sum(e, axis=0, keepdims=True)


def _split3(x):
    a = x.astype(BF16)
    r = x - a.astype(F32)
    b = r.astype(BF16)
    c = (r - b.astype(F32)).astype(BF16)
    return a, b, c


def _hgrn_kernel(layer, q_ref, f_ref, v_ref, logit_ref, tri_ref, o_ref, state_ref,
                 st_scr, qq_scr, u2_scr, c2_scr, qt_scr, kd_scr):
    step = pl.program_id(0)
    rows = q_ref.shape[0]

    @pl.when(step == 0)
    def _():
        st_scr[...] = jnp.zeros_like(st_scr)

    lb = _lower_bound(logit_ref[...], layer)
    f = lb + (1.0 - lb) * jax.nn.sigmoid(f_ref[...])
    g = jnp.log(f)
    qq = jax.nn.silu(q_ref[...])
    kk = 1.0 - f
    tri = tri_ref[...]
    sums = functools.reduce(
        lambda a, b: a + b,
        [jnp.dot(tri, piece, preferred_element_type=F32) for piece in _split3(g)])
    cum = sums[:rows]
    rev = sums[rows:]
    c2 = cum * LOG2E
    c2_scr[...] = c2
    u2_scr[...] = jnp.log(kk) * LOG2E - c2
    qq_scr[...] = qq
    qt_scr[...] = qq * jnp.exp2(c2)
    kd_scr[...] = kk * jnp.exp(rev)

    half = HG_BLOCK // 2
    t_loc = lax.broadcasted_iota(jnp.int32, (half, HEAD_DIM), 0)
    lane = lax.broadcasted_iota(jnp.int32, (half, HEAD_DIM), 1)

    def block(b, carry):
        base = pl.multiple_of(b * HG_BLOCK, HG_BLOCK)
        for h in range(N_HEADS):
            sl = slice(h * HEAD_DIM, (h + 1) * HEAD_DIM)
            st = st_scr[h]
            inter = _dot_nt(qt_scr[pl.ds(base, HG_BLOCK), sl].astype(BF16), st.astype(BF16))
            q_b = qq_scr[pl.ds(base, HG_BLOCK), sl]
            c_b = c2_scr[pl.ds(base, HG_BLOCK), sl]
            u_b = u2_scr[pl.ds(base, HG_BLOCK), sl]
            v_b = v_ref[pl.ds(base, HG_BLOCK), sl]
            for part in range(2):
                rows_t = slice(part * half, (part + 1) * half)
                q_t = q_b[rows_t]
                c_t = c_b[rows_t]
                acc = inter[rows_t]
                for s in range((part + 1) * half):
                    arg = c_t + u_b[s:s + 1]
                    if s >= part * half:
                        arg = jnp.where(t_loc >= s - part * half, arg, NEG_BIG)
                    a = jnp.sum(q_t * jnp.exp2(arg), axis=-1, keepdims=True)
                    acc = acc + a * v_b[s:s + 1]
                o_ref[pl.ds(base + part * half, half), sl] = acc
            decay = jnp.exp2(c_b[HG_BLOCK - 1:HG_BLOCK])
            delta = _dot_tn(v_b.astype(BF16), kd_scr[pl.ds(base, HG_BLOCK), sl].astype(BF16))
            st_scr[h] = st * decay + delta
        return carry

    lax.fori_loop(0, rows // HG_BLOCK, block, 0)

    @pl.when(step == pl.num_programs(0) - 1)
    def _():
        for h in range(N_HEADS):
            state_ref[h] = st_scr[h].T


def _block_tri(rows):
    idx = np.arange(rows)
    same = (idx[:, None] // HG_BLOCK) == (idx[None, :] // HG_BLOCK)
    lower = same & (idx[None, :] <= idx[:, None])
    upper = same & (idx[None, :] > idx[:, None])
    return jnp.asarray(np.concatenate([lower, upper], axis=0), dtype=BF16)


def _hgrn_prompt(proj, logits, layer, rows):
    t = proj.shape[0]
    assert t % rows == 0 and rows % HG_BLOCK == 0
    col = lambda g: pl.BlockSpec((rows, GROUP_WIDTH), lambda i, g=g: (i, g))
    n_l = logits.shape[0]
    return pl.pallas_call(
        functools.partial(_hgrn_kernel, layer),
        grid=(t // rows,),
        in_specs=[
            col(3), col(4), col(5),
            pl.BlockSpec((n_l, GROUP_WIDTH), lambda i: (0, 0)),
            pl.BlockSpec((2 * rows, rows), lambda i: (0, 0)),
        ],
        out_specs=[
            pl.BlockSpec((rows, GROUP_WIDTH), lambda i: (i, 0)),
            pl.BlockSpec((N_HEADS, HEAD_DIM, HEAD_DIM), lambda i: (0, 0, 0)),
        ],
        out_shape=[
            jax.ShapeDtypeStruct((t, GROUP_WIDTH), F32),
            jax.ShapeDtypeStruct((N_HEADS, HEAD_DIM, HEAD_DIM), F32),
        ],
        scratch_shapes=[pltpu.VMEM((N_HEADS, HEAD_DIM, HEAD_DIM), F32)]
        + [pltpu.VMEM((rows, GROUP_WIDTH), F32) for _ in range(5)],
        compiler_params=_params("arbitrary"),
        name="hgrn_prompt",
    )(proj, proj, proj, logits, _block_tri(rows))


def _hgrn_step_kernel(layer, q_ref, f_ref, v_ref, logit_ref, s_ref, o_ref, snew_ref):
    lb = _lower_bound(logit_ref[...], layer)
    f = lb + (1.0 - lb) * jax.nn.sigmoid(f_ref[0])
    qq = jax.nn.silu(q_ref[0])
    vv = v_ref[0]
    eye = (lax.broadcasted_iota(jnp.int32, (HEAD_DIM, HEAD_DIM), 0)
           == lax.broadcasted_iota(jnp.int32, (HEAD_DIM, HEAD_DIM), 1))

    def column(row):
        return jnp.sum(jnp.where(eye, jnp.broadcast_to(row, (HEAD_DIM, HEAD_DIM)), 0.0),
                       axis=-1, keepdims=True)

    for h in range(N_HEADS):
        sl = slice(h * HEAD_DIM, (h + 1) * HEAD_DIM)
        f_col = column(f[:, sl])
        q_col = column(qq[:, sl])
        s_new = f_col * s_ref[0, h] + (1.0 - f_col) * vv[:, sl]
        snew_ref[0, h] = s_new
        o_ref[0, :, sl] = jnp.sum(q_col * s_new, axis=0, keepdims=True)


def _hgrn_step(proj_s, logits, layer, state):
    b = proj_s.shape[0]
    p3 = proj_s.reshape(b, 1, N_PROJ_GROUPS * GROUP_WIDTH)
    row = (1, 1, GROUP_WIDTH)
    n_l = logits.shape[0]
    st_blk = (1, N_HEADS, HEAD_DIM, HEAD_DIM)
    o, s_new = pl.pallas_call(
        functools.partial(_hgrn_step_kernel, layer),
        grid=(b,),
        in_specs=[pl.BlockSpec(row, lambda i, g=g: (i, 0, g)) for g in (3, 4, 5)]
        + [pl.BlockSpec((n_l, GROUP_WIDTH), lambda i: (0, 0)),
           pl.BlockSpec(st_blk, lambda i: (i, 0, 0, 0))],
        out_specs=[pl.BlockSpec(row, lambda i: (i, 0, 0)),
                   pl.BlockSpec(st_blk, lambda i: (i, 0, 0, 0))],
        out_shape=[jax.ShapeDtypeStruct((b, 1, GROUP_WIDTH), F32),
                   jax.ShapeDtypeStruct(state.shape, F32)],
        compiler_params=_params("parallel"),
        name="hgrn_step",
    )(p3, p3, p3, logits, state)
    return o.reshape(b, GROUP_WIDTH), s_new


def _merge_kernel(n_pat, *refs):
    o_refs = refs[:n_pat]
    lse_refs = refs[n_pat:2 * n_pat] if n_pat > 1 else ()
    k = 2 * n_pat if n_pat > 1 else n_pat
    (hg_ref, gate_ref, x_ref, again_ref, hgain_ref, wout_ref, npost_ref, npre_ref,
     x1_ref, h2_ref) = refs[k:]
    tm = x_ref.shape[0]

    if n_pat > 1:
        lses = [r[...] for r in lse_refs]
        top = functools.reduce(jnp.maximum, lses)
        ws = [jnp.exp(x - top) for x in lses]
        inv = 1.0 / functools.reduce(lambda a, b: a + b, ws)
        ws = [w * inv for w in ws]

    att, hg = [], []
    sq = jnp.zeros((tm, 1), F32)
    for h in range(N_HEADS):
        sl = slice(h * HEAD_DIM, (h + 1) * HEAD_DIM)
        if n_pat > 1:
            a = functools.reduce(
                lambda x, y: x + y,
                [w[:, h:h + 1] * r[h] for w, r in zip(ws, o_refs)])
        else:
            a = o_refs[0][:, sl]
        att.append(a)
        sq = sq + jnp.sum(a * a, axis=-1, keepdims=True)
        g = gate_ref[:, sl]
        hg.append((_rms(hg_ref[:, sl], hgain_ref[...]) * (g * jax.nn.sigmoid(g))).astype(BF16))
    inv_rms = lax.rsqrt(sq * (1.0 / GROUP_WIDTH) + EPS)
    att = [(a * inv_rms * again_ref[:, h * HEAD_DIM:(h + 1) * HEAD_DIM]).astype(BF16)
           for h, a in enumerate(att)]
    merged = jnp.concatenate(att + hg, axis=-1)
    mix = jnp.dot(merged, wout_ref[...], preferred_element_type=F32)
    x1 = x_ref[...] + _rms(mix, npost_ref[...])
    x1_ref[...] = x1
    h2_ref[...] = _rms(x1, npre_ref[...]).astype(BF16)


def _merge(o_list, lse_list, o_hg, proj, x, attn_gain, hg_gain, w_out_bf16, n_post, n_pre, tm):
    t, d = x.shape
    n_pat = len(o_list)
    row = lambda w: pl.BlockSpec((tm, w), lambda i: (i, 0))
    full = lambda a: pl.BlockSpec(a.shape, lambda i: (0,) * a.ndim)
    args = list(o_list)
    if n_pat > 1:
        in_specs = [pl.BlockSpec((N_HEADS, tm, HEAD_DIM), lambda i: (0, i, 0)) for _ in o_list]
        args += list(lse_list)
        in_specs += [row(HEAD_DIM) for _ in lse_list]
    else:
        in_specs = [row(GROUP_WIDTH)]
    small = [attn_gain.reshape(1, -1), hg_gain.reshape(1, -1), w_out_bf16,
             n_post.reshape(1, -1), n_pre.reshape(1, -1)]
    args += [o_hg, proj, x] + small
    in_specs += [row(GROUP_WIDTH), pl.BlockSpec((tm, GROUP_WIDTH), lambda i: (i, 6)), row(d)]
    in_specs += [full(a) for a in small]
    return pl.pallas_call(
        functools.partial(_merge_kernel, n_pat),
        grid=(t // tm,),
        in_specs=in_specs,
        out_specs=[row(d), row(d)],
        out_shape=[jax.ShapeDtypeStruct((t, d), F32), jax.ShapeDtypeStruct((t, d), BF16)],
        compiler_params=_params("parallel"),
        name="merge_outproj",
    )(*args)


def _ffn_kernel(h_ref, x1_ref, wg_ref, wu_ref, wd_ref, npost_ref, y_ref, acc_scr):
    j = pl.program_id(1)
    tf = wg_ref.shape[1]
    d = wd_ref.shape[1]

    @pl.when(j == 0)
    def _():
        acc_scr[...] = jnp.zeros_like(acc_scr)

    h = h_ref[...]
    acts = []
    for c in range(0, tf, FFN_CHUNK):
        gate = jnp.dot(h, wg_ref[:, c:c + FFN_CHUNK], preferred_element_type=F32)
        up = jnp.dot(h, wu_ref[:, c:c + FFN_CHUNK], preferred_element_type=F32)
        acts.append((gate * jax.nn.sigmoid(gate) * up).astype(BF16))
    act = jnp.concatenate(acts, axis=-1)
    for c in range(0, d, 2 * FFN_CHUNK):
        cols = slice(c, c + 2 * FFN_CHUNK)
        acc_scr[:, cols] += jnp.dot(act, wd_ref[:, cols], preferred_element_type=F32)

    @pl.when(j == pl.num_programs(1) - 1)
    def _():
        y_ref[...] = x1_ref[...] + _rms(acc_scr[...], npost_ref[...])


def _ffn(h2, x1, wg, wu, wd, n_post, tm, tf):
    t, d = x1.shape
    dff = wg.shape[1]
    assert dff % tf == 0
    return pl.pallas_call(
        _ffn_kernel,
        grid=(t // tm, dff // tf),
        in_specs=[
            pl.BlockSpec((tm, d), lambda i, j: (i, 0)),
            pl.BlockSpec((tm, d), lambda i, j: (i, 0)),
            pl.BlockSpec((d, tf), lambda i, j: (0, j)),
            pl.BlockSpec((d, tf), lambda i, j: (0, j)),
            pl.BlockSpec((tf, d), lambda i, j: (j, 0)),
            pl.BlockSpec((1, d), lambda i, j: (0, 0)),
        ],
        out_specs=pl.BlockSpec((tm, d), lambda i, j: (i, 0)),
        out_shape=jax.ShapeDtypeStruct((t, d), F32),
        scratch_shapes=[pltpu.VMEM((tm, d), F32)],
        compiler_params=_params("parallel", "arbitrary"),
        name="ffn",
    )(h2, x1, wg, wu, wd, n_post.reshape(1, d))


def _row_tile(t, want):
    tm = min(t, want)
    assert t % tm == 0
    return tm


def kernel(x_prompt, x_sample, cache_win_k, cache_win_v, state_hgrn, norm_pre_mix, w_in,
           hg_lb_logits, attn_out_gain, hg_norm_gain, w_out, norm_post_mix, norm_pre_ffn,
           w_gate, w_up, w_down, norm_post_ffn):
    depth = w_in.shape[0]
    bp, t, d = x_prompt.shape
    bs, ts, _ = x_sample.shape
    assert bp == 1 and ts == 1
    keep = min(max(w for w, _ in PATTERNS), t)
    logits = hg_lb_logits.astype(F32)

    yp = x_prompt.reshape(t, d)
    ys = x_sample.reshape(bs, d)
    outs = [[] for _ in range(6)]
    for l in range(depth):
        w_in_b, w_out_b = w_in[l].astype(BF16), w_out[l].astype(BF16)
        wg_b, wu_b, wd_b = w_gate[l].astype(BF16), w_up[l].astype(BF16), w_down[l].astype(BF16)

        dils = tuple(dil for _, dil in PATTERNS)
        proj, *dec = _inproj(yp, norm_pre_mix[l], w_in_b, _row_tile(t, 512), dils)
        o_list, lse_list = zip(*[_band_attn(qkv, dil) for qkv, dil in zip(dec, dils)])
        o_hg, s_fin = _hgrn_prompt(proj, logits, l, _row_tile(t, 256))
        x1, h2 = _merge(o_list, lse_list, o_hg, proj, yp, attn_out_gain[l], hg_norm_gain[l],
                        w_out_b, norm_post_mix[l], norm_pre_ffn[l], _row_tile(t, 256))
        yp = _ffn(h2, x1, wg_b, wu_b, wd_b, norm_post_ffn[l], _row_tile(t, 512), 512)
        outs[0].append(proj[t - keep:, GROUP_WIDTH:2 * GROUP_WIDTH].reshape(1, keep, N_HEADS, HEAD_DIM))
        outs[1].append(proj[t - keep:, 2 * GROUP_WIDTH:3 * GROUP_WIDTH].reshape(1, keep, N_HEADS, HEAD_DIM))
        outs[2].append(s_fin.reshape(1, N_HEADS, HEAD_DIM, HEAD_DIM))

        (proj_s,) = _inproj(ys, norm_pre_mix[l], w_in_b, bs)
        o_att_s = _sample_attn(proj_s, cache_win_k[l], cache_win_v[l])
        o_hg_s, s_new = _hgrn_step(proj_s, logits, l, state_hgrn[l])
        x1s, h2s = _merge([o_att_s], [], o_hg_s, proj_s, ys, attn_out_gain[l], hg_norm_gain[l],
                          w_out_b, norm_post_mix[l], norm_pre_ffn[l], bs)
        ys = _ffn(h2s, x1s, wg_b, wu_b, wd_b, norm_post_ffn[l], bs, 512)
        outs[3].append(proj_s[:, GROUP_WIDTH:2 * GROUP_WIDTH].reshape(bs, 1, N_HEADS, HEAD_DIM))
        outs[4].append(proj_s[:, 2 * GROUP_WIDTH:3 * GROUP_WIDTH].reshape(bs, 1, N_HEADS, HEAD_DIM))
        outs[5].append(s_new)

    st = lambda xs: jnp.stack(xs)
    return (yp.reshape(1, t, d), ys.reshape(bs, 1, d), st(outs[0]), st(outs[1]), st(outs[2]),
            st(outs[3]), st(outs[4]), st(outs[5]))
```

```python
import functools

import jax
import jax.numpy as jnp
import numpy as np
from jax import lax
from jax.experimental import pallas as pl
from jax.experimental.pallas import tpu as pltpu

F32 = jnp.float32
BF16 = jnp.bfloat16

EPS = 1e-6
HEAD_DIM = 128
N_HEADS = 8
GROUP_WIDTH = N_HEADS * HEAD_DIM
N_PROJ_GROUPS = 7
PATTERNS = ((128, 1), (512, 4), (2048, 16))
BAND = 128
NEG_BIG = -1e30
LOG2E = 1.4426950408889634
HG_BLOCK = 16
MXU_COLS = 256
FFN_CHUNK = MXU_COLS
VMEM_LIMIT = 56 * 1024 * 1024


def _params(*sem):
    return pltpu.CompilerParams(dimension_semantics=sem, vmem_limit_bytes=VMEM_LIMIT)


def _rms(x, gain):
    ms = jnp.mean(x * x, axis=-1, keepdims=True)
    return x * lax.rsqrt(ms + EPS) * gain


def _dot_nt(a, b):
    return lax.dot_general(a, b, (((1,), (1,)), ((), ())), preferred_element_type=F32)


def _dot_tn(a, b):
    return lax.dot_general(a, b, (((0,), (0,)), ((), ())), preferred_element_type=F32)


def _inproj_kernel(dils, x_ref, g_ref, w_ref, o_ref, *rest):
    dec_refs, h_scr = rest[:len(dils)], rest[len(dils)]
    j = pl.program_id(1)
    tm = x_ref.shape[0]

    @pl.when(j == 0)
    def _():
        h_scr[...] = _rms(x_ref[...], g_ref[...]).astype(BF16)

    if not dils:
        o_ref[...] = jnp.dot(h_scr[...], w_ref[...], preferred_element_type=F32)
        return

    head_scr = rest[len(dils) + 1]
    plane_scrs = dict(zip(dils[1:-1], rest[len(dils) + 2:]))
    n_att = 3 * GROUP_WIDTH // w_ref.shape[1]

    @pl.when(j >= n_att)
    def _():
        o_ref[...] = jnp.dot(h_scr[...], w_ref[...], preferred_element_type=F32)

    @pl.when(j < n_att)
    def _():
        h = h_scr[...]
        for c in range(0, w_ref.shape[1], MXU_COLS):
            res = jnp.dot(h, w_ref[:, c:c + MXU_COLS], preferred_element_type=F32)
            for hh in range(c // HEAD_DIM, (c + MXU_COLS) // HEAD_DIM):
                sl = slice(hh * HEAD_DIM, (hh + 1) * HEAD_DIM)
                head_res = res[:, hh * HEAD_DIM - c:(hh + 1) * HEAD_DIM - c]
                head_scr[hh] = head_res
                dec_refs[0][0, :, sl] = head_res.astype(BF16)
                prev, planes = 1, {0: head_scr.at[hh]}
                for dil, dec_ref in zip(dils[1:], dec_refs[1:]):
                    ratio, new_planes = dil // prev, {}
                    for rp, src in planes.items():
                        for q in range(ratio):
                            r = rp + prev * q
                            val = src[pl.ds(q, tm // dil, stride=ratio), :]
                            dec_ref[r, :, sl] = val.astype(BF16)
                            if dil in plane_scrs:
                                plane_scrs[dil][hh * dil + r] = val
                                new_planes[r] = plane_scrs[dil].at[hh * dil + r]
                    prev, planes = dil, new_planes


def _inproj(x, gain, w_bf16, tm, tn, dils=()):
    t, d = x.shape
    n = w_bf16.shape[1]
    if dils:
        n_att = 3 * GROUP_WIDTH // tn
        assert n == N_PROJ_GROUPS * GROUP_WIDTH and tn % MXU_COLS == 0 and GROUP_WIDTH % tn == 0
        assert dils[0] == 1 and all(b % a == 0 for a, b in zip(dils, dils[1:]))
        out_specs = [pl.BlockSpec((tm, tn), lambda i, j: (i, jnp.maximum(j - n_att, 0)))]
        out_shape = [jax.ShapeDtypeStruct((t, n - 3 * GROUP_WIDTH), F32)]
        for dil in dils:
            assert tm % (16 * dil) == 0
            out_specs.append(pl.BlockSpec((dil, tm // dil, tn),
                                          lambda i, j: (0, i, jnp.minimum(j, n_att - 1))))
            out_shape.append(jax.ShapeDtypeStruct((dil, t // dil, 3 * GROUP_WIDTH), BF16))
    else:
        out_specs = [pl.BlockSpec((tm, tn), lambda i, j: (i, j))]
        out_shape = [jax.ShapeDtypeStruct((t, n), F32)]
    return pl.pallas_call(
        functools.partial(_inproj_kernel, tuple(dils)),
        grid=(t // tm, n // tn),
        in_specs=[
            pl.BlockSpec((tm, d), lambda i, j: (i, 0)),
            pl.BlockSpec((1, d), lambda i, j: (0, 0)),
            pl.BlockSpec((d, tn), lambda i, j: (0, j)),
        ],
        out_specs=out_specs,
        out_shape=out_shape,
        scratch_shapes=[pltpu.VMEM((tm, d), BF16)]
        + ([pltpu.VMEM((tn // HEAD_DIM, tm, HEAD_DIM), F32)] if dils else [])
        + [pltpu.VMEM((tn // HEAD_DIM * dil, tm // dil, HEAD_DIM), F32) for dil in dils[1:-1]],
        compiler_params=_params("parallel", "arbitrary"),
        name="inproj",
    )(x, gain.reshape(1, d), w_bf16)


def _band_attn_kernel(dil, q_ref, k_ref, v_ref, o_ref, lse_ref, o_scr, kprev_scr, vprev_scr):
    i = pl.program_id(0)
    r = pl.program_id(1)
    bq = q_ref.shape[0]

    @pl.when(i == 0)
    def _():
        kprev_scr[r] = jnp.zeros(kprev_scr.shape[1:], kprev_scr.dtype)
        vprev_scr[r] = jnp.zeros(vprev_scr.shape[1:], vprev_scr.dtype)

    row = lax.broadcasted_iota(jnp.int32, (bq, 2 * bq), 0)
    col = lax.broadcasted_iota(jnp.int32, (bq, 2 * bq), 1)
    dist = row + bq - col
    lo = jnp.where(i == 0, bq, 0)
    valid = (dist >= 0) & (dist <= BAND) & (col >= lo)
    lane = lax.broadcasted_iota(jnp.int32, (bq, HEAD_DIM), 1)
    scale = HEAD_DIM ** -0.5
    lse_tile = jnp.zeros((bq, HEAD_DIM), F32)
    for h in range(N_HEADS):
        sl = slice(h * HEAD_DIM, (h + 1) * HEAD_DIM)
        q = q_ref[:, sl]
        k = jnp.concatenate([kprev_scr[r, :, sl], k_ref[:, sl]], axis=0)
        v = jnp.concatenate([vprev_scr[r, :, sl], v_ref[:, sl]], axis=0)
        s = _dot_nt(q, k) * scale
        s = jnp.where(valid, s, NEG_BIG)
        m = jnp.max(s, axis=-1, keepdims=True)
        p = jnp.exp(s - m)
        l = jnp.sum(p, axis=-1, keepdims=True)
        o = jnp.dot(p.astype(BF16), v, preferred_element_type=F32)
        o_scr[h] = o * (1.0 / l)
        lse_tile = jnp.where(lane == h, m + jnp.log(l), lse_tile)
    kprev_scr[r] = k_ref[...]
    vprev_scr[r] = v_ref[...]
    if dil == 1:
        o_ref[...] = o_scr[...]
        lse_ref[...] = lse_tile
    else:
        for rr in range(dil):
            @pl.when(r == rr)
            def _():
                for h in range(N_HEADS):
                    o_ref[h, pl.ds(rr, bq, stride=dil), :] = o_scr[h]
                lse_ref[pl.ds(rr, bq, stride=dil), :] = lse_tile


def _band_attn(qkv, dil):
    rows = qkv.shape[1]
    t = rows * dil
    bq = BAND
    assert qkv.shape[0] == dil and rows % bq == 0
    blk = (None, bq, GROUP_WIDTH)
    cur = lambda g: (lambda i, r: (r, i, g))
    carry = pltpu.VMEM((dil, bq, GROUP_WIDTH), qkv.dtype)
    return pl.pallas_call(
        functools.partial(_band_attn_kernel, dil),
        grid=(rows // bq, dil),
        in_specs=[pl.BlockSpec(blk, cur(g)) for g in range(3)],
        out_specs=[
            pl.BlockSpec((N_HEADS, dil * bq, HEAD_DIM), lambda i, r: (0, i, 0)),
            pl.BlockSpec((dil * bq, HEAD_DIM), lambda i, r: (i, 0)),
        ],
        out_shape=[
            jax.ShapeDtypeStruct((N_HEADS, t, HEAD_DIM), F32),
            jax.ShapeDtypeStruct((t, HEAD_DIM), F32),
        ],
        scratch_shapes=[pltpu.VMEM((N_HEADS, bq, HEAD_DIM), F32), carry, carry],
        compiler_params=_params("arbitrary", "arbitrary"),
        name=f"band_attn_d{dil}",
    )(qkv, qkv, qkv)


def _sample_attn_kernel(q_ref, kn_ref, vn_ref, *rest):
    n_pat = len(PATTERNS)
    cache_refs = rest[: 2 * n_pat]
    o_ref = rest[2 * n_pat]
    scale = HEAD_DIM ** -0.5
    q = q_ref[0]
    kn = kn_ref[0]
    vn = vn_ref[0]
    s_new = jnp.sum(q * kn, axis=-1, keepdims=True) * scale
    outs, lses = [], []
    for p in range(n_pat):
        kc = cache_refs[2 * p][...]
        vc = cache_refs[2 * p + 1][...]
        s = jnp.sum(kc * q[None], axis=-1, keepdims=True) * scale
        m = jnp.maximum(jnp.max(s, axis=0), s_new)
        e = jnp.exp(s - m[None])
        e_new = jnp.exp(s_new - m)
        l = jnp.sum(e, axis=0) + e_new
        o = jnp.sum(e * vc, axis=0) + e_new * vn
        outs.append(o * (1.0 / l))
        lses.append(m + jnp.log(l))
    top = functools.reduce(jnp.maximum, lses)
    ws = [jnp.exp(x - top) for x in lses]
    den = functools.reduce(lambda a, b: a + b, ws)
    o_ref[0] = functools.reduce(lambda a, b: a + b, [w * o for w, o in zip(ws, outs)]) * (1.0 / den)


def _sample_attn(proj_s, cache_k, cache_v):
    b = proj_s.shape[0]
    past = cache_k.shape[1]
    assert past == max(w for w, _ in PATTERNS), "every dilated key must lie inside the cached window"
    head_blk = (1, N_HEADS, HEAD_DIM)
    in_specs = [pl.BlockSpec(head_blk, lambda i: (i, 0, 0)) for _ in range(3)]
    args = [proj_s[:, g * GROUP_WIDTH:(g + 1) * GROUP_WIDTH].reshape(b, N_HEADS, HEAD_DIM)
            for g in range(3)]
    for _, dil in PATTERNS:
        rows = past // dil
        last_blk = rows // BAND - 1
        for c in (cache_k, cache_v):
            args.append(c.reshape(b, rows, dil, N_HEADS, HEAD_DIM))
            in_specs.append(pl.BlockSpec((None, BAND, None, N_HEADS, HEAD_DIM),
                                         lambda i, lb=last_blk: (i, lb, 0, 0, 0)))
    out = pl.pallas_call(
        _sample_attn_kernel,
        grid=(b,),
        in_specs=in_specs,
        out_specs=pl.BlockSpec(head_blk, lambda i: (i, 0, 0)),
        out_shape=jax.ShapeDtypeStruct((b, N_HEADS, HEAD_DIM), F32),
        compiler_params=_params("parallel"),
        name="sample_attn",
    )(*args)
    return out.reshape(b, GROUP_WIDTH)


def _lower_bound(logits, layer):
    top = jnp.max(logits, axis=0, keepdims=True)
    e = jnp.exp(logits - top)
    return jnp.sum(e[: layer + 1], axis=0, keepdims=True) / jnp.sum(e, axis=0, keepdims=True)


def _split3(x):
    a = x.astype(BF16)
    r = x - a.astype(F32)
    b = r.astype(BF16)
    c = (r - b.astype(F32)).astype(BF16)
    return a, b, c


def _hgrn_kernel(layer, q_ref, f_ref, v_ref, logit_ref, tri_ref, o_ref, state_ref,
                 st_scr, qq_scr, u2_scr, c2_scr, qt_scr, kd_scr):
    step = pl.program_id(0)
    rows = q_ref.shape[0]

    @pl.when(step == 0)
    def _():
        st_scr[...] = jnp.zeros_like(st_scr)

    lb = _lower_bound(logit_ref[...], layer)
    f = lb + (1.0 - lb) * jax.nn.sigmoid(f_ref[...])
    g = jnp.log(f)
    qq = jax.nn.silu(q_ref[...])
    kk = 1.0 - f
    tri = tri_ref[...]
    sums = functools.reduce(
        lambda a, b: a + b,
        [jnp.dot(tri, piece, preferred_element_type=F32) for piece in _split3(g)])
    cum = sums[:rows]
    rev = sums[rows:]
    c2 = cum * LOG2E
    c2_scr[...] = c2
    u2_scr[...] = jnp.log(kk) * LOG2E - c2
    qq_scr[...] = qq
    qt_scr[...] = qq * jnp.exp2(c2)
    kd_scr[...] = kk * jnp.exp(rev)

    half = HG_BLOCK // 2
    t_loc = lax.broadcasted_iota(jnp.int32, (half, HEAD_DIM), 0)
    lane = lax.broadcasted_iota(jnp.int32, (half, HEAD_DIM), 1)

    def block(b, carry):
        base = pl.multiple_of(b * HG_BLOCK, HG_BLOCK)
        for h in range(N_HEADS):
            sl = slice(h * HEAD_DIM, (h + 1) * HEAD_DIM)
            st = st_scr[h]
            inter = _dot_nt(qt_scr[pl.ds(base, HG_BLOCK), sl].astype(BF16), st.astype(BF16))
            q_b = qq_scr[pl.ds(base, HG_BLOCK), sl]
            c_b = c2_scr[pl.ds(base, HG_BLOCK), sl]
            u_b = u2_scr[pl.ds(base, HG_BLOCK), sl]
            v_b = v_ref[pl.ds(base, HG_BLOCK), sl]
            for part in range(2):
                rows_t = slice(part * half, (part + 1) * half)
                q_t = q_b[rows_t]
                c_t = c_b[rows_t]
                acc = inter[rows_t]
                for s in range((part + 1) * half):
                    arg = c_t + u_b[s:s + 1]
                    if s >= part * half:
                        arg = jnp.where(t_loc >= s - part * half, arg, NEG_BIG)
                    a = jnp.sum(q_t * jnp.exp2(arg), axis=-1, keepdims=True)
                    acc = acc + a * v_b[s:s + 1]
                o_ref[pl.ds(base + part * half, half), sl] = acc
            decay = jnp.exp2(c_b[HG_BLOCK - 1:HG_BLOCK])
            delta = _dot_tn(v_b.astype(BF16), kd_scr[pl.ds(base, HG_BLOCK), sl].astype(BF16))
            st_scr[h] = st * decay + delta
        return carry

    lax.fori_loop(0, rows // HG_BLOCK, block, 0)

    @pl.when(step == pl.num_programs(0) - 1)
    def _():
        for h in range(N_HEADS):
            state_ref[h] = st_scr[h].T


def _block_tri(rows):
    idx = np.arange(rows)
    same = (idx[:, None] // HG_BLOCK) == (idx[None, :] // HG_BLOCK)
    lower = same & (idx[None, :] <= idx[:, None])
    upper = same & (idx[None, :] > idx[:, None])
    return jnp.asarray(np.concatenate([lower, upper], axis=0), dtype=BF16)


def _hgrn_prompt(proj, logits, layer, rows):
    t = proj.shape[0]
    assert t % rows == 0 and rows % HG_BLOCK == 0
    col = lambda g: pl.BlockSpec((rows, GROUP_WIDTH), lambda i, g=g: (i, g))
    n_l = logits.shape[0]
    return pl.pallas_call(
        functools.partial(_hgrn_kernel, layer),
        grid=(t // rows,),
        in_specs=[
            col(0), col(1), col(2),
            pl.BlockSpec((n_l, GROUP_WIDTH), lambda i: (0, 0)),
            pl.BlockSpec((2 * rows, rows), lambda i: (0, 0)),
        ],
        out_specs=[
            pl.BlockSpec((rows, GROUP_WIDTH), lambda i: (i, 0)),
            pl.BlockSpec((N_HEADS, HEAD_DIM, HEAD_DIM), lambda i: (0, 0, 0)),
        ],
        out_shape=[
            jax.ShapeDtypeStruct((t, GROUP_WIDTH), F32),
            jax.ShapeDtypeStruct((N_HEADS, HEAD_DIM, HEAD_DIM), F32),
        ],
        scratch_shapes=[pltpu.VMEM((N_HEADS, HEAD_DIM, HEAD_DIM), F32)]
        + [pltpu.VMEM((rows, GROUP_WIDTH), F32) for _ in range(5)],
        compiler_params=_params("arbitrary"),
        name="hgrn_prompt",
    )(proj, proj, proj, logits, _block_tri(rows))


def _hgrn_step_kernel(layer, q_ref, f_ref, v_ref, logit_ref, s_ref, o_ref, snew_ref):
    lb = _lower_bound(logit_ref[...], layer)
    f = lb + (1.0 - lb) * jax.nn.sigmoid(f_ref[0])
    qq = jax.nn.silu(q_ref[0])
    vv = v_ref[0]
    eye = (lax.broadcasted_iota(jnp.int32, (HEAD_DIM, HEAD_DIM), 0)
           == lax.broadcasted_iota(jnp.int32, (HEAD_DIM, HEAD_DIM), 1))

    def column(row):
        return jnp.sum(jnp.where(eye, jnp.broadcast_to(row, (HEAD_DIM, HEAD_DIM)), 0.0),
                       axis=-1, keepdims=True)

    for h in range(N_HEADS):
        sl = slice(h * HEAD_DIM, (h + 1) * HEAD_DIM)
        f_col = column(f[:, sl])
        q_col = column(qq[:, sl])
        s_new = f_col * s_ref[0, h] + (1.0 - f_col) * vv[:, sl]
        snew_ref[0, h] = s_new
        o_ref[0, :, sl] = jnp.sum(q_col * s_new, axis=0, keepdims=True)


def _hgrn_step(proj_s, logits, layer, state):
    b = proj_s.shape[0]
    p3 = proj_s.reshape(b, 1, proj_s.shape[1])
    row = (1, 1, GROUP_WIDTH)
    n_l = logits.shape[0]
    st_blk = (1, N_HEADS, HEAD_DIM, HEAD_DIM)
    o, s_new = pl.pallas_call(
        functools.partial(_hgrn_step_kernel, layer),
        grid=(b,),
        in_specs=[pl.BlockSpec(row, lambda i, g=g: (i, 0, g)) for g in (0, 1, 2)]
        + [pl.BlockSpec((n_l, GROUP_WIDTH), lambda i: (0, 0)),
           pl.BlockSpec(st_blk, lambda i: (i, 0, 0, 0))],
        out_specs=[pl.BlockSpec(row, lambda i: (i, 0, 0)),
                   pl.BlockSpec(st_blk, lambda i: (i, 0, 0, 0))],
        out_shape=[jax.ShapeDtypeStruct((b, 1, GROUP_WIDTH), F32),
                   jax.ShapeDtypeStruct(state.shape, F32)],
        compiler_params=_params("parallel"),
        name="hgrn_step",
    )(p3, p3, p3, logits, state)
    return o.reshape(b, GROUP_WIDTH), s_new


def _merge_kernel(n_pat, *refs):
    o_refs = refs[:n_pat]
    lse_refs = refs[n_pat:2 * n_pat] if n_pat > 1 else ()
    k = 2 * n_pat if n_pat > 1 else n_pat
    (hg_ref, gate_ref, x_ref, again_ref, hgain_ref, wout_ref, npost_ref, npre_ref,
     x1_ref, h2_ref, buf_a, buf_b) = refs[k:]
    tm = x_ref.shape[0]
    i = pl.program_id(0)

    def merge_stage(dst):
        if n_pat > 1:
            lses = [r[...] for r in lse_refs]
            top = functools.reduce(jnp.maximum, lses)
            ws = [jnp.exp(x - top) for x in lses]
            inv = 1.0 / functools.reduce(lambda a, b: a + b, ws)
            ws = [w * inv for w in ws]
        att = []
        sq = jnp.zeros((tm, 1), F32)
        for h in range(N_HEADS):
            sl = slice(h * HEAD_DIM, (h + 1) * HEAD_DIM)
            if n_pat > 1:
                a = functools.reduce(
                    lambda x, y: x + y,
                    [w[:, h:h + 1] * r[h] for w, r in zip(ws, o_refs)])
            else:
                a = o_refs[0][:, sl]
            att.append(a)
            sq = sq + jnp.sum(a * a, axis=-1, keepdims=True)
            g = gate_ref[:, sl]
            dst[:, GROUP_WIDTH + h * HEAD_DIM:GROUP_WIDTH + (h + 1) * HEAD_DIM] = (
                _rms(hg_ref[:, sl], hgain_ref[...]) * (g * jax.nn.sigmoid(g))).astype(BF16)
        inv_rms = lax.rsqrt(sq * (1.0 / GROUP_WIDTH) + EPS)
        for h, a in enumerate(att):
            sl = slice(h * HEAD_DIM, (h + 1) * HEAD_DIM)
            dst[:, sl] = (a * inv_rms * again_ref[:, sl]).astype(BF16)

    def project_stage(src):
        mix = jnp.dot(src[...], wout_ref[...], preferred_element_type=F32)
        x1 = x_ref[...] + _rms(mix, npost_ref[...])
        x1_ref[...] = x1
        h2_ref[...] = _rms(x1, npre_ref[...]).astype(BF16)

    @pl.when(i == 0)
    def _():
        buf_b[...] = jnp.zeros_like(buf_b)

    @pl.when(i % 2 == 0)
    def _():
        project_stage(buf_b)
        merge_stage(buf_a)

    @pl.when(i % 2 == 1)
    def _():
        project_stage(buf_a)
        merge_stage(buf_b)


def _merge(o_list, lse_list, o_hg, proj, x, attn_gain, hg_gain, w_out_bf16, n_post, n_pre, tm):
    t, d = x.shape
    n_pat = len(o_list)
    n = t // tm
    cur = lambda i: jnp.minimum(i, n - 1)
    prev = lambda i: jnp.maximum(i - 1, 0)
    row_cur = lambda w: pl.BlockSpec((tm, w), lambda i: (cur(i), 0))
    row_prev = lambda w: pl.BlockSpec((tm, w), lambda i: (prev(i), 0))
    full = lambda a: pl.BlockSpec(a.shape, lambda i: (0,) * a.ndim)
    args = list(o_list)
    if n_pat > 1:
        in_specs = [pl.BlockSpec((N_HEADS, tm, HEAD_DIM), lambda i: (0, cur(i), 0)) for _ in o_list]
        args += list(lse_list)
        in_specs += [row_cur(HEAD_DIM) for _ in lse_list]
    else:
        in_specs = [row_cur(GROUP_WIDTH)]
    small = [attn_gain.reshape(1, -1), hg_gain.reshape(1, -1), w_out_bf16,
             n_post.reshape(1, -1), n_pre.reshape(1, -1)]
    args += [o_hg, proj, x] + small
    in_specs += [row_cur(GROUP_WIDTH), pl.BlockSpec((tm, GROUP_WIDTH), lambda i: (cur(i), 3)),
                 row_prev(d)]
    in_specs += [full(a) for a in small]
    return pl.pallas_call(
        functools.partial(_merge_kernel, n_pat),
        grid=(n + 1,),
        in_specs=in_specs,
        out_specs=[row_prev(d), row_prev(d)],
        out_shape=[jax.ShapeDtypeStruct((t, d), F32), jax.ShapeDtypeStruct((t, d), BF16)],
        scratch_shapes=[pltpu.VMEM((tm, 2 * GROUP_WIDTH), BF16) for _ in range(2)],
        compiler_params=_params("arbitrary"),
        name="merge_outproj",
    )(*args)


def _ffn_kernel(h_ref, x1_ref, wg_ref, wu_ref, wd_ref, npost_ref, y_ref):
    j = pl.program_id(1)
    tf = wg_ref.shape[1]
    d = wd_ref.shape[1]
    acc_scr = y_ref

    @pl.when(j == 0)
    def _():
        acc_scr[...] = jnp.zeros_like(acc_scr)

    h = h_ref[...]
    acts = []
    for c in range(0, tf, FFN_CHUNK):
        gate = jnp.dot(h, wg_ref[:, c:c + FFN_CHUNK], preferred_element_type=F32)
        up = jnp.dot(h, wu_ref[:, c:c + FFN_CHUNK], preferred_element_type=F32)
        acts.append((gate * jax.nn.sigmoid(gate) * up).astype(BF16))
    act = jnp.concatenate(acts, axis=-1)
    for c in range(0, d, FFN_CHUNK):
        cols = slice(c, c + FFN_CHUNK)
        acc_scr[:, cols] += jnp.dot(act, wd_ref[:, cols], preferred_element_type=F32)

    @pl.when(j == pl.num_programs(1) - 1)
    def _():
        y_ref[...] = x1_ref[...] + _rms(acc_scr[...], npost_ref[...])


def _ffn(h2, x1, wg, wu, wd, n_post, tm, tf):
    t, d = x1.shape
    dff = wg.shape[1]
    assert dff % tf == 0
    return pl.pallas_call(
        _ffn_kernel,
        grid=(t // tm, dff // tf),
        in_specs=[
            pl.BlockSpec((tm, d), lambda i, j: (i, 0)),
            pl.BlockSpec((tm, d), lambda i, j: (i, 0)),
            pl.BlockSpec((d, tf), lambda i, j: (0, j)),
            pl.BlockSpec((d, tf), lambda i, j: (0, j)),
            pl.BlockSpec((tf, d), lambda i, j: (j, 0)),
            pl.BlockSpec((1, d), lambda i, j: (0, 0)),
        ],
        out_specs=pl.BlockSpec((tm, d), lambda i, j: (i, 0)),
        out_shape=jax.ShapeDtypeStruct((t, d), F32),
        compiler_params=_params("parallel", "arbitrary"),
        name="ffn",
    )(h2, x1, wg, wu, wd, n_post.reshape(1, d))


def _row_tile(t, want):
    tm = min(t, want)
    assert t % tm == 0
    return tm


INPROJ_TILE = (1024, 512)
HGRN_ROWS = 256
MERGE_ROWS = 256
FFN_TILE = (1024, 256)
SMALL_BATCH_COLS = 512


def kernel(x_prompt, x_sample, cache_win_k, cache_win_v, state_hgrn, norm_pre_mix, w_in,
           hg_lb_logits, attn_out_gain, hg_norm_gain, w_out, norm_post_mix, norm_pre_ffn,
           w_gate, w_up, w_down, norm_post_ffn):
    depth = w_in.shape[0]
    bp, t, d = x_prompt.shape
    bs, ts, _ = x_sample.shape
    assert bp == 1 and ts == 1
    keep = min(max(w for w, _ in PATTERNS), t)
    logits = hg_lb_logits.astype(F32)

    yp = x_prompt.reshape(t, d)
    ys = x_sample.reshape(bs, d)
    outs = [[] for _ in range(6)]
    for l in range(depth):
        w_in_b, w_out_b = w_in[l].astype(BF16), w_out[l].astype(BF16)
        wg_b, wu_b, wd_b = w_gate[l].astype(BF16), w_up[l].astype(BF16), w_down[l].astype(BF16)

        dils = tuple(dil for _, dil in PATTERNS)
        hg_cols, *dec = _inproj(yp, norm_pre_mix[l], w_in_b, _row_tile(t, INPROJ_TILE[0]),
                                INPROJ_TILE[1], dils)
        o_list, lse_list = zip(*[_band_attn(qkv, dil) for qkv, dil in zip(dec, dils)])
        o_hg, s_fin = _hgrn_prompt(hg_cols, logits, l, _row_tile(t, HGRN_ROWS))
        x1, h2 = _merge(o_list, lse_list, o_hg, hg_cols, yp, attn_out_gain[l], hg_norm_gain[l],
                        w_out_b, norm_post_mix[l], norm_pre_ffn[l], _row_tile(t, MERGE_ROWS))
        (proj,) = _inproj(yp[t - keep:], norm_pre_mix[l], w_in_b[:, GROUP_WIDTH:3 * GROUP_WIDTH],
                          _row_tile(keep, INPROJ_TILE[0]), INPROJ_TILE[1])
        yp = _ffn(h2, x1, wg_b, wu_b, wd_b, norm_post_ffn[l], _row_tile(t, FFN_TILE[0]), FFN_TILE[1])
        outs[0].append(proj[:, :GROUP_WIDTH].reshape(1, keep, N_HEADS, HEAD_DIM))
        outs[1].append(proj[:, GROUP_WIDTH:].reshape(1, keep, N_HEADS, HEAD_DIM))
        outs[2].append(s_fin.reshape(1, N_HEADS, HEAD_DIM, HEAD_DIM))

        (proj_s,) = _inproj(ys, norm_pre_mix[l], w_in_b, bs, SMALL_BATCH_COLS)
        hg_cols_s = proj_s[:, 3 * GROUP_WIDTH:]
        o_att_s = _sample_attn(proj_s, cache_win_k[l], cache_win_v[l])
        o_hg_s, s_new = _hgrn_step(hg_cols_s, logits, l, state_hgrn[l])
        x1s, h2s = _merge([o_att_s], [], o_hg_s, hg_cols_s, ys, attn_out_gain[l], hg_norm_gain[l],
                          w_out_b, norm_post_mix[l], norm_pre_ffn[l], bs)
        ys = _ffn(h2s, x1s, wg_b, wu_b, wd_b, norm_post_ffn[l], bs, SMALL_BATCH_COLS)
        outs[3].append(proj_s[:, GROUP_WIDTH:2 * GROUP_WIDTH].reshape(bs, 1, N_HEADS, HEAD_DIM))
        outs[4].append(proj_s[:, 2 * GROUP_WIDTH:3 * GROUP_WIDTH].reshape(bs, 1, N_HEADS, HEAD_DIM))
        outs[5].append(s_new)

    st = lambda xs: jnp.stack(xs)
    return (yp.reshape(1, t, d), ys.reshape(bs, 1, d), st(outs[0]), st(outs[1]), st(outs[2]),
            st(outs[3]), st(outs[4]), st(outs[5]))
```

```python
import functools

import jax
import jax.numpy as jnp
import numpy as np
from jax import lax
from jax.experimental import pallas as pl
from jax.experimental.pallas import tpu as pltpu

F32 = jnp.float32
BF16 = jnp.bfloat16

EPS = 1e-6
HEAD_DIM = 128
N_HEADS = 8
GROUP_WIDTH = N_HEADS * HEAD_DIM
N_PROJ_GROUPS = 7
PATTERNS = ((128, 1), (512, 4), (2048, 16))
BAND = 128
NEG_BIG = -1e30
LOG2E = 1.4426950408889634
HG_BLOCK = 128
SUBLANES = 8
MXU_COLS = 256
FFN_CHUNK = MXU_COLS
VMEM_LIMIT = 56 * 1024 * 1024


def _params(*sem):
    return pltpu.CompilerParams(dimension_semantics=sem, vmem_limit_bytes=VMEM_LIMIT)


def _rms(x, gain):
    ms = jnp.mean(x * x, axis=-1, keepdims=True)
    return x * lax.rsqrt(ms + EPS) * gain


def _dot_nt(a, b):
    return lax.dot_general(a, b, (((1,), (1,)), ((), ())), preferred_element_type=F32)


def _dot_tn(a, b):
    return lax.dot_general(a, b, (((0,), (0,)), ((), ())), preferred_element_type=F32)


def _inproj_kernel(dils, x_ref, g_ref, w_ref, o_ref, *rest):
    dec_refs, h_scr = rest[:len(dils)], rest[len(dils)]
    j = pl.program_id(1)
    tm = x_ref.shape[0]

    @pl.when(j == 0)
    def _():
        h_scr[...] = _rms(x_ref[...], g_ref[...]).astype(BF16)

    if not dils:
        o_ref[...] = jnp.dot(h_scr[...], w_ref[...], preferred_element_type=F32)
        return

    head_scr = rest[len(dils) + 1]
    plane_scrs = dict(zip(dils[1:-1], rest[len(dils) + 2:]))
    n_att = 3 * GROUP_WIDTH // w_ref.shape[1]

    @pl.when(j >= n_att)
    def _():
        o_ref[...] = jnp.dot(h_scr[...], w_ref[...], preferred_element_type=F32)

    @pl.when(j < n_att)
    def _():
        h = h_scr[...]
        for c in range(0, w_ref.shape[1], MXU_COLS):
            res = jnp.dot(h, w_ref[:, c:c + MXU_COLS], preferred_element_type=F32)
            for hh in range(c // HEAD_DIM, (c + MXU_COLS) // HEAD_DIM):
                sl = slice(hh * HEAD_DIM, (hh + 1) * HEAD_DIM)
                head_res = res[:, hh * HEAD_DIM - c:(hh + 1) * HEAD_DIM - c]
                head_scr[hh] = head_res
                dec_refs[0][0, :, sl] = head_res.astype(BF16)
                prev, planes = 1, {0: head_scr.at[hh]}
                for dil, dec_ref in zip(dils[1:], dec_refs[1:]):
                    ratio, new_planes = dil // prev, {}
                    for rp, src in planes.items():
                        for q in range(ratio):
                            r = rp + prev * q
                            val = src[pl.ds(q, tm // dil, stride=ratio), :]
                            dec_ref[r, :, sl] = val.astype(BF16)
                            if dil in plane_scrs:
                                plane_scrs[dil][hh * dil + r] = val
                                new_planes[r] = plane_scrs[dil].at[hh * dil + r]
                    prev, planes = dil, new_planes


def _col_tiles(w, tn):
    d, n = w.shape
    assert n % tn == 0
    return w.reshape(d, n // tn, tn).transpose(1, 0, 2)


def _inproj(x, gain, w_tiles, tm, dils=()):
    t, d = x.shape
    tn = w_tiles.shape[2]
    n = w_tiles.shape[0] * tn
    if dils:
        n_att = 3 * GROUP_WIDTH // tn
        assert n == N_PROJ_GROUPS * GROUP_WIDTH and tn % MXU_COLS == 0 and GROUP_WIDTH % tn == 0
        assert dils[0] == 1 and all(b % a == 0 for a, b in zip(dils, dils[1:]))
        out_specs = [pl.BlockSpec((tm, tn), lambda i, j: (i, jnp.maximum(j - n_att, 0)))]
        out_shape = [jax.ShapeDtypeStruct((t, n - 3 * GROUP_WIDTH), F32)]
        for dil in dils:
            assert tm % (16 * dil) == 0
            out_specs.append(pl.BlockSpec((dil, tm // dil, tn),
                                          lambda i, j: (0, i, jnp.minimum(j, n_att - 1))))
            out_shape.append(jax.ShapeDtypeStruct((dil, t // dil, 3 * GROUP_WIDTH), BF16))
    else:
        out_specs = [pl.BlockSpec((tm, tn), lambda i, j: (i, j))]
        out_shape = [jax.ShapeDtypeStruct((t, n), F32)]
    return pl.pallas_call(
        functools.partial(_inproj_kernel, tuple(dils)),
        grid=(t // tm, n // tn),
        in_specs=[
            pl.BlockSpec((tm, d), lambda i, j: (i, 0)),
            pl.BlockSpec((1, d), lambda i, j: (0, 0)),
            pl.BlockSpec((None, d, tn), lambda i, j: (j, 0, 0)),
        ],
        out_specs=out_specs,
        out_shape=out_shape,
        scratch_shapes=[pltpu.VMEM((tm, d), BF16)]
        + ([pltpu.VMEM((tn // HEAD_DIM, tm, HEAD_DIM), F32)] if dils else [])
        + [pltpu.VMEM((tn // HEAD_DIM * dil, tm // dil, HEAD_DIM), F32) for dil in dils[1:-1]],
        compiler_params=_params("parallel", "arbitrary"),
        name="inproj",
    )(x, gain.reshape(1, d), w_tiles)


def _band_attn_kernel(dil, q_ref, k_ref, v_ref, o_ref, lse_ref, o_scr, kprev_scr, vprev_scr):
    i = pl.program_id(0)
    r = pl.program_id(1)
    bq = q_ref.shape[0]

    @pl.when(i == 0)
    def _():
        kprev_scr[r] = jnp.zeros(kprev_scr.shape[1:], kprev_scr.dtype)
        vprev_scr[r] = jnp.zeros(vprev_scr.shape[1:], vprev_scr.dtype)

    row = lax.broadcasted_iota(jnp.int32, (bq, 2 * bq), 0)
    col = lax.broadcasted_iota(jnp.int32, (bq, 2 * bq), 1)
    dist = row + bq - col
    lo = jnp.where(i == 0, bq, 0)
    valid = (dist >= 0) & (dist <= BAND) & (col >= lo)
    lane = lax.broadcasted_iota(jnp.int32, (bq, HEAD_DIM), 1)
    scale = HEAD_DIM ** -0.5
    lse_tile = jnp.zeros((bq, HEAD_DIM), F32)
    for h in range(N_HEADS):
        sl = slice(h * HEAD_DIM, (h + 1) * HEAD_DIM)
        q = q_ref[:, sl]
        k = jnp.concatenate([kprev_scr[r, :, sl], k_ref[:, sl]], axis=0)
        v = jnp.concatenate([vprev_scr[r, :, sl], v_ref[:, sl]], axis=0)
        s = _dot_nt(q, k) * scale
        s = jnp.where(valid, s, NEG_BIG)
        m = jnp.max(s, axis=-1, keepdims=True)
        p = jnp.exp(s - m)
        l = jnp.sum(p, axis=-1, keepdims=True)
        o = jnp.dot(p.astype(BF16), v, preferred_element_type=F32)
        o_scr[h] = o * (1.0 / l)
        lse_tile = jnp.where(lane == h, m + jnp.log(l), lse_tile)
    kprev_scr[r] = k_ref[...]
    vprev_scr[r] = v_ref[...]
    if dil == 1:
        o_ref[...] = o_scr[...]
        lse_ref[...] = lse_tile
    else:
        for rr in range(dil):
            @pl.when(r == rr)
            def _():
                for h in range(N_HEADS):
                    o_ref[h, pl.ds(rr, bq, stride=dil), :] = o_scr[h]
                lse_ref[pl.ds(rr, bq, stride=dil), :] = lse_tile


def _band_attn(qkv, dil):
    rows = qkv.shape[1]
    t = rows * dil
    bq = BAND
    assert qkv.shape[0] == dil and rows % bq == 0
    blk = (None, bq, GROUP_WIDTH)
    cur = lambda g: (lambda i, r: (r, i, g))
    carry = pltpu.VMEM((dil, bq, GROUP_WIDTH), qkv.dtype)
    return pl.pallas_call(
        functools.partial(_band_attn_kernel, dil),
        grid=(rows // bq, dil),
        in_specs=[pl.BlockSpec(blk, cur(g)) for g in range(3)],
        out_specs=[
            pl.BlockSpec((N_HEADS, dil * bq, HEAD_DIM), lambda i, r: (0, i, 0)),
            pl.BlockSpec((dil * bq, HEAD_DIM), lambda i, r: (i, 0)),
        ],
        out_shape=[
            jax.ShapeDtypeStruct((N_HEADS, t, HEAD_DIM), F32),
            jax.ShapeDtypeStruct((t, HEAD_DIM), F32),
        ],
        scratch_shapes=[pltpu.VMEM((N_HEADS, bq, HEAD_DIM), F32), carry, carry],
        compiler_params=_params("arbitrary", "arbitrary"),
        name=f"band_attn_d{dil}",
    )(qkv, qkv, qkv)


def _sample_attn_kernel(q_ref, kn_ref, vn_ref, *rest):
    n_pat = len(PATTERNS)
    cache_refs = rest[: 2 * n_pat]
    o_ref = rest[2 * n_pat]
    scale = HEAD_DIM ** -0.5
    q = q_ref[0]
    kn = kn_ref[0]
    vn = vn_ref[0]
    s_new = jnp.sum(q * kn, axis=-1, keepdims=True) * scale
    outs, lses = [], []
    for p in range(n_pat):
        kc = cache_refs[2 * p][...]
        vc = cache_refs[2 * p + 1][...]
        s = jnp.sum(kc * q[None], axis=-1, keepdims=True) * scale
        m = jnp.maximum(jnp.max(s, axis=0), s_new)
        e = jnp.exp(s - m[None])
        e_new = jnp.exp(s_new - m)
        l = jnp.sum(e, axis=0) + e_new
        o = jnp.sum(e * vc, axis=0) + e_new * vn
        outs.append(o * (1.0 / l))
        lses.append(m + jnp.log(l))
    top = functools.reduce(jnp.maximum, lses)
    ws = [jnp.exp(x - top) for x in lses]
    den = functools.reduce(lambda a, b: a + b, ws)
    o_ref[0] = functools.reduce(lambda a, b: a + b, [w * o for w, o in zip(ws, outs)]) * (1.0 / den)


def _sample_attn(proj_s, cache_k, cache_v):
    b = proj_s.shape[0]
    past = cache_k.shape[1]
    assert past == max(w for w, _ in PATTERNS), "every dilated key must lie inside the cached window"
    head_blk = (1, N_HEADS, HEAD_DIM)
    in_specs = [pl.BlockSpec(head_blk, lambda i: (i, 0, 0)) for _ in range(3)]
    args = [proj_s[:, g * GROUP_WIDTH:(g + 1) * GROUP_WIDTH].reshape(b, N_HEADS, HEAD_DIM)
            for g in range(3)]
    for _, dil in PATTERNS:
        rows = past // dil
        last_blk = rows // BAND - 1
        for c in (cache_k, cache_v):
            args.append(c.reshape(b, rows, dil, N_HEADS, HEAD_DIM))
            in_specs.append(pl.BlockSpec((None, BAND, None, N_HEADS, HEAD_DIM),
                                         lambda i, lb=last_blk: (i, lb, 0, 0, 0)))
    out = pl.pallas_call(
        _sample_attn_kernel,
        grid=(b,),
        in_specs=in_specs,
        out_specs=pl.BlockSpec(head_blk, lambda i: (i, 0, 0)),
        out_shape=jax.ShapeDtypeStruct((b, N_HEADS, HEAD_DIM), F32),
        compiler_params=_params("parallel"),
        name="sample_attn",
    )(*args)
    return out.reshape(b, GROUP_WIDTH)


def _lower_bound(logits, layer):
    top = jnp.max(logits, axis=0, keepdims=True)
    e = jnp.exp(logits - top)
    return jnp.sum(e[: layer + 1], axis=0, keepdims=True) / jnp.sum(e, axis=0, keepdims=True)


def _split3(x):
    a = x.astype(BF16)
    r = x - a.astype(F32)
    b = r.astype(BF16)
    c = (r - b.astype(F32)).astype(BF16)
    return a, b, c


def _pair_weights(c, q, k, level_of_pair):
    n = len(c)
    sub = lax.broadcasted_iota(jnp.int32, c[0].shape, 0)
    n_levels = (n * SUBLANES).bit_length() - 1
    q_all = jnp.concatenate(q, axis=0).astype(BF16)
    k_all = jnp.concatenate(k, axis=0).astype(BF16)
    a = jnp.where(level_of_pair == n_levels, _dot_nt(q_all, k_all), 0.0)
    for lev in range(n_levels):
        half = 1 << lev
        xs = []
        for i in range(n):
            if half >= SUBLANES:
                nv = 2 * half // SUBLANES
                first = (i // nv) * nv
                upper = (i % nv) >= nv // 2
                ref = c[first + nv // 2 - 1][SUBLANES - 1:SUBLANES]
                arg = c[i] - ref if upper else ref - c[i]
                base = q[i] if upper else k[i]
            else:
                upper = (sub & half) != 0
                if half == SUBLANES // 2:
                    ref = c[i][half - 1:half]
                elif half == 1:
                    ref = jnp.where(upper, pltpu.roll(c[i], 1, 0), c[i])
                else:
                    odd = (sub & 1) != 0
                    z = jnp.where(odd, c[i], pltpu.roll(c[i], SUBLANES - 1, 0))
                    ref = jnp.where(upper, pltpu.roll(z, 2, 0), z)
                diff = c[i] - ref
                arg = jnp.where(upper, diff, -diff)
                base = jnp.where(upper, q[i], k[i])
            xs.append(base * jnp.exp2(arg))
        x = jnp.concatenate(xs, axis=0).astype(BF16)
        a = jnp.where(level_of_pair == lev, _dot_nt(x, x), a)
    return a


def _hgrn_kernel(layer, q_ref, f_ref, v_ref, logit_ref, tri_ref, lv_ref, o_ref, state_ref,
                 st_scr, qq_scr, kk_scr, c2_scr, qt_scr, kd_scr):
    step = pl.program_id(0)
    rows = q_ref.shape[0]

    @pl.when(step == 0)
    def _():
        st_scr[...] = jnp.zeros_like(st_scr)

    lb = _lower_bound(logit_ref[...], layer)
    f = lb + (1.0 - lb) * jax.nn.sigmoid(f_ref[...])
    g = jnp.log(f)
    qq = jax.nn.silu(q_ref[...])
    kk = 1.0 - f
    tri = tri_ref[...]
    sums = functools.reduce(
        lambda a, b: a + b,
        [jnp.dot(tri, piece, preferred_element_type=F32) for piece in _split3(g)])
    cum = sums[:rows]
    rev = sums[rows:]
    c2 = cum * LOG2E
    c2_scr[...] = c2
    qq_scr[...] = qq
    kk_scr[...] = kk
    qt_scr[...] = qq * jnp.exp2(c2)
    kd_scr[...] = kk * jnp.exp(rev)

    n_groups = HG_BLOCK // SUBLANES

    def block(b, carry):
        base = pl.multiple_of(b * HG_BLOCK, HG_BLOCK)
        level_of_pair = lv_ref[...]
        for h in range(N_HEADS):
            sl = slice(h * HEAD_DIM, (h + 1) * HEAD_DIM)
            groups = lambda ref: [ref[pl.ds(base + SUBLANES * i, SUBLANES), sl] for i in range(n_groups)]
            c_b = groups(c2_scr)
            st = st_scr[h]
            inter = _dot_nt(qt_scr[pl.ds(base, HG_BLOCK), sl].astype(BF16), st.astype(BF16))
            pair = _pair_weights(c_b, groups(qq_scr), groups(kk_scr), level_of_pair)
            v_b = v_ref[pl.ds(base, HG_BLOCK), sl].astype(BF16)
            o_ref[pl.ds(base, HG_BLOCK), sl] = inter + jnp.dot(
                pair.astype(BF16), v_b, preferred_element_type=F32)
            decay = jnp.exp2(c_b[-1][SUBLANES - 1:SUBLANES])
            delta = _dot_tn(v_b, kd_scr[pl.ds(base, HG_BLOCK), sl].astype(BF16))
            st_scr[h] = st * decay + delta
        return carry

    lax.fori_loop(0, rows // HG_BLOCK, block, 0)

    @pl.when(step == pl.num_programs(0) - 1)
    def _():
        for h in range(N_HEADS):
            state_ref[h] = st_scr[h].T


def _block_tri(rows):
    idx = np.arange(rows)
    same = (idx[:, None] // HG_BLOCK) == (idx[None, :] // HG_BLOCK)
    lower = same & (idx[None, :] <= idx[:, None])
    upper = same & (idx[None, :] > idx[:, None])
    return jnp.asarray(np.concatenate([lower, upper], axis=0), dtype=BF16)


def _pair_levels():
    idx = np.arange(HG_BLOCK)
    t, s = idx[:, None], idx[None, :]
    n_levels = HG_BLOCK.bit_length() - 1
    level = np.floor(np.log2(np.maximum(t ^ s, 1))).astype(np.int32)
    return jnp.asarray(np.where(t == s, n_levels, np.where(s < t, level, -1)).astype(np.int32))


def _hgrn_prompt(proj, logits, layer, rows):
    t = proj.shape[0]
    assert t % rows == 0 and rows % HG_BLOCK == 0
    col = lambda g: pl.BlockSpec((rows, GROUP_WIDTH), lambda i, g=g: (i, g))
    n_l = logits.shape[0]
    return pl.pallas_call(
        functools.partial(_hgrn_kernel, layer),
        grid=(t // rows,),
        in_specs=[
            col(0), col(1), col(2),
            pl.BlockSpec((n_l, GROUP_WIDTH), lambda i: (0, 0)),
            pl.BlockSpec((2 * rows, rows), lambda i: (0, 0)),
            pl.BlockSpec((HG_BLOCK, HG_BLOCK), lambda i: (0, 0)),
        ],
        out_specs=[
            pl.BlockSpec((rows, GROUP_WIDTH), lambda i: (i, 0)),
            pl.BlockSpec((N_HEADS, HEAD_DIM, HEAD_DIM), lambda i: (0, 0, 0)),
        ],
        out_shape=[
            jax.ShapeDtypeStruct((t, GROUP_WIDTH), F32),
            jax.ShapeDtypeStruct((N_HEADS, HEAD_DIM, HEAD_DIM), F32),
        ],
        scratch_shapes=[pltpu.VMEM((N_HEADS, HEAD_DIM, HEAD_DIM), F32)]
        + [pltpu.VMEM((rows, GROUP_WIDTH), F32) for _ in range(5)],
        compiler_params=_params("arbitrary"),
        name="hgrn_prompt",
    )(proj, proj, proj, logits, _block_tri(rows), _pair_levels())


def _hgrn_step_kernel(layer, q_ref, f_ref, v_ref, logit_ref, s_ref, o_ref, snew_ref):
    lb = _lower_bound(logit_ref[...], layer)
    f = lb + (1.0 - lb) * jax.nn.sigmoid(f_ref[0])
    qq = jax.nn.silu(q_ref[0])
    vv = v_ref[0]
    eye = (lax.broadcasted_iota(jnp.int32, (HEAD_DIM, HEAD_DIM), 0)
           == lax.broadcasted_iota(jnp.int32, (HEAD_DIM, HEAD_DIM), 1))

    def column(row):
        return jnp.sum(jnp.where(eye, jnp.broadcast_to(row, (HEAD_DIM, HEAD_DIM)), 0.0),
                       axis=-1, keepdims=True)

    for h in range(N_HEADS):
        sl = slice(h * HEAD_DIM, (h + 1) * HEAD_DIM)
        f_col = column(f[:, sl])
        q_col = column(qq[:, sl])
        s_new = f_col * s_ref[0, h] + (1.0 - f_col) * vv[:, sl]
        snew_ref[0, h] = s_new
        o_ref[0, :, sl] = jnp.sum(q_col * s_new, axis=0, keepdims=True)


def _hgrn_step(proj_s, logits, layer, state):
    b = proj_s.shape[0]
    p3 = proj_s.reshape(b, 1, proj_s.shape[1])
    row = (1, 1, GROUP_WIDTH)
    n_l = logits.shape[0]
    st_blk = (1, N_HEADS, HEAD_DIM, HEAD_DIM)
    o, s_new = pl.pallas_call(
        functools.partial(_hgrn_step_kernel, layer),
        grid=(b,),
        in_specs=[pl.BlockSpec(row, lambda i, g=g: (i, 0, g)) for g in (0, 1, 2)]
        + [pl.BlockSpec((n_l, GROUP_WIDTH), lambda i: (0, 0)),
           pl.BlockSpec(st_blk, lambda i: (i, 0, 0, 0))],
        out_specs=[pl.BlockSpec(row, lambda i: (i, 0, 0)),
                   pl.BlockSpec(st_blk, lambda i: (i, 0, 0, 0))],
        out_shape=[jax.ShapeDtypeStruct((b, 1, GROUP_WIDTH), F32),
                   jax.ShapeDtypeStruct(state.shape, F32)],
        compiler_params=_params("parallel"),
        name="hgrn_step",
    )(p3, p3, p3, logits, state)
    return o.reshape(b, GROUP_WIDTH), s_new


def _merge_kernel(n_pat, *refs):
    o_refs = refs[:n_pat]
    lse_refs = refs[n_pat:2 * n_pat] if n_pat > 1 else ()
    k = 2 * n_pat if n_pat > 1 else n_pat
    (hg_ref, gate_ref, x_ref, again_ref, hgain_ref, wout_ref, npost_ref, npre_ref,
     x1_ref, h2_ref, buf_a, buf_b) = refs[k:]
    tm = x_ref.shape[0]
    i = pl.program_id(0)

    def merge_stage(dst):
        if n_pat > 1:
            lses = [r[...] for r in lse_refs]
            top = functools.reduce(jnp.maximum, lses)
            ws = [jnp.exp(x - top) for x in lses]
            inv = 1.0 / functools.reduce(lambda a, b: a + b, ws)
            ws = [w * inv for w in ws]
        att = []
        sq = jnp.zeros((tm, 1), F32)
        for h in range(N_HEADS):
            sl = slice(h * HEAD_DIM, (h + 1) * HEAD_DIM)
            if n_pat > 1:
                a = functools.reduce(
                    lambda x, y: x + y,
                    [w[:, h:h + 1] * r[h] for w, r in zip(ws, o_refs)])
            else:
                a = o_refs[0][:, sl]
            att.append(a)
            sq = sq + jnp.sum(a * a, axis=-1, keepdims=True)
            g = gate_ref[:, sl]
            dst[:, GROUP_WIDTH + h * HEAD_DIM:GROUP_WIDTH + (h + 1) * HEAD_DIM] = (
                _rms(hg_ref[:, sl], hgain_ref[...]) * (g * jax.nn.sigmoid(g))).astype(BF16)
        inv_rms = lax.rsqrt(sq * (1.0 / GROUP_WIDTH) + EPS)
        for h, a in enumerate(att):
            sl = slice(h * HEAD_DIM, (h + 1) * HEAD_DIM)
            dst[:, sl] = (a * inv_rms * again_ref[:, sl]).astype(BF16)

    def project_stage(src):
        mix = jnp.dot(src[...], wout_ref[...], preferred_element_type=F32)
        x1 = x_ref[...] + _rms(mix, npost_ref[...])
        x1_ref[...] = x1
        h2_ref[...] = _rms(x1, npre_ref[...]).astype(BF16)

    @pl.when(i == 0)
    def _():
        buf_b[...] = jnp.zeros_like(buf_b)

    @pl.when(i % 2 == 0)
    def _():
        project_stage(buf_b)
        merge_stage(buf_a)

    @pl.when(i % 2 == 1)
    def _():
        project_stage(buf_a)
        merge_stage(buf_b)


def _merge(o_list, lse_list, o_hg, proj, x, attn_gain, hg_gain, w_out_bf16, n_post, n_pre, tm):
    t, d = x.shape
    n_pat = len(o_list)
    n = t // tm
    cur = lambda i: jnp.minimum(i, n - 1)
    prev = lambda i: jnp.maximum(i - 1, 0)
    row_cur = lambda w: pl.BlockSpec((tm, w), lambda i: (cur(i), 0))
    row_prev = lambda w: pl.BlockSpec((tm, w), lambda i: (prev(i), 0))
    full = lambda a: pl.BlockSpec(a.shape, lambda i: (0,) * a.ndim)
    args = list(o_list)
    if n_pat > 1:
        in_specs = [pl.BlockSpec((N_HEADS, tm, HEAD_DIM), lambda i: (0, cur(i), 0)) for _ in o_list]
        args += list(lse_list)
        in_specs += [row_cur(HEAD_DIM) for _ in lse_list]
    else:
        in_specs = [row_cur(GROUP_WIDTH)]
    small = [attn_gain.reshape(1, -1), hg_gain.reshape(1, -1), w_out_bf16,
             n_post.reshape(1, -1), n_pre.reshape(1, -1)]
    args += [o_hg, proj, x] + small
    in_specs += [row_cur(GROUP_WIDTH), pl.BlockSpec((tm, GROUP_WIDTH), lambda i: (cur(i), 3)),
                 row_prev(d)]
    in_specs += [full(a) for a in small]
    return pl.pallas_call(
        functools.partial(_merge_kernel, n_pat),
        grid=(n + 1,),
        in_specs=in_specs,
        out_specs=[row_prev(d), row_prev(d)],
        out_shape=[jax.ShapeDtypeStruct((t, d), F32), jax.ShapeDtypeStruct((t, d), BF16)],
        scratch_shapes=[pltpu.VMEM((tm, 2 * GROUP_WIDTH), BF16) for _ in range(2)],
        compiler_params=_params("arbitrary"),
        name="merge_outproj",
    )(*args)


def _ffn_kernel(h_ref, x1_ref, wg_ref, wu_ref, wd_ref, npost_ref, y_ref):
    j = pl.program_id(1)
    tf = wg_ref.shape[1]
    d = wd_ref.shape[1]
    acc_scr = y_ref

    @pl.when(j == 0)
    def _():
        acc_scr[...] = jnp.zeros_like(acc_scr)

    h = h_ref[...]
    acts = []
    for c in range(0, tf, FFN_CHUNK):
        gate = jnp.dot(h, wg_ref[:, c:c + FFN_CHUNK], preferred_element_type=F32)
        up = jnp.dot(h, wu_ref[:, c:c + FFN_CHUNK], preferred_element_type=F32)
        acts.append((gate * jax.nn.sigmoid(gate) * up).astype(BF16))
    act = jnp.concatenate(acts, axis=-1)
    for c in range(0, d, FFN_CHUNK):
        cols = slice(c, c + FFN_CHUNK)
        acc_scr[:, cols] += jnp.dot(act, wd_ref[:, cols], preferred_element_type=F32)

    @pl.when(j == pl.num_programs(1) - 1)
    def _():
        y_ref[...] = x1_ref[...] + _rms(acc_scr[...], npost_ref[...])


def _ffn(h2, x1, wg_tiles, wu_tiles, wd, n_post, tm):
    t, d = x1.shape
    n_tiles, _, tf = wg_tiles.shape
    return pl.pallas_call(
        _ffn_kernel,
        grid=(t // tm, n_tiles),
        in_specs=[
            pl.BlockSpec((tm, d), lambda i, j: (i, 0)),
            pl.BlockSpec((tm, d), lambda i, j: (i, 0)),
            pl.BlockSpec((None, d, tf), lambda i, j: (j, 0, 0)),
            pl.BlockSpec((None, d, tf), lambda i, j: (j, 0, 0)),
            pl.BlockSpec((tf, d), lambda i, j: (j, 0)),
            pl.BlockSpec((1, d), lambda i, j: (0, 0)),
        ],
        out_specs=pl.BlockSpec((tm, d), lambda i, j: (i, 0)),
        out_shape=jax.ShapeDtypeStruct((t, d), F32),
        compiler_params=_params("parallel", "arbitrary"),
        name="ffn",
    )(h2, x1, wg_tiles, wu_tiles, wd, n_post.reshape(1, d))


def _row_tile(t, want):
    tm = min(t, want)
    assert t % tm == 0
    return tm


INPROJ_ROWS, INPROJ_COLS = 1024, 512
HGRN_ROWS = 256
MERGE_ROWS = 256
FFN_ROWS, FFN_COLS = 1024, 256


def kernel(x_prompt, x_sample, cache_win_k, cache_win_v, state_hgrn, norm_pre_mix, w_in,
           hg_lb_logits, attn_out_gain, hg_norm_gain, w_out, norm_post_mix, norm_pre_ffn,
           w_gate, w_up, w_down, norm_post_ffn):
    depth = w_in.shape[0]
    bp, t, d = x_prompt.shape
    bs, ts, _ = x_sample.shape
    assert bp == 1 and ts == 1
    keep = min(max(w for w, _ in PATTERNS), t)
    logits = hg_lb_logits.astype(F32)

    yp = x_prompt.reshape(t, d)
    ys = x_sample.reshape(bs, d)
    outs = [[] for _ in range(6)]
    for l in range(depth):
        w_in_t = _col_tiles(w_in[l].astype(BF16), INPROJ_COLS)
        wg_t = _col_tiles(w_gate[l].astype(BF16), FFN_COLS)
        wu_t = _col_tiles(w_up[l].astype(BF16), FFN_COLS)
        w_out_b, wd_b = w_out[l].astype(BF16), w_down[l].astype(BF16)

        dils = tuple(dil for _, dil in PATTERNS)
        hg_cols, *dec = _inproj(yp, norm_pre_mix[l], w_in_t, _row_tile(t, INPROJ_ROWS), dils)
        o_list, lse_list = zip(*[_band_attn(qkv, dil) for qkv, dil in zip(dec, dils)])
        o_hg, s_fin = _hgrn_prompt(hg_cols, logits, l, _row_tile(t, HGRN_ROWS))
        x1, h2 = _merge(o_list, lse_list, o_hg, hg_cols, yp, attn_out_gain[l], hg_norm_gain[l],
                        w_out_b, norm_post_mix[l], norm_pre_ffn[l], _row_tile(t, MERGE_ROWS))
        kv_tiles = w_in_t[GROUP_WIDTH // INPROJ_COLS:3 * GROUP_WIDTH // INPROJ_COLS]
        (proj,) = _inproj(yp[t - keep:], norm_pre_mix[l], kv_tiles, _row_tile(keep, INPROJ_ROWS))
        yp = _ffn(h2, x1, wg_t, wu_t, wd_b, norm_post_ffn[l], _row_tile(t, FFN_ROWS))
        outs[0].append(proj[:, :GROUP_WIDTH].reshape(1, keep, N_HEADS, HEAD_DIM))
        outs[1].append(proj[:, GROUP_WIDTH:].reshape(1, keep, N_HEADS, HEAD_DIM))
        outs[2].append(s_fin.reshape(1, N_HEADS, HEAD_DIM, HEAD_DIM))

        (proj_s,) = _inproj(ys, norm_pre_mix[l], w_in_t, bs)
        hg_cols_s = proj_s[:, 3 * GROUP_WIDTH:]
        o_att_s = _sample_attn(proj_s, cache_win_k[l], cache_win_v[l])
        o_hg_s, s_new = _hgrn_step(hg_cols_s, logits, l, state_hgrn[l])
        x1s, h2s = _merge([o_att_s], [], o_hg_s, hg_cols_s, ys, attn_out_gain[l], hg_norm_gain[l],
                          w_out_b, norm_post_mix[l], norm_pre_ffn[l], bs)
        ys = _ffn(h2s, x1s, wg_t, wu_t, wd_b, norm_post_ffn[l], bs)
        outs[3].append(proj_s[:, GROUP_WIDTH:2 * GROUP_WIDTH].reshape(bs, 1, N_HEADS, HEAD_DIM))
        outs[4].append(proj_s[:, 2 * GROUP_WIDTH:3 * GROUP_WIDTH].reshape(bs, 1, N_HEADS, HEAD_DIM))
        outs[5].append(s_new)

    st = lambda xs: jnp.stack(xs)
    return (yp.reshape(1, t, d), ys.reshape(bs, 1, d), st(outs[0]), st(outs[1]), st(outs[2]),
            st(outs[3]), st(outs[4]), st(outs[5]))
```

```python
import functools

import jax
import jax.numpy as jnp
import numpy as np
from jax import lax
from jax.experimental import pallas as pl
from jax.experimental.pallas import tpu as pltpu

F32 = jnp.float32
BF16 = jnp.bfloat16

EPS = 1e-6
HEAD_DIM = 128
N_HEADS = 8
GROUP_WIDTH = N_HEADS * HEAD_DIM
N_PROJ_GROUPS = 7
PATTERNS = ((128, 1), (512, 4), (2048, 16))
BAND = 128
NEG_BIG = -1e30
LOG2E = 1.4426950408889634
HG_BLOCK = 128
SUBLANES = 8
MXU_COLS = 256
FFN_CHUNK = MXU_COLS
VMEM_LIMIT = 56 * 1024 * 1024


def _params(*sem):
    return pltpu.CompilerParams(dimension_semantics=sem, vmem_limit_bytes=VMEM_LIMIT)


def _rms(x, gain):
    ms = jnp.mean(x * x, axis=-1, keepdims=True)
    return x * lax.rsqrt(ms + EPS) * gain


def _dot_nt(a, b):
    return lax.dot_general(a, b, (((1,), (1,)), ((), ())), preferred_element_type=F32)


def _dot_tn(a, b):
    return lax.dot_general(a, b, (((0,), (0,)), ((), ())), preferred_element_type=F32)


def _inproj_kernel(dils, x_ref, g_ref, w_ref, o_ref, *rest):
    dec_refs, h_scr = rest[:len(dils)], rest[len(dils)]
    j = pl.program_id(1)
    tm = x_ref.shape[0]

    @pl.when(j == 0)
    def _():
        h_scr[...] = _rms(x_ref[...], g_ref[...]).astype(BF16)

    if not dils:
        o_ref[...] = jnp.dot(h_scr[...], w_ref[...], preferred_element_type=F32)
        return

    head_scr = rest[len(dils) + 1]
    plane_scrs = dict(zip(dils[1:-1], rest[len(dils) + 2:]))
    n_att = 3 * GROUP_WIDTH // w_ref.shape[1]

    @pl.when(j >= n_att)
    def _():
        o_ref[...] = jnp.dot(h_scr[...], w_ref[...], preferred_element_type=F32)

    @pl.when(j < n_att)
    def _():
        h = h_scr[...]
        for c in range(0, w_ref.shape[1], MXU_COLS):
            res = jnp.dot(h, w_ref[:, c:c + MXU_COLS], preferred_element_type=F32)
            for hh in range(c // HEAD_DIM, (c + MXU_COLS) // HEAD_DIM):
                sl = slice(hh * HEAD_DIM, (hh + 1) * HEAD_DIM)
                head_res = res[:, hh * HEAD_DIM - c:(hh + 1) * HEAD_DIM - c]
                head_scr[hh] = head_res
                dec_refs[0][0, :, sl] = head_res.astype(BF16)
                prev, planes = 1, {0: head_scr.at[hh]}
                for dil, dec_ref in zip(dils[1:], dec_refs[1:]):
                    ratio, new_planes = dil // prev, {}
                    for rp, src in planes.items():
                        for q in range(ratio):
                            r = rp + prev * q
                            val = src[pl.ds(q, tm // dil, stride=ratio), :]
                            dec_ref[r, :, sl] = val.astype(BF16)
                            if dil in plane_scrs:
                                plane_scrs[dil][hh * dil + r] = val
                                new_planes[r] = plane_scrs[dil].at[hh * dil + r]
                    prev, planes = dil, new_planes


def _inproj(x, gain, w_bf16, tm, tn, dils=(), rows=None, cols=None):
    d = x.shape[1]
    r0, r1 = rows or (0, x.shape[0])
    c0, c1 = cols or (0, w_bf16.shape[1])
    t, n = r1 - r0, c1 - c0
    assert r0 % tm == 0 and t % tm == 0 and c0 % tn == 0 and n % tn == 0
    i0, j0 = r0 // tm, c0 // tn
    if dils:
        assert (r0, c0) == (0, 0)
        n_att = 3 * GROUP_WIDTH // tn
        assert n == N_PROJ_GROUPS * GROUP_WIDTH and tn % MXU_COLS == 0 and GROUP_WIDTH % tn == 0
        assert dils[0] == 1 and all(b % a == 0 for a, b in zip(dils, dils[1:]))
        out_specs = [pl.BlockSpec((tm, tn), lambda i, j: (i, jnp.maximum(j - n_att, 0)))]
        out_shape = [jax.ShapeDtypeStruct((t, n - 3 * GROUP_WIDTH), F32)]
        for dil in dils:
            assert tm % (16 * dil) == 0
            out_specs.append(pl.BlockSpec((dil, tm // dil, tn),
                                          lambda i, j: (0, i, jnp.minimum(j, n_att - 1))))
            out_shape.append(jax.ShapeDtypeStruct((dil, t // dil, 3 * GROUP_WIDTH), BF16))
    else:
        out_specs = [pl.BlockSpec((tm, tn), lambda i, j: (i, j))]
        out_shape = [jax.ShapeDtypeStruct((t, n), F32)]
    return pl.pallas_call(
        functools.partial(_inproj_kernel, tuple(dils)),
        grid=(t // tm, n // tn),
        in_specs=[
            pl.BlockSpec((tm, d), lambda i, j: (i + i0, 0)),
            pl.BlockSpec((1, d), lambda i, j: (0, 0)),
            pl.BlockSpec((d, tn), lambda i, j: (0, j + j0)),
        ],
        out_specs=out_specs,
        out_shape=out_shape,
        scratch_shapes=[pltpu.VMEM((tm, d), BF16)]
        + ([pltpu.VMEM((tn // HEAD_DIM, tm, HEAD_DIM), F32)] if dils else [])
        + [pltpu.VMEM((tn // HEAD_DIM * dil, tm // dil, HEAD_DIM), F32) for dil in dils[1:-1]],
        compiler_params=_params("parallel", "arbitrary"),
        name="inproj",
    )(x, gain.reshape(1, d), w_bf16)


def _band_attn_kernel(dil, q_ref, k_ref, v_ref, o_ref, lse_ref, o_scr, kprev_scr, vprev_scr):
    i = pl.program_id(0)
    r = pl.program_id(1)
    bq = q_ref.shape[0]

    @pl.when(i == 0)
    def _():
        kprev_scr[r] = jnp.zeros(kprev_scr.shape[1:], kprev_scr.dtype)
        vprev_scr[r] = jnp.zeros(vprev_scr.shape[1:], vprev_scr.dtype)

    row = lax.broadcasted_iota(jnp.int32, (bq, 2 * bq), 0)
    col = lax.broadcasted_iota(jnp.int32, (bq, 2 * bq), 1)
    dist = row + bq - col
    lo = jnp.where(i == 0, bq, 0)
    valid = (dist >= 0) & (dist <= BAND) & (col >= lo)
    lane = lax.broadcasted_iota(jnp.int32, (bq, HEAD_DIM), 1)
    scale = HEAD_DIM ** -0.5
    lse_tile = jnp.zeros((bq, HEAD_DIM), F32)
    for h in range(N_HEADS):
        sl = slice(h * HEAD_DIM, (h + 1) * HEAD_DIM)
        q = q_ref[:, sl]
        k = jnp.concatenate([kprev_scr[r, :, sl], k_ref[:, sl]], axis=0)
        v = jnp.concatenate([vprev_scr[r, :, sl], v_ref[:, sl]], axis=0)
        s = jnp.where(valid, _dot_nt(q, k), NEG_BIG)
        m = jnp.max(s, axis=-1, keepdims=True)
        p = jnp.exp2((s - m) * (scale * LOG2E))
        l = jnp.sum(p, axis=-1, keepdims=True)
        o = jnp.dot(p.astype(BF16), v, preferred_element_type=F32)
        o_scr[h] = o * (1.0 / l)
        lse_tile = jnp.where(lane == h, m * scale + jnp.log(l), lse_tile)
    kprev_scr[r] = k_ref[...]
    vprev_scr[r] = v_ref[...]
    if dil == 1:
        o_ref[...] = o_scr[...]
        lse_ref[...] = lse_tile
    else:
        for rr in range(dil):
            @pl.when(r == rr)
            def _():
                for h in range(N_HEADS):
                    o_ref[h, pl.ds(rr, bq, stride=dil), :] = o_scr[h]
                lse_ref[pl.ds(rr, bq, stride=dil), :] = lse_tile


def _band_attn(qkv, dil):
    rows = qkv.shape[1]
    t = rows * dil
    bq = BAND
    assert qkv.shape[0] == dil and rows % bq == 0
    blk = (None, bq, GROUP_WIDTH)
    cur = lambda g: (lambda i, r: (r, i, g))
    carry = pltpu.VMEM((dil, bq, GROUP_WIDTH), qkv.dtype)
    return pl.pallas_call(
        functools.partial(_band_attn_kernel, dil),
        grid=(rows // bq, dil),
        in_specs=[pl.BlockSpec(blk, cur(g)) for g in range(3)],
        out_specs=[
            pl.BlockSpec((N_HEADS, dil * bq, HEAD_DIM), lambda i, r: (0, i, 0)),
            pl.BlockSpec((dil * bq, HEAD_DIM), lambda i, r: (i, 0)),
        ],
        out_shape=[
            jax.ShapeDtypeStruct((N_HEADS, t, HEAD_DIM), F32),
            jax.ShapeDtypeStruct((t, HEAD_DIM), F32),
        ],
        scratch_shapes=[pltpu.VMEM((N_HEADS, bq, HEAD_DIM), F32), carry, carry],
        compiler_params=_params("arbitrary", "arbitrary"),
        name=f"band_attn_d{dil}",
    )(qkv, qkv, qkv)


def _sample_attn_kernel(q_ref, kn_ref, vn_ref, *rest):
    n_pat = len(PATTERNS)
    cache_refs = rest[: 2 * n_pat]
    o_ref = rest[2 * n_pat]
    scale = HEAD_DIM ** -0.5
    for b in range(q_ref.shape[0]):
        q = q_ref[b]
        kn = kn_ref[b]
        vn = vn_ref[b]
        s_new = jnp.sum(q * kn, axis=-1, keepdims=True) * scale
        outs, lses = [], []
        for p in range(n_pat):
            kc = cache_refs[2 * p][b]
            vc = cache_refs[2 * p + 1][b]
            s = jnp.sum(kc * q[None], axis=-1, keepdims=True) * scale
            m = jnp.maximum(jnp.max(s, axis=0), s_new)
            e = jnp.exp(s - m[None])
            e_new = jnp.exp(s_new - m)
            l = jnp.sum(e, axis=0) + e_new
            o = jnp.sum(e * vc, axis=0) + e_new * vn
            outs.append(o * (1.0 / l))
            lses.append(m + jnp.log(l))
        top = functools.reduce(jnp.maximum, lses)
        ws = [jnp.exp(x - top) for x in lses]
        den = functools.reduce(lambda a, b: a + b, ws)
        o_ref[b] = functools.reduce(lambda a, b: a + b, [w * o for w, o in zip(ws, outs)]) * (1.0 / den)


def _sample_attn(proj_s, cache_k, cache_v):
    b = proj_s.shape[0]
    past = cache_k.shape[1]
    assert past == max(w for w, _ in PATTERNS), "every dilated key must lie inside the cached window"
    nb = _row_tile(b, SAMPLES_PER_STEP)
    head_blk = (nb, N_HEADS, HEAD_DIM)
    in_specs = [pl.BlockSpec(head_blk, lambda i: (i, 0, 0)) for _ in range(3)]
    args = [proj_s[:, g * GROUP_WIDTH:(g + 1) * GROUP_WIDTH].reshape(b, N_HEADS, HEAD_DIM)
            for g in range(3)]
    for _, dil in PATTERNS:
        rows = past // dil
        last_blk = rows // BAND - 1
        for c in (cache_k, cache_v):
            args.append(c.reshape(b, rows, dil, N_HEADS, HEAD_DIM))
            in_specs.append(pl.BlockSpec((nb, BAND, None, N_HEADS, HEAD_DIM),
                                         lambda i, lb=last_blk: (i, lb, 0, 0, 0)))
    out = pl.pallas_call(
        _sample_attn_kernel,
        grid=(b // nb,),
        in_specs=in_specs,
        out_specs=pl.BlockSpec(head_blk, lambda i: (i, 0, 0)),
        out_shape=jax.ShapeDtypeStruct((b, N_HEADS, HEAD_DIM), F32),
        compiler_params=_params("parallel"),
        name="sample_attn",
    )(*args)
    return out.reshape(b, GROUP_WIDTH)


def _lower_bound(logits, layer):
    top = jnp.max(logits, axis=0, keepdims=True)
    e = jnp.exp(logits - top)
    return jnp.sum(e[: layer + 1], axis=0, keepdims=True) / jnp.sum(e, axis=0, keepdims=True)


def _split3(x):
    a = x.astype(BF16)
    r = x - a.astype(F32)
    b = r.astype(BF16)
    c = (r - b.astype(F32)).astype(BF16)
    return a, b, c


def _pair_weights(c, q, k, level_of_pair):
    n = len(c)
    sub = lax.broadcasted_iota(jnp.int32, c[0].shape, 0)
    n_levels = (n * SUBLANES).bit_length() - 1
    q_all = jnp.concatenate(q, axis=0).astype(BF16)
    k_all = jnp.concatenate(k, axis=0).astype(BF16)
    a = jnp.where(level_of_pair == n_levels, _dot_nt(q_all, k_all), 0.0)
    for lev in range(n_levels):
        half = 1 << lev
        xs = []
        for i in range(n):
            if half >= SUBLANES:
                nv = 2 * half // SUBLANES
                first = (i // nv) * nv
                upper = (i % nv) >= nv // 2
                ref = c[first + nv // 2 - 1][SUBLANES - 1:SUBLANES]
                arg = c[i] - ref if upper else ref - c[i]
                base = q[i] if upper else k[i]
            else:
                upper = (sub & half) != 0
                if half == SUBLANES // 2:
                    ref = c[i][half - 1:half]
                elif half == 1:
                    ref = jnp.where(upper, pltpu.roll(c[i], 1, 0), c[i])
                else:
                    odd = (sub & 1) != 0
                    z = jnp.where(odd, c[i], pltpu.roll(c[i], SUBLANES - 1, 0))
                    ref = jnp.where(upper, pltpu.roll(z, 2, 0), z)
                diff = c[i] - ref
                arg = jnp.where(upper, diff, -diff)
                base = jnp.where(upper, q[i], k[i])
            xs.append(base * jnp.exp2(arg))
        x = jnp.concatenate(xs, axis=0).astype(BF16)
        a = jnp.where(level_of_pair == lev, _dot_nt(x, x), a)
    return a


def _hgrn_kernel(layer, q_ref, f_ref, v_ref, logit_ref, tri_ref, lv_ref, o_ref, state_ref,
                 st_scr, qq_scr, kk_scr, c2_scr, qt_scr, kd_scr):
    step = pl.program_id(0)
    rows = q_ref.shape[0]

    @pl.when(step == 0)
    def _():
        st_scr[...] = jnp.zeros_like(st_scr)

    lb = _lower_bound(logit_ref[...], layer)
    f = lb + (1.0 - lb) * jax.nn.sigmoid(f_ref[...])
    g = jnp.log(f)
    qq = jax.nn.silu(q_ref[...])
    kk = 1.0 - f
    tri = tri_ref[...]
    sums = functools.reduce(
        lambda a, b: a + b,
        [jnp.dot(tri, piece, preferred_element_type=F32) for piece in _split3(g)])
    cum = sums[:rows]
    rev = sums[rows:]
    c2 = cum * LOG2E
    c2_scr[...] = c2
    qq_scr[...] = qq
    kk_scr[...] = kk
    qt_scr[...] = qq * jnp.exp2(c2)
    kd_scr[...] = kk * jnp.exp(rev)

    n_groups = HG_BLOCK // SUBLANES

    def block(b, carry):
        base = pl.multiple_of(b * HG_BLOCK, HG_BLOCK)
        level_of_pair = lv_ref[...]
        for h in range(N_HEADS):
            sl = slice(h * HEAD_DIM, (h + 1) * HEAD_DIM)
            groups = lambda ref: [ref[pl.ds(base + SUBLANES * i, SUBLANES), sl] for i in range(n_groups)]
            c_b = groups(c2_scr)
            st = st_scr[h]
            inter = _dot_nt(qt_scr[pl.ds(base, HG_BLOCK), sl].astype(BF16), st.astype(BF16))
            pair = _pair_weights(c_b, groups(qq_scr), groups(kk_scr), level_of_pair)
            v_b = v_ref[pl.ds(base, HG_BLOCK), sl].astype(BF16)
            o_ref[pl.ds(base, HG_BLOCK), sl] = inter + jnp.dot(
                pair.astype(BF16), v_b, preferred_element_type=F32)
            decay = jnp.exp2(c_b[-1][SUBLANES - 1:SUBLANES])
            delta = _dot_tn(v_b, kd_scr[pl.ds(base, HG_BLOCK), sl].astype(BF16))
            st_scr[h] = st * decay + delta
        return carry

    lax.fori_loop(0, rows // HG_BLOCK, block, 0)

    @pl.when(step == pl.num_programs(0) - 1)
    def _():
        for h in range(N_HEADS):
            state_ref[h] = st_scr[h].T


def _block_tri(rows):
    idx = np.arange(rows)
    same = (idx[:, None] // HG_BLOCK) == (idx[None, :] // HG_BLOCK)
    lower = same & (idx[None, :] <= idx[:, None])
    upper = same & (idx[None, :] > idx[:, None])
    return jnp.asarray(np.concatenate([lower, upper], axis=0), dtype=BF16)


def _pair_levels():
    idx = np.arange(HG_BLOCK)
    t, s = idx[:, None], idx[None, :]
    n_levels = HG_BLOCK.bit_length() - 1
    level = np.floor(np.log2(np.maximum(t ^ s, 1))).astype(np.int32)
    return jnp.asarray(np.where(t == s, n_levels, np.where(s < t, level, -1)).astype(np.int32))


def _hgrn_prompt(proj, logits, layer, rows):
    t = proj.shape[0]
    assert t % rows == 0 and rows % HG_BLOCK == 0
    col = lambda g: pl.BlockSpec((rows, GROUP_WIDTH), lambda i, g=g: (i, g))
    n_l = logits.shape[0]
    return pl.pallas_call(
        functools.partial(_hgrn_kernel, layer),
        grid=(t // rows,),
        in_specs=[
            col(0), col(1), col(2),
            pl.BlockSpec((n_l, GROUP_WIDTH), lambda i: (0, 0)),
            pl.BlockSpec((2 * rows, rows), lambda i: (0, 0)),
            pl.BlockSpec((HG_BLOCK, HG_BLOCK), lambda i: (0, 0)),
        ],
        out_specs=[
            pl.BlockSpec((rows, GROUP_WIDTH), lambda i: (i, 0)),
            pl.BlockSpec((N_HEADS, HEAD_DIM, HEAD_DIM), lambda i: (0, 0, 0)),
        ],
        out_shape=[
            jax.ShapeDtypeStruct((t, GROUP_WIDTH), F32),
            jax.ShapeDtypeStruct((N_HEADS, HEAD_DIM, HEAD_DIM), F32),
        ],
        scratch_shapes=[pltpu.VMEM((N_HEADS, HEAD_DIM, HEAD_DIM), F32)]
        + [pltpu.VMEM((rows, GROUP_WIDTH), F32) for _ in range(5)],
        compiler_params=_params("arbitrary"),
        name="hgrn_prompt",
    )(proj, proj, proj, logits, _block_tri(rows), _pair_levels())


def _hgrn_step_kernel(layer, q_ref, f_ref, v_ref, logit_ref, s_ref, o_ref, snew_ref):
    lb = _lower_bound(logit_ref[...], layer)
    eye = (lax.broadcasted_iota(jnp.int32, (HEAD_DIM, HEAD_DIM), 0)
           == lax.broadcasted_iota(jnp.int32, (HEAD_DIM, HEAD_DIM), 1))

    def column(row):
        return jnp.sum(jnp.where(eye, jnp.broadcast_to(row, (HEAD_DIM, HEAD_DIM)), 0.0),
                       axis=-1, keepdims=True)

    for b in range(q_ref.shape[0]):
        f = lb + (1.0 - lb) * jax.nn.sigmoid(f_ref[b])
        qq = jax.nn.silu(q_ref[b])
        vv = v_ref[b]
        for h in range(N_HEADS):
            sl = slice(h * HEAD_DIM, (h + 1) * HEAD_DIM)
            f_col = column(f[:, sl])
            q_col = column(qq[:, sl])
            s_new = f_col * s_ref[b, h] + (1.0 - f_col) * vv[:, sl]
            snew_ref[b, h] = s_new
            o_ref[b, :, sl] = jnp.sum(q_col * s_new, axis=0, keepdims=True)


def _hgrn_step(proj_s, logits, layer, state):
    b = proj_s.shape[0]
    nb = _row_tile(b, SAMPLES_PER_STEP)
    p3 = proj_s.reshape(b, 1, proj_s.shape[1])
    row = (nb, 1, GROUP_WIDTH)
    n_l = logits.shape[0]
    st_blk = (nb, N_HEADS, HEAD_DIM, HEAD_DIM)
    o, s_new = pl.pallas_call(
        functools.partial(_hgrn_step_kernel, layer),
        grid=(b // nb,),
        in_specs=[pl.BlockSpec(row, lambda i, g=g: (i, 0, g)) for g in (0, 1, 2)]
        + [pl.BlockSpec((n_l, GROUP_WIDTH), lambda i: (0, 0)),
           pl.BlockSpec(st_blk, lambda i: (i, 0, 0, 0))],
        out_specs=[pl.BlockSpec(row, lambda i: (i, 0, 0)),
                   pl.BlockSpec(st_blk, lambda i: (i, 0, 0, 0))],
        out_shape=[jax.ShapeDtypeStruct((b, 1, GROUP_WIDTH), F32),
                   jax.ShapeDtypeStruct(state.shape, F32)],
        compiler_params=_params("parallel"),
        name="hgrn_step",
    )(p3, p3, p3, logits, state)
    return o.reshape(b, GROUP_WIDTH), s_new


def _merge_kernel(n_pat, *refs):
    o_refs = refs[:n_pat]
    lse_refs = refs[n_pat:2 * n_pat] if n_pat > 1 else ()
    k = 2 * n_pat if n_pat > 1 else n_pat
    (hg_ref, gate_ref, x_ref, again_ref, hgain_ref, wout_ref, npost_ref, npre_ref,
     x1_ref, h2_ref, buf_a, buf_b) = refs[k:]
    tm = x_ref.shape[0]
    i = pl.program_id(0)

    def merge_stage(dst):
        if n_pat > 1:
            lses = [r[...] for r in lse_refs]
            top = functools.reduce(jnp.maximum, lses)
            ws = [jnp.exp(x - top) for x in lses]
            inv = 1.0 / functools.reduce(lambda a, b: a + b, ws)
            ws = [w * inv for w in ws]
        att = []
        sq = jnp.zeros((tm, 1), F32)
        for h in range(N_HEADS):
            sl = slice(h * HEAD_DIM, (h + 1) * HEAD_DIM)
            if n_pat > 1:
                a = functools.reduce(
                    lambda x, y: x + y,
                    [w[:, h:h + 1] * r[h] for w, r in zip(ws, o_refs)])
            else:
                a = o_refs[0][:, sl]
            att.append(a)
            sq = sq + jnp.sum(a * a, axis=-1, keepdims=True)
            g = gate_ref[:, sl]
            dst[:, GROUP_WIDTH + h * HEAD_DIM:GROUP_WIDTH + (h + 1) * HEAD_DIM] = (
                _rms(hg_ref[:, sl], hgain_ref[...]) * (g * jax.nn.sigmoid(g))).astype(BF16)
        inv_rms = lax.rsqrt(sq * (1.0 / GROUP_WIDTH) + EPS)
        for h, a in enumerate(att):
            sl = slice(h * HEAD_DIM, (h + 1) * HEAD_DIM)
            dst[:, sl] = (a * inv_rms * again_ref[:, sl]).astype(BF16)

    def project_stage(src):
        mix = jnp.dot(src[...], wout_ref[...], preferred_element_type=F32)
        x1 = x_ref[...] + _rms(mix, npost_ref[...])
        x1_ref[...] = x1
        h2_ref[...] = _rms(x1, npre_ref[...]).astype(BF16)

    @pl.when(i == 0)
    def _():
        buf_b[...] = jnp.zeros_like(buf_b)

    @pl.when(i % 2 == 0)
    def _():
        project_stage(buf_b)
        merge_stage(buf_a)

    @pl.when(i % 2 == 1)
    def _():
        project_stage(buf_a)
        merge_stage(buf_b)


def _merge(o_list, lse_list, o_hg, proj, x, attn_gain, hg_gain, w_out_bf16, n_post, n_pre, tm):
    t, d = x.shape
    n_pat = len(o_list)
    n = t // tm
    cur = lambda i: jnp.minimum(i, n - 1)
    prev = lambda i: jnp.maximum(i - 1, 0)
    row_cur = lambda w: pl.BlockSpec((tm, w), lambda i: (cur(i), 0))
    row_prev = lambda w: pl.BlockSpec((tm, w), lambda i: (prev(i), 0))
    full = lambda a: pl.BlockSpec(a.shape, lambda i: (0,) * a.ndim)
    args = list(o_list)
    if n_pat > 1:
        in_specs = [pl.BlockSpec((N_HEADS, tm, HEAD_DIM), lambda i: (0, cur(i), 0)) for _ in o_list]
        args += list(lse_list)
        in_specs += [row_cur(HEAD_DIM) for _ in lse_list]
    else:
        in_specs = [row_cur(GROUP_WIDTH)]
    small = [attn_gain.reshape(1, -1), hg_gain.reshape(1, -1), w_out_bf16,
             n_post.reshape(1, -1), n_pre.reshape(1, -1)]
    args += [o_hg, proj, x] + small
    in_specs += [row_cur(GROUP_WIDTH), pl.BlockSpec((tm, GROUP_WIDTH), lambda i: (cur(i), 3)),
                 row_prev(d)]
    in_specs += [full(a) for a in small]
    return pl.pallas_call(
        functools.partial(_merge_kernel, n_pat),
        grid=(n + 1,),
        in_specs=in_specs,
        out_specs=[row_prev(d), row_prev(d)],
        out_shape=[jax.ShapeDtypeStruct((t, d), F32), jax.ShapeDtypeStruct((t, d), BF16)],
        scratch_shapes=[pltpu.VMEM((tm, 2 * GROUP_WIDTH), BF16) for _ in range(2)],
        compiler_params=_params("arbitrary"),
        name="merge_outproj",
    )(*args)


def _ffn_kernel(h_ref, x1_ref, wg_ref, wu_ref, wd_ref, npost_ref, y_ref):
    j = pl.program_id(1)
    tf = wg_ref.shape[1]
    d = wd_ref.shape[1]
    acc_scr = y_ref

    @pl.when(j == 0)
    def _():
        acc_scr[...] = jnp.zeros_like(acc_scr)

    h = h_ref[...]
    acts = []
    for c in range(0, tf, FFN_CHUNK):
        gate = jnp.dot(h, wg_ref[:, c:c + FFN_CHUNK], preferred_element_type=F32)
        up = jnp.dot(h, wu_ref[:, c:c + FFN_CHUNK], preferred_element_type=F32)
        acts.append((gate * jax.nn.sigmoid(gate) * up).astype(BF16))
    act = jnp.concatenate(acts, axis=-1)
    for c in range(0, d, FFN_CHUNK):
        cols = slice(c, c + FFN_CHUNK)
        acc_scr[:, cols] += jnp.dot(act, wd_ref[:, cols], preferred_element_type=F32)

    @pl.when(j == pl.num_programs(1) - 1)
    def _():
        y_ref[...] = x1_ref[...] + _rms(acc_scr[...], npost_ref[...])


def _ffn(h2, x1, wg, wu, wd, n_post, tm, tf):
    t, d = x1.shape
    dff = wg.shape[1]
    assert dff % tf == 0
    return pl.pallas_call(
        _ffn_kernel,
        grid=(t // tm, dff // tf),
        in_specs=[
            pl.BlockSpec((tm, d), lambda i, j: (i, 0)),
            pl.BlockSpec((tm, d), lambda i, j: (i, 0)),
            pl.BlockSpec((d, tf), lambda i, j: (0, j)),
            pl.BlockSpec((d, tf), lambda i, j: (0, j)),
            pl.BlockSpec((tf, d), lambda i, j: (j, 0)),
            pl.BlockSpec((1, d), lambda i, j: (0, 0)),
        ],
        out_specs=pl.BlockSpec((tm, d), lambda i, j: (i, 0)),
        out_shape=jax.ShapeDtypeStruct((t, d), F32),
        compiler_params=_params("parallel", "arbitrary"),
        name="ffn",
    )(h2, x1, wg, wu, wd, n_post.reshape(1, d))


def _row_tile(t, want):
    tm = min(t, want)
    assert t % tm == 0
    return tm


INPROJ_ROWS, INPROJ_COLS = 1024, 512
HGRN_ROWS = 256
MERGE_ROWS = 256
FFN_ROWS, FFN_COLS = 1024, 256
SAMPLES_PER_STEP = 4


def kernel(x_prompt, x_sample, cache_win_k, cache_win_v, state_hgrn, norm_pre_mix, w_in,
           hg_lb_logits, attn_out_gain, hg_norm_gain, w_out, norm_post_mix, norm_pre_ffn,
           w_gate, w_up, w_down, norm_post_ffn):
    depth = w_in.shape[0]
    bp, t, d = x_prompt.shape
    bs, ts, _ = x_sample.shape
    assert bp == 1 and ts == 1
    keep = min(max(w for w, _ in PATTERNS), t)
    logits = hg_lb_logits.astype(F32)

    yp = x_prompt.reshape(t, d)
    ys = x_sample.reshape(bs, d)
    outs = [[] for _ in range(6)]
    for l in range(depth):
        w_in_b, w_out_b = w_in[l].astype(BF16), w_out[l].astype(BF16)
        wg_b, wu_b, wd_b = w_gate[l].astype(BF16), w_up[l].astype(BF16), w_down[l].astype(BF16)

        dils = tuple(dil for _, dil in PATTERNS)
        hg_cols, *dec = _inproj(yp, norm_pre_mix[l], w_in_b, _row_tile(t, INPROJ_ROWS), INPROJ_COLS, dils)
        o_list, lse_list = zip(*[_band_attn(qkv, dil) for qkv, dil in zip(dec, dils)])
        o_hg, s_fin = _hgrn_prompt(hg_cols, logits, l, _row_tile(t, HGRN_ROWS))
        x1, h2 = _merge(o_list, lse_list, o_hg, hg_cols, yp, attn_out_gain[l], hg_norm_gain[l],
                        w_out_b, norm_post_mix[l], norm_pre_ffn[l], _row_tile(t, MERGE_ROWS))
        (proj,) = _inproj(yp, norm_pre_mix[l], w_in_b, _row_tile(keep, INPROJ_ROWS), INPROJ_COLS,
                          rows=(t - keep, t), cols=(GROUP_WIDTH, 3 * GROUP_WIDTH))
        yp = _ffn(h2, x1, wg_b, wu_b, wd_b, norm_post_ffn[l], _row_tile(t, FFN_ROWS), FFN_COLS)
        outs[0].append(proj[:, :GROUP_WIDTH].reshape(1, keep, N_HEADS, HEAD_DIM))
        outs[1].append(proj[:, GROUP_WIDTH:].reshape(1, keep, N_HEADS, HEAD_DIM))
        outs[2].append(s_fin.reshape(1, N_HEADS, HEAD_DIM, HEAD_DIM))

        (proj_s,) = _inproj(ys, norm_pre_mix[l], w_in_b, bs, 2 * INPROJ_COLS)
        hg_cols_s = proj_s[:, 3 * GROUP_WIDTH:]
        o_att_s = _sample_attn(proj_s, cache_win_k[l], cache_win_v[l])
        o_hg_s, s_new = _hgrn_step(hg_cols_s, logits, l, state_hgrn[l])
        x1s, h2s = _merge([o_att_s], [], o_hg_s, hg_cols_s, ys, attn_out_gain[l], hg_norm_gain[l],
                          w_out_b, norm_post_mix[l], norm_pre_ffn[l], bs)
        ys = _ffn(h2s, x1s, wg_b, wu_b, wd_b, norm_post_ffn[l], bs, 2 * FFN_COLS)
        outs[3].append(proj_s[:, GROUP_WIDTH:2 * GROUP_WIDTH].reshape(bs, 1, N_HEADS, HEAD_DIM))
        outs[4].append(proj_s[:, 2 * GROUP_WIDTH:3 * GROUP_WIDTH].reshape(bs, 1, N_HEADS, HEAD_DIM))
        outs[5].append(s_new)

    st = lambda xs: jnp.stack(xs)
    return (yp.reshape(1, t, d), ys.reshape(bs, 1, d), st(outs[0]), st(outs[1]), st(outs[2]),
            st(outs[3]), st(outs[4]), st(outs[5]))
```

```python
import functools

import jax
import jax.numpy as jnp
import numpy as np
from jax import lax
from jax.experimental import pallas as pl
from jax.experimental.pallas import tpu as pltpu

F32 = jnp.float32
BF16 = jnp.bfloat16

EPS = 1e-6
HEAD_DIM = 128
N_HEADS = 8
GROUP_WIDTH = N_HEADS * HEAD_DIM
N_PROJ_GROUPS = 7
PATTERNS = ((128, 1), (512, 4), (2048, 16))
BAND = 128
NEG_BIG = -1e30
LOG2E = 1.4426950408889634
HG_BLOCK = 128
SUBLANES = 8
MXU_COLS = 256
FFN_CHUNK = MXU_COLS
VMEM_LIMIT = 56 * 1024 * 1024


def _params(*sem):
    return pltpu.CompilerParams(dimension_semantics=sem, vmem_limit_bytes=VMEM_LIMIT)


def _rms(x, gain):
    ms = jnp.mean(x * x, axis=-1, keepdims=True)
    return x * lax.rsqrt(ms + EPS) * gain


def _dot_nt(a, b):
    return lax.dot_general(a, b, (((1,), (1,)), ((), ())), preferred_element_type=F32)


def _dot_tn(a, b):
    return lax.dot_general(a, b, (((0,), (0,)), ((), ())), preferred_element_type=F32)


def _inproj_kernel(dils, x_ref, g_ref, w_ref, o_ref, *rest):
    dec_refs, h_scr = rest[:len(dils)], rest[len(dils)]
    j = pl.program_id(1)
    tm = x_ref.shape[0]

    @pl.when(j == 0)
    def _():
        h_scr[...] = _rms(x_ref[...], g_ref[...]).astype(BF16)

    if not dils:
        o_ref[...] = jnp.dot(h_scr[...], w_ref[...], preferred_element_type=F32)
        return

    head_scr = rest[len(dils) + 1]
    plane_scrs = dict(zip(dils[1:-1], rest[len(dils) + 2:]))
    n_att = 3 * GROUP_WIDTH // w_ref.shape[1]

    @pl.when(j >= n_att)
    def _():
        o_ref[...] = jnp.dot(h_scr[...], w_ref[...], preferred_element_type=F32)

    @pl.when(j < n_att)
    def _():
        h = h_scr[...]
        for c in range(0, w_ref.shape[1], MXU_COLS):
            res = jnp.dot(h, w_ref[:, c:c + MXU_COLS], preferred_element_type=F32)
            for hh in range(c // HEAD_DIM, (c + MXU_COLS) // HEAD_DIM):
                sl = slice(hh * HEAD_DIM, (hh + 1) * HEAD_DIM)
                head_res = res[:, hh * HEAD_DIM - c:(hh + 1) * HEAD_DIM - c]
                head_scr[hh] = head_res
                dec_refs[0][0, :, sl] = head_res.astype(BF16)
                prev, planes = 1, {0: head_scr.at[hh]}
                for dil, dec_ref in zip(dils[1:], dec_refs[1:]):
                    ratio, new_planes = dil // prev, {}
                    for rp, src in planes.items():
                        for q in range(ratio):
                            r = rp + prev * q
                            val = src[pl.ds(q, tm // dil, stride=ratio), :]
                            dec_ref[r, :, sl] = val.astype(BF16)
                            if dil in plane_scrs:
                                plane_scrs[dil][hh * dil + r] = val
                                new_planes[r] = plane_scrs[dil].at[hh * dil + r]
                    prev, planes = dil, new_planes


def _inproj(x, gain, w_bf16, tm, tn, dils=(), rows=None, cols=None):
    d = x.shape[1]
    r0, r1 = rows or (0, x.shape[0])
    c0, c1 = cols or (0, w_bf16.shape[1])
    t, n = r1 - r0, c1 - c0
    assert r0 % tm == 0 and t % tm == 0 and c0 % tn == 0 and n % tn == 0
    i0, j0 = r0 // tm, c0 // tn
    if dils:
        assert (r0, c0) == (0, 0)
        n_att = 3 * GROUP_WIDTH // tn
        assert n == N_PROJ_GROUPS * GROUP_WIDTH and tn % MXU_COLS == 0 and GROUP_WIDTH % tn == 0
        assert dils[0] == 1 and all(b % a == 0 for a, b in zip(dils, dils[1:]))
        out_specs = [pl.BlockSpec((tm, tn), lambda i, j: (i, jnp.maximum(j - n_att, 0)))]
        out_shape = [jax.ShapeDtypeStruct((t, n - 3 * GROUP_WIDTH), F32)]
        for dil in dils:
            assert tm % (16 * dil) == 0
            out_specs.append(pl.BlockSpec((dil, tm // dil, tn),
                                          lambda i, j: (0, i, jnp.minimum(j, n_att - 1))))
            out_shape.append(jax.ShapeDtypeStruct((dil, t // dil, 3 * GROUP_WIDTH), BF16))
    else:
        out_specs = [pl.BlockSpec((tm, tn), lambda i, j: (i, j))]
        out_shape = [jax.ShapeDtypeStruct((t, n), F32)]
    return pl.pallas_call(
        functools.partial(_inproj_kernel, tuple(dils)),
        grid=(t // tm, n // tn),
        in_specs=[
            pl.BlockSpec((tm, d), lambda i, j: (i + i0, 0)),
            pl.BlockSpec((1, d), lambda i, j: (0, 0)),
            pl.BlockSpec((d, tn), lambda i, j: (0, j + j0)),
        ],
        out_specs=out_specs,
        out_shape=out_shape,
        scratch_shapes=[pltpu.VMEM((tm, d), BF16)]
        + ([pltpu.VMEM((tn // HEAD_DIM, tm, HEAD_DIM), F32)] if dils else [])
        + [pltpu.VMEM((tn // HEAD_DIM * dil, tm // dil, HEAD_DIM), F32) for dil in dils[1:-1]],
        compiler_params=_params("parallel", "arbitrary"),
        name="inproj",
    )(x, gain.reshape(1, d), w_bf16)


def _band_attn_kernel(dil, nres, nsub, q_ref, k_ref, v_ref, o_ref, lse_ref,
                      o_scr, lse_scr, kprev_scr, vprev_scr):
    i = pl.program_id(0)
    rg = pl.program_id(1)
    bq = BAND
    row = lax.broadcasted_iota(jnp.int32, (bq, 2 * bq), 0)
    col = lax.broadcasted_iota(jnp.int32, (bq, 2 * bq), 1)
    dist = row + bq - col
    in_band = (dist >= 0) & (dist <= BAND)
    first_valid = in_band & (col >= jnp.where(i == 0, bq, 0))
    lane = lax.broadcasted_iota(jnp.int32, (bq, HEAD_DIM), 1)
    scale = HEAD_DIM ** -0.5
    for rr in range(nres):
        r = rg * nres + rr

        @pl.when(i == 0)
        def _():
            kprev_scr[r] = jnp.zeros(kprev_scr.shape[1:], kprev_scr.dtype)
            vprev_scr[r] = jnp.zeros(vprev_scr.shape[1:], vprev_scr.dtype)

        for sub in range(nsub):
            rows = slice(sub * bq, (sub + 1) * bq)
            before = slice((sub - 1) * bq, sub * bq)
            valid = first_valid if sub == 0 else in_band
            lse_tile = jnp.zeros((bq, HEAD_DIM), F32)
            for h in range(N_HEADS):
                sl = slice(h * HEAD_DIM, (h + 1) * HEAD_DIM)
                q = q_ref[rr, rows, sl]
                k_prev = kprev_scr[r, :, sl] if sub == 0 else k_ref[rr, before, sl]
                v_prev = vprev_scr[r, :, sl] if sub == 0 else v_ref[rr, before, sl]
                k = jnp.concatenate([k_prev, k_ref[rr, rows, sl]], axis=0)
                v = jnp.concatenate([v_prev, v_ref[rr, rows, sl]], axis=0)
                s = jnp.where(valid, _dot_nt(q, k), NEG_BIG)
                m = jnp.max(s, axis=-1, keepdims=True)
                p = jnp.exp2((s - m) * (scale * LOG2E))
                l = jnp.sum(p, axis=-1, keepdims=True)
                o = jnp.dot(p.astype(BF16), v, preferred_element_type=F32)
                o_scr[rr * nsub + sub, h] = o * (1.0 / l)
                lse_tile = jnp.where(lane == h, m * scale + jnp.log(l), lse_tile)
            lse_scr[rr * nsub + sub] = lse_tile
        last = slice((nsub - 1) * bq, nsub * bq)
        kprev_scr[r] = k_ref[rr, last, :]
        vprev_scr[r] = v_ref[rr, last, :]

    def write_out(first_residue):
        for rr in range(nres):
            for sub in range(nsub):
                start = dil * sub * bq + first_residue + rr
                dst = pl.ds(start, bq, stride=dil) if dil > 1 else pl.ds(start, bq)
                for h in range(N_HEADS):
                    o_ref[h, dst, :] = o_scr[rr * nsub + sub, h]
                lse_ref[dst, :] = lse_scr[rr * nsub + sub]

    if dil == nres:
        write_out(0)
    else:
        for g in range(dil // nres):
            pl.when(rg == g)(functools.partial(write_out, g * nres))


def _band_attn(qkv, dil):
    rows = qkv.shape[1]
    t = rows * dil
    bq = BAND
    nres = min(dil, ATTN_BLOCKS_PER_STEP)
    nsub = min(ATTN_BLOCKS_PER_STEP // nres, rows // bq)
    assert qkv.shape[0] == dil and rows % (nsub * bq) == 0 and dil % nres == 0
    blk = (nres, nsub * bq, GROUP_WIDTH)
    cur = lambda g: (lambda i, rg: (rg, i, g))
    carry = pltpu.VMEM((dil, bq, GROUP_WIDTH), qkv.dtype)
    return pl.pallas_call(
        functools.partial(_band_attn_kernel, dil, nres, nsub),
        grid=(rows // (nsub * bq), dil // nres),
        in_specs=[pl.BlockSpec(blk, cur(g)) for g in range(3)],
        out_specs=[
            pl.BlockSpec((N_HEADS, dil * nsub * bq, HEAD_DIM), lambda i, rg: (0, i, 0)),
            pl.BlockSpec((dil * nsub * bq, HEAD_DIM), lambda i, rg: (i, 0)),
        ],
        out_shape=[
            jax.ShapeDtypeStruct((N_HEADS, t, HEAD_DIM), F32),
            jax.ShapeDtypeStruct((t, HEAD_DIM), F32),
        ],
        scratch_shapes=[pltpu.VMEM((nres * nsub, N_HEADS, bq, HEAD_DIM), F32),
                        pltpu.VMEM((nres * nsub, bq, HEAD_DIM), F32), carry, carry],
        compiler_params=_params("arbitrary", "arbitrary"),
        name=f"band_attn_d{dil}",
    )(qkv, qkv, qkv)


def _sample_attn_kernel(q_ref, kn_ref, vn_ref, *rest):
    n_pat = len(PATTERNS)
    cache_refs = rest[: 2 * n_pat]
    o_ref = rest[2 * n_pat]
    scale = HEAD_DIM ** -0.5
    for b in range(q_ref.shape[0]):
        q = q_ref[b]
        kn = kn_ref[b]
        vn = vn_ref[b]
        s_new = jnp.sum(q * kn, axis=-1, keepdims=True) * scale
        outs, lses = [], []
        for p in range(n_pat):
            kc = cache_refs[2 * p][b]
            vc = cache_refs[2 * p + 1][b]
            s = jnp.sum(kc * q[None], axis=-1, keepdims=True) * scale
            m = jnp.maximum(jnp.max(s, axis=0), s_new)
            e = jnp.exp(s - m[None])
            e_new = jnp.exp(s_new - m)
            l = jnp.sum(e, axis=0) + e_new
            o = jnp.sum(e * vc, axis=0) + e_new * vn
            outs.append(o * (1.0 / l))
            lses.append(m + jnp.log(l))
        top = functools.reduce(jnp.maximum, lses)
        ws = [jnp.exp(x - top) for x in lses]
        den = functools.reduce(lambda a, b: a + b, ws)
        o_ref[b] = functools.reduce(lambda a, b: a + b, [w * o for w, o in zip(ws, outs)]) * (1.0 / den)


def _sample_attn(proj_s, cache_k, cache_v):
    b = proj_s.shape[0]
    past = cache_k.shape[1]
    assert past == max(w for w, _ in PATTERNS), "every dilated key must lie inside the cached window"
    nb = _row_tile(b, SAMPLES_PER_STEP)
    head_blk = (nb, N_HEADS, HEAD_DIM)
    in_specs = [pl.BlockSpec(head_blk, lambda i: (i, 0, 0)) for _ in range(3)]
    args = [proj_s[:, g * GROUP_WIDTH:(g + 1) * GROUP_WIDTH].reshape(b, N_HEADS, HEAD_DIM)
            for g in range(3)]
    for _, dil in PATTERNS:
        rows = past // dil
        last_blk = rows // BAND - 1
        for c in (cache_k, cache_v):
            args.append(c.reshape(b, rows, dil, N_HEADS, HEAD_DIM))
            in_specs.append(pl.BlockSpec((nb, BAND, None, N_HEADS, HEAD_DIM),
                                         lambda i, lb=last_blk: (i, lb, 0, 0, 0)))
    out = pl.pallas_call(
        _sample_attn_kernel,
        grid=(b // nb,),
        in_specs=in_specs,
        out_specs=pl.BlockSpec(head_blk, lambda i: (i, 0, 0)),
        out_shape=jax.ShapeDtypeStruct((b, N_HEADS, HEAD_DIM), F32),
        compiler_params=_params("parallel"),
        name="sample_attn",
    )(*args)
    return out.reshape(b, GROUP_WIDTH)


def _lower_bound(logits, layer):
    top = jnp.max(logits, axis=0, keepdims=True)
    e = jnp.exp(logits - top)
    return jnp.sum(e[: layer + 1], axis=0, keepdims=True) / jnp.sum(e, axis=0, keepdims=True)


def _split3(x):
    a = x.astype(BF16)
    r = x - a.astype(F32)
    b = r.astype(BF16)
    c = (r - b.astype(F32)).astype(BF16)
    return a, b, c


def _pair_weights(c, q, k, level_of_pair):
    n = len(c)
    sub = lax.broadcasted_iota(jnp.int32, c[0].shape, 0)
    n_levels = (n * SUBLANES).bit_length() - 1
    q_all = jnp.concatenate(q, axis=0).astype(BF16)
    k_all = jnp.concatenate(k, axis=0).astype(BF16)
    a = jnp.where(level_of_pair == n_levels, _dot_nt(q_all, k_all), 0.0)
    for lev in range(n_levels):
        half = 1 << lev
        xs = []
        for i in range(n):
            if half >= SUBLANES:
                nv = 2 * half // SUBLANES
                first = (i // nv) * nv
                upper = (i % nv) >= nv // 2
                ref = c[first + nv // 2 - 1][SUBLANES - 1:SUBLANES]
                arg = c[i] - ref if upper else ref - c[i]
                base = q[i] if upper else k[i]
            else:
                upper = (sub & half) != 0
                if half == SUBLANES // 2:
                    ref = c[i][half - 1:half]
                elif half == 1:
                    ref = jnp.where(upper, pltpu.roll(c[i], 1, 0), c[i])
                else:
                    odd = (sub & 1) != 0
                    z = jnp.where(odd, c[i], pltpu.roll(c[i], SUBLANES - 1, 0))
                    ref = jnp.where(upper, pltpu.roll(z, 2, 0), z)
                diff = c[i] - ref
                arg = jnp.where(upper, diff, -diff)
                base = jnp.where(upper, q[i], k[i])
            xs.append(base * jnp.exp2(arg))
        x = jnp.concatenate(xs, axis=0).astype(BF16)
        a = jnp.where(level_of_pair == lev, _dot_nt(x, x), a)
    return a


def _gated_norm(o, gain, gate):
    return (_rms(o, gain) * (gate * jax.nn.sigmoid(gate))).astype(BF16)


def _hgrn_kernel(layer, q_ref, f_ref, v_ref, gate_ref, logit_ref, gain_ref, tri_ref, lv_ref,
                 o_ref, state_ref, st_scr, qq_scr, kk_scr, c2_scr, qt_scr, kd_scr):
    step = pl.program_id(0)
    rows = q_ref.shape[0]

    @pl.when(step == 0)
    def _():
        st_scr[...] = jnp.zeros_like(st_scr)

    lb = _lower_bound(logit_ref[...], layer)
    f = lb + (1.0 - lb) * jax.nn.sigmoid(f_ref[...])
    g = jnp.log(f)
    qq = jax.nn.silu(q_ref[...])
    kk = 1.0 - f
    tri = tri_ref[...]
    sums = functools.reduce(
        lambda a, b: a + b,
        [jnp.dot(tri, piece, preferred_element_type=F32) for piece in _split3(g)])
    cum = sums[:rows]
    rev = sums[rows:]
    c2 = cum * LOG2E
    c2_scr[...] = c2
    qq_scr[...] = qq
    kk_scr[...] = kk
    qt_scr[...] = qq * jnp.exp2(c2)
    kd_scr[...] = kk * jnp.exp(rev)

    n_groups = HG_BLOCK // SUBLANES

    def block(b, carry):
        base = pl.multiple_of(b * HG_BLOCK, HG_BLOCK)
        level_of_pair = lv_ref[...]
        for h in range(N_HEADS):
            sl = slice(h * HEAD_DIM, (h + 1) * HEAD_DIM)
            groups = lambda ref: [ref[pl.ds(base + SUBLANES * i, SUBLANES), sl] for i in range(n_groups)]
            c_b = groups(c2_scr)
            st = st_scr[h]
            inter = _dot_nt(qt_scr[pl.ds(base, HG_BLOCK), sl].astype(BF16), st.astype(BF16))
            pair = _pair_weights(c_b, groups(qq_scr), groups(kk_scr), level_of_pair)
            v_b = v_ref[pl.ds(base, HG_BLOCK), sl].astype(BF16)
            o = inter + jnp.dot(pair.astype(BF16), v_b, preferred_element_type=F32)
            o_ref[pl.ds(base, HG_BLOCK), sl] = _gated_norm(
                o, gain_ref[...], gate_ref[pl.ds(base, HG_BLOCK), sl])
            decay = jnp.exp2(c_b[-1][SUBLANES - 1:SUBLANES])
            delta = _dot_tn(v_b, kd_scr[pl.ds(base, HG_BLOCK), sl].astype(BF16))
            st_scr[h] = st * decay + delta
        return carry

    lax.fori_loop(0, rows // HG_BLOCK, block, 0)

    @pl.when(step == pl.num_programs(0) - 1)
    def _():
        for h in range(N_HEADS):
            state_ref[h] = st_scr[h].T


def _block_tri(rows):
    idx = np.arange(rows)
    same = (idx[:, None] // HG_BLOCK) == (idx[None, :] // HG_BLOCK)
    lower = same & (idx[None, :] <= idx[:, None])
    upper = same & (idx[None, :] > idx[:, None])
    return jnp.asarray(np.concatenate([lower, upper], axis=0), dtype=BF16)


def _pair_levels():
    idx = np.arange(HG_BLOCK)
    t, s = idx[:, None], idx[None, :]
    n_levels = HG_BLOCK.bit_length() - 1
    level = np.floor(np.log2(np.maximum(t ^ s, 1))).astype(np.int32)
    return jnp.asarray(np.where(t == s, n_levels, np.where(s < t, level, -1)).astype(np.int32))


def _hgrn_prompt(proj, logits, layer, hg_gain, rows):
    t = proj.shape[0]
    assert t % rows == 0 and rows % HG_BLOCK == 0
    col = lambda g: pl.BlockSpec((rows, GROUP_WIDTH), lambda i, g=g: (i, g))
    n_l = logits.shape[0]
    return pl.pallas_call(
        functools.partial(_hgrn_kernel, layer),
        grid=(t // rows,),
        in_specs=[
            col(0), col(1), col(2), col(3),
            pl.BlockSpec((n_l, GROUP_WIDTH), lambda i: (0, 0)),
            pl.BlockSpec((1, HEAD_DIM), lambda i: (0, 0)),
            pl.BlockSpec((2 * rows, rows), lambda i: (0, 0)),
            pl.BlockSpec((HG_BLOCK, HG_BLOCK), lambda i: (0, 0)),
        ],
        out_specs=[
            pl.BlockSpec((rows, GROUP_WIDTH), lambda i: (i, 0)),
            pl.BlockSpec((N_HEADS, HEAD_DIM, HEAD_DIM), lambda i: (0, 0, 0)),
        ],
        out_shape=[
            jax.ShapeDtypeStruct((t, GROUP_WIDTH), BF16),
            jax.ShapeDtypeStruct((N_HEADS, HEAD_DIM, HEAD_DIM), F32),
        ],
        scratch_shapes=[pltpu.VMEM((N_HEADS, HEAD_DIM, HEAD_DIM), F32)]
        + [pltpu.VMEM((rows, GROUP_WIDTH), F32) for _ in range(5)],
        compiler_params=_params("arbitrary"),
        name="hgrn_prompt",
    )(proj, proj, proj, proj, logits, hg_gain.reshape(1, HEAD_DIM), _block_tri(rows), _pair_levels())


def _hgrn_step_kernel(layer, q_ref, f_ref, v_ref, gate_ref, logit_ref, gain_ref, s_ref, o_ref, snew_ref):
    lb = _lower_bound(logit_ref[...], layer)
    eye = (lax.broadcasted_iota(jnp.int32, (HEAD_DIM, HEAD_DIM), 0)
           == lax.broadcasted_iota(jnp.int32, (HEAD_DIM, HEAD_DIM), 1))

    def column(row):
        return jnp.sum(jnp.where(eye, jnp.broadcast_to(row, (HEAD_DIM, HEAD_DIM)), 0.0),
                       axis=-1, keepdims=True)

    for b in range(q_ref.shape[0]):
        f = lb + (1.0 - lb) * jax.nn.sigmoid(f_ref[b])
        qq = jax.nn.silu(q_ref[b])
        vv = v_ref[b]
        for h in range(N_HEADS):
            sl = slice(h * HEAD_DIM, (h + 1) * HEAD_DIM)
            f_col = column(f[:, sl])
            q_col = column(qq[:, sl])
            s_new = f_col * s_ref[b, h] + (1.0 - f_col) * vv[:, sl]
            snew_ref[b, h] = s_new
            o = jnp.sum(q_col * s_new, axis=0, keepdims=True)
            o_ref[b, :, sl] = _gated_norm(o, gain_ref[...], gate_ref[b][:, sl]).astype(F32)


def _hgrn_step(proj_s, logits, layer, hg_gain, state):
    b = proj_s.shape[0]
    nb = _row_tile(b, SAMPLES_PER_STEP)
    p3 = proj_s.reshape(b, 1, proj_s.shape[1])
    row = (nb, 1, GROUP_WIDTH)
    n_l = logits.shape[0]
    st_blk = (nb, N_HEADS, HEAD_DIM, HEAD_DIM)
    o, s_new = pl.pallas_call(
        functools.partial(_hgrn_step_kernel, layer),
        grid=(b // nb,),
        in_specs=[pl.BlockSpec(row, lambda i, g=g: (i, 0, g)) for g in (0, 1, 2, 3)]
        + [pl.BlockSpec((n_l, GROUP_WIDTH), lambda i: (0, 0)),
           pl.BlockSpec((1, HEAD_DIM), lambda i: (0, 0)),
           pl.BlockSpec(st_blk, lambda i: (i, 0, 0, 0))],
        out_specs=[pl.BlockSpec(row, lambda i: (i, 0, 0)),
                   pl.BlockSpec(st_blk, lambda i: (i, 0, 0, 0))],
        out_shape=[jax.ShapeDtypeStruct((b, 1, GROUP_WIDTH), F32),
                   jax.ShapeDtypeStruct(state.shape, F32)],
        compiler_params=_params("parallel"),
        name="hgrn_step",
    )(p3, p3, p3, p3, logits, hg_gain.reshape(1, HEAD_DIM), state)
    return o.reshape(b, GROUP_WIDTH), s_new


def _merge_kernel(n_pat, *refs):
    o_refs = refs[:n_pat]
    lse_refs = refs[n_pat:2 * n_pat] if n_pat > 1 else ()
    k = 2 * n_pat if n_pat > 1 else n_pat
    (hg_ref, x_ref, again_ref, wout_ref, npost_ref, npre_ref,
     x1_ref, h2_ref, buf_a, buf_b) = refs[k:]
    tm = x_ref.shape[0]
    i = pl.program_id(0)

    def merge_stage(dst):
        if n_pat > 1:
            lses = [r[...] for r in lse_refs]
            top = functools.reduce(jnp.maximum, lses)
            ws = [jnp.exp(x - top) for x in lses]
            inv = 1.0 / functools.reduce(lambda a, b: a + b, ws)
            ws = [w * inv for w in ws]
        att = []
        sq = jnp.zeros((tm, 1), F32)
        for h in range(N_HEADS):
            sl = slice(h * HEAD_DIM, (h + 1) * HEAD_DIM)
            if n_pat > 1:
                a = functools.reduce(
                    lambda x, y: x + y,
                    [w[:, h:h + 1] * r[h] for w, r in zip(ws, o_refs)])
            else:
                a = o_refs[0][:, sl]
            att.append(a)
            sq = sq + jnp.sum(a * a, axis=-1, keepdims=True)
        dst[:, GROUP_WIDTH:] = hg_ref[...].astype(BF16)
        inv_rms = lax.rsqrt(sq * (1.0 / GROUP_WIDTH) + EPS)
        for h, a in enumerate(att):
            sl = slice(h * HEAD_DIM, (h + 1) * HEAD_DIM)
            dst[:, sl] = (a * inv_rms * again_ref[:, sl]).astype(BF16)

    def project_stage(src):
        mix = jnp.dot(src[...], wout_ref[...], preferred_element_type=F32)
        x1 = x_ref[...] + _rms(mix, npost_ref[...])
        x1_ref[...] = x1
        h2_ref[...] = _rms(x1, npre_ref[...]).astype(BF16)

    @pl.when(i == 0)
    def _():
        buf_b[...] = jnp.zeros_like(buf_b)

    @pl.when(i % 2 == 0)
    def _():
        project_stage(buf_b)
        merge_stage(buf_a)

    @pl.when(i % 2 == 1)
    def _():
        project_stage(buf_a)
        merge_stage(buf_b)


def _merge(o_list, lse_list, o_hg, x, attn_gain, w_out_bf16, n_post, n_pre, tm):
    t, d = x.shape
    n_pat = len(o_list)
    n = t // tm
    cur = lambda i: jnp.minimum(i, n - 1)
    prev = lambda i: jnp.maximum(i - 1, 0)
    row_cur = lambda w: pl.BlockSpec((tm, w), lambda i: (cur(i), 0))
    row_prev = lambda w: pl.BlockSpec((tm, w), lambda i: (prev(i), 0))
    full = lambda a: pl.BlockSpec(a.shape, lambda i: (0,) * a.ndim)
    args = list(o_list)
    if n_pat > 1:
        in_specs = [pl.BlockSpec((N_HEADS, tm, HEAD_DIM), lambda i: (0, cur(i), 0)) for _ in o_list]
        args += list(lse_list)
        in_specs += [row_cur(HEAD_DIM) for _ in lse_list]
    else:
        in_specs = [row_cur(GROUP_WIDTH)]
    small = [attn_gain.reshape(1, -1), w_out_bf16, n_post.reshape(1, -1), n_pre.reshape(1, -1)]
    args += [o_hg, x] + small
    in_specs += [row_cur(GROUP_WIDTH), row_prev(d)]
    in_specs += [full(a) for a in small]
    return pl.pallas_call(
        functools.partial(_merge_kernel, n_pat),
        grid=(n + 1,),
        in_specs=in_specs,
        out_specs=[row_prev(d), row_prev(d)],
        out_shape=[jax.ShapeDtypeStruct((t, d), F32), jax.ShapeDtypeStruct((t, d), BF16)],
        scratch_shapes=[pltpu.VMEM((tm, 2 * GROUP_WIDTH), BF16) for _ in range(2)],
        compiler_params=_params("arbitrary"),
        name="merge_outproj",
    )(*args)


def _ffn_kernel(h_ref, x1_ref, wg_ref, wu_ref, wd_ref, npost_ref, y_ref):
    j = pl.program_id(1)
    tf = wg_ref.shape[1]
    d = wd_ref.shape[1]
    acc_scr = y_ref

    @pl.when(j == 0)
    def _():
        acc_scr[...] = jnp.zeros_like(acc_scr)

    h = h_ref[...]
    acts = []
    for c in range(0, tf, FFN_CHUNK):
        gate = jnp.dot(h, wg_ref[:, c:c + FFN_CHUNK], preferred_element_type=F32)
        up = jnp.dot(h, wu_ref[:, c:c + FFN_CHUNK], preferred_element_type=F32)
        acts.append((gate * jax.nn.sigmoid(gate) * up).astype(BF16))
    act = jnp.concatenate(acts, axis=-1)
    for c in range(0, d, FFN_CHUNK):
        cols = slice(c, c + FFN_CHUNK)
        acc_scr[:, cols] += jnp.dot(act, wd_ref[:, cols], preferred_element_type=F32)

    @pl.when(j == pl.num_programs(1) - 1)
    def _():
        y_ref[...] = x1_ref[...] + _rms(acc_scr[...], npost_ref[...])


def _ffn(h2, x1, wg, wu, wd, n_post, tm, tf):
    t, d = x1.shape
    dff = wg.shape[1]
    assert dff % tf == 0
    return pl.pallas_call(
        _ffn_kernel,
        grid=(t // tm, dff // tf),
        in_specs=[
            pl.BlockSpec((tm, d), lambda i, j: (i, 0)),
            pl.BlockSpec((tm, d), lambda i, j: (i, 0)),
            pl.BlockSpec((d, tf), lambda i, j: (0, j)),
            pl.BlockSpec((d, tf), lambda i, j: (0, j)),
            pl.BlockSpec((tf, d), lambda i, j: (j, 0)),
            pl.BlockSpec((1, d), lambda i, j: (0, 0)),
        ],
        out_specs=pl.BlockSpec((tm, d), lambda i, j: (i, 0)),
        out_shape=jax.ShapeDtypeStruct((t, d), F32),
        compiler_params=_params("parallel", "arbitrary"),
        name="ffn",
    )(h2, x1, wg, wu, wd, n_post.reshape(1, d))


def _row_tile(t, want):
    tm = min(t, want)
    assert t % tm == 0
    return tm


INPROJ_ROWS, INPROJ_COLS = 1024, 512
HGRN_ROWS = 256
MERGE_ROWS = 256
FFN_ROWS, FFN_COLS = 1024, 256
SAMPLES_PER_STEP = 4
ATTN_BLOCKS_PER_STEP = 4


def kernel(x_prompt, x_sample, cache_win_k, cache_win_v, state_hgrn, norm_pre_mix, w_in,
           hg_lb_logits, attn_out_gain, hg_norm_gain, w_out, norm_post_mix, norm_pre_ffn,
           w_gate, w_up, w_down, norm_post_ffn):
    depth = w_in.shape[0]
    bp, t, d = x_prompt.shape
    bs, ts, _ = x_sample.shape
    assert bp == 1 and ts == 1
    keep = min(max(w for w, _ in PATTERNS), t)
    logits = hg_lb_logits.astype(F32)

    yp = x_prompt.reshape(t, d)
    ys = x_sample.reshape(bs, d)
    outs = [[] for _ in range(6)]
    for l in range(depth):
        w_in_b, w_out_b = w_in[l].astype(BF16), w_out[l].astype(BF16)
        wg_b, wu_b, wd_b = w_gate[l].astype(BF16), w_up[l].astype(BF16), w_down[l].astype(BF16)

        dils = tuple(dil for _, dil in PATTERNS)
        hg_cols, *dec = _inproj(yp, norm_pre_mix[l], w_in_b, _row_tile(t, INPROJ_ROWS), INPROJ_COLS, dils)
        o_list, lse_list = zip(*[_band_attn(qkv, dil) for qkv, dil in zip(dec, dils)])
        o_hg, s_fin = _hgrn_prompt(hg_cols, logits, l, hg_norm_gain[l], _row_tile(t, HGRN_ROWS))
        x1, h2 = _merge(o_list, lse_list, o_hg, yp, attn_out_gain[l],
                        w_out_b, norm_post_mix[l], norm_pre_ffn[l], _row_tile(t, MERGE_ROWS))
        (proj,) = _inproj(yp, norm_pre_mix[l], w_in_b, _row_tile(keep, INPROJ_ROWS), INPROJ_COLS,
                          rows=(t - keep, t), cols=(GROUP_WIDTH, 3 * GROUP_WIDTH))
        yp = _ffn(h2, x1, wg_b, wu_b, wd_b, norm_post_ffn[l], _row_tile(t, FFN_ROWS), FFN_COLS)
        outs[0].append(proj[:, :GROUP_WIDTH].reshape(1, keep, N_HEADS, HEAD_DIM))
        outs[1].append(proj[:, GROUP_WIDTH:].reshape(1, keep, N_HEADS, HEAD_DIM))
        outs[2].append(s_fin.reshape(1, N_HEADS, HEAD_DIM, HEAD_DIM))

        (proj_s,) = _inproj(ys, norm_pre_mix[l], w_in_b, bs, 2 * INPROJ_COLS)
        hg_cols_s = proj_s[:, 3 * GROUP_WIDTH:]
        o_att_s = _sample_attn(proj_s, cache_win_k[l], cache_win_v[l])
        o_hg_s, s_new = _hgrn_step(hg_cols_s, logits, l, hg_norm_gain[l], state_hgrn[l])
        x1s, h2s = _merge([o_att_s], [], o_hg_s, ys, attn_out_gain[l],
                          w_out_b, norm_post_mix[l], norm_pre_ffn[l], bs)
        ys = _ffn(h2s, x1s, wg_b, wu_b, wd_b, norm_post_ffn[l], bs, 2 * FFN_COLS)
        outs[3].append(proj_s[:, GROUP_WIDTH:2 * GROUP_WIDTH].reshape(bs, 1, N_HEADS, HEAD_DIM))
        outs[4].append(proj_s[:, 2 * GROUP_WIDTH:3 * GROUP_WIDTH].reshape(bs, 1, N_HEADS, HEAD_DIM))
        outs[5].append(s_new)

    st = lambda xs: jnp.stack(xs)
    return (yp.reshape(1, t, d), ys.reshape(bs, 1, d), st(outs[0]), st(outs[1]), st(outs[2]),
            st(outs[3]), st(outs[4]), st(outs[5]))
```

```python
import functools

import jax
import jax.numpy as jnp
import numpy as np
from jax import lax
from jax.experimental import pallas as pl
from jax.experimental.pallas import tpu as pltpu

F32 = jnp.float32
BF16 = jnp.bfloat16

EPS = 1e-6
HEAD_DIM = 128
N_HEADS = 8
GROUP_WIDTH = N_HEADS * HEAD_DIM
N_PROJ_GROUPS = 7
PATTERNS = ((128, 1), (512, 4), (2048, 16))
BAND = 128
NEG_BIG = -1e30
LOG2E = 1.4426950408889634
HG_BLOCK = 128
SUBLANES = 8
MXU_COLS = 256
FFN_CHUNK = MXU_COLS
VMEM_LIMIT = 56 * 1024 * 1024


def _params(*sem):
    return pltpu.CompilerParams(dimension_semantics=sem, vmem_limit_bytes=VMEM_LIMIT)


def _rms(x, gain):
    ms = jnp.mean(x * x, axis=-1, keepdims=True)
    return x * lax.rsqrt(ms + EPS) * gain


def _dot_nt(a, b):
    return lax.dot_general(a, b, (((1,), (1,)), ((), ())), preferred_element_type=F32)


def _dot_tn(a, b):
    return lax.dot_general(a, b, (((0,), (0,)), ((), ())), preferred_element_type=F32)


def _inproj_kernel(dils, x_ref, g_ref, w_ref, o_ref, *rest):
    dec_refs, h_scr = rest[:len(dils)], rest[len(dils)]
    j = pl.program_id(1)
    tm = x_ref.shape[0]

    @pl.when(j == 0)
    def _():
        h_scr[...] = _rms(x_ref[...], g_ref[...]).astype(BF16)

    if not dils:
        o_ref[...] = jnp.dot(h_scr[...], w_ref[...], preferred_element_type=F32)
        return

    head_scr = rest[len(dils) + 1]
    plane_scrs = dict(zip(dils[1:-1], rest[len(dils) + 2:]))
    n_att = 3 * GROUP_WIDTH // w_ref.shape[1]

    @pl.when(j >= n_att)
    def _():
        o_ref[...] = jnp.dot(h_scr[...], w_ref[...], preferred_element_type=F32)

    @pl.when(j < n_att)
    def _():
        h = h_scr[...]
        for c in range(0, w_ref.shape[1], MXU_COLS):
            res = jnp.dot(h, w_ref[:, c:c + MXU_COLS], preferred_element_type=F32)
            for hh in range(c // HEAD_DIM, (c + MXU_COLS) // HEAD_DIM):
                sl = slice(hh * HEAD_DIM, (hh + 1) * HEAD_DIM)
                head_res = res[:, hh * HEAD_DIM - c:(hh + 1) * HEAD_DIM - c]
                head_scr[hh] = head_res
                dec_refs[0][0, :, sl] = head_res.astype(BF16)
                prev, planes = 1, {0: head_scr.at[hh]}
                for dil, dec_ref in zip(dils[1:], dec_refs[1:]):
                    ratio, new_planes = dil // prev, {}
                    for rp, src in planes.items():
                        for q in range(ratio):
                            r = rp + prev * q
                            val = src[pl.ds(q, tm // dil, stride=ratio), :]
                            dec_ref[r, :, sl] = val.astype(BF16)
                            if dil in plane_scrs:
                                plane_scrs[dil][hh * dil + r] = val
                                new_planes[r] = plane_scrs[dil].at[hh * dil + r]
                    prev, planes = dil, new_planes


def _inproj(x, gain, w_bf16, tm, tn, dils=(), rows=None, cols=None):
    d = x.shape[1]
    r0, r1 = rows or (0, x.shape[0])
    c0, c1 = cols or (0, w_bf16.shape[1])
    t, n = r1 - r0, c1 - c0
    assert r0 % tm == 0 and t % tm == 0 and c0 % tn == 0 and n % tn == 0
    i0, j0 = r0 // tm, c0 // tn
    if dils:
        assert (r0, c0) == (0, 0)
        n_att = 3 * GROUP_WIDTH // tn
        assert n == N_PROJ_GROUPS * GROUP_WIDTH and tn % MXU_COLS == 0 and GROUP_WIDTH % tn == 0
        assert dils[0] == 1 and all(b % a == 0 for a, b in zip(dils, dils[1:]))
        out_specs = [pl.BlockSpec((tm, tn), lambda i, j: (i, jnp.maximum(j - n_att, 0)))]
        out_shape = [jax.ShapeDtypeStruct((t, n - 3 * GROUP_WIDTH), F32)]
        for dil in dils:
            assert tm % (16 * dil) == 0
            out_specs.append(pl.BlockSpec((dil, tm // dil, tn),
                                          lambda i, j: (0, i, jnp.minimum(j, n_att - 1))))
            out_shape.append(jax.ShapeDtypeStruct((dil, t // dil, 3 * GROUP_WIDTH), BF16))
    else:
        out_specs = [pl.BlockSpec((tm, tn), lambda i, j: (i, j))]
        out_shape = [jax.ShapeDtypeStruct((t, n), F32)]
    return pl.pallas_call(
        functools.partial(_inproj_kernel, tuple(dils)),
        grid=(t // tm, n // tn),
        in_specs=[
            pl.BlockSpec((tm, d), lambda i, j: (i + i0, 0)),
            pl.BlockSpec((1, d), lambda i, j: (0, 0)),
            pl.BlockSpec((d, tn), lambda i, j: (0, j + j0)),
        ],
        out_specs=out_specs,
        out_shape=out_shape,
        scratch_shapes=[pltpu.VMEM((tm, d), BF16)]
        + ([pltpu.VMEM((tn // HEAD_DIM, tm, HEAD_DIM), F32)] if dils else [])
        + [pltpu.VMEM((tn // HEAD_DIM * dil, tm // dil, HEAD_DIM), F32) for dil in dils[1:-1]],
        compiler_params=_params("parallel", "arbitrary"),
        name="inproj",
    )(x, gain.reshape(1, d), w_bf16)


def _band_attn_kernel(dil, nres, nsub, q_ref, k_ref, v_ref, o_ref, lse_ref,
                      o_scr, lse_scr, kprev_scr, vprev_scr):
    i = pl.program_id(0)
    rg = pl.program_id(1)
    bq = BAND
    row = lax.broadcasted_iota(jnp.int32, (bq, 2 * bq), 0)
    col = lax.broadcasted_iota(jnp.int32, (bq, 2 * bq), 1)
    dist = row + bq - col
    in_band = (dist >= 0) & (dist <= BAND)
    first_valid = in_band & (col >= jnp.where(i == 0, bq, 0))
    lane = lax.broadcasted_iota(jnp.int32, (bq, HEAD_DIM), 1)
    scale = HEAD_DIM ** -0.5
    for rr in range(nres):
        r = rg * nres + rr

        @pl.when(i == 0)
        def _():
            kprev_scr[r] = jnp.zeros(kprev_scr.shape[1:], kprev_scr.dtype)
            vprev_scr[r] = jnp.zeros(vprev_scr.shape[1:], vprev_scr.dtype)

        for sub in range(nsub):
            rows = slice(sub * bq, (sub + 1) * bq)
            before = slice((sub - 1) * bq, sub * bq)
            valid = first_valid if sub == 0 else in_band
            lse_tile = jnp.zeros((bq, HEAD_DIM), F32)
            for h in range(N_HEADS):
                sl = slice(h * HEAD_DIM, (h + 1) * HEAD_DIM)
                q = q_ref[rr, rows, sl]
                k_prev = kprev_scr[r, :, sl] if sub == 0 else k_ref[rr, before, sl]
                v_prev = vprev_scr[r, :, sl] if sub == 0 else v_ref[rr, before, sl]
                k = jnp.concatenate([k_prev, k_ref[rr, rows, sl]], axis=0)
                v = jnp.concatenate([v_prev, v_ref[rr, rows, sl]], axis=0)
                s = jnp.where(valid, _dot_nt(q, k), NEG_BIG)
                m = jnp.max(s, axis=-1, keepdims=True)
                p = jnp.exp2((s - m) * (scale * LOG2E))
                l = jnp.sum(p, axis=-1, keepdims=True)
                o = jnp.dot(p.astype(BF16), v, preferred_element_type=F32)
                o_scr[rr * nsub + sub, h] = o * (1.0 / l)
                lse_tile = jnp.where(lane == h, m * scale + jnp.log(l), lse_tile)
            lse_scr[rr * nsub + sub] = lse_tile
        last = slice((nsub - 1) * bq, nsub * bq)
        kprev_scr[r] = k_ref[rr, last, :]
        vprev_scr[r] = v_ref[rr, last, :]

    def write_out(first_residue):
        for rr in range(nres):
            for sub in range(nsub):
                start = dil * sub * bq + first_residue + rr
                dst = pl.ds(start, bq, stride=dil) if dil > 1 else pl.ds(start, bq)
                for h in range(N_HEADS):
                    o_ref[h, dst, :] = o_scr[rr * nsub + sub, h]
                lse_ref[dst, :] = lse_scr[rr * nsub + sub]

    if dil == nres:
        write_out(0)
    else:
        for g in range(dil // nres):
            pl.when(rg == g)(functools.partial(write_out, g * nres))


def _band_attn(qkv, dil):
    rows = qkv.shape[1]
    t = rows * dil
    bq = BAND
    nres = min(dil, ATTN_BLOCKS_PER_STEP)
    nsub = min(ATTN_BLOCKS_PER_STEP // nres, rows // bq)
    assert qkv.shape[0] == dil and rows % (nsub * bq) == 0 and dil % nres == 0
    blk = (nres, nsub * bq, GROUP_WIDTH)
    cur = lambda g: (lambda i, rg: (rg, i, g))
    carry = pltpu.VMEM((dil, bq, GROUP_WIDTH), qkv.dtype)
    return pl.pallas_call(
        functools.partial(_band_attn_kernel, dil, nres, nsub),
        grid=(rows // (nsub * bq), dil // nres),
        in_specs=[pl.BlockSpec(blk, cur(g)) for g in range(3)],
        out_specs=[
            pl.BlockSpec((N_HEADS, dil * nsub * bq, HEAD_DIM), lambda i, rg: (0, i, 0)),
            pl.BlockSpec((dil * nsub * bq, HEAD_DIM), lambda i, rg: (i, 0)),
        ],
        out_shape=[
            jax.ShapeDtypeStruct((N_HEADS, t, HEAD_DIM), F32),
            jax.ShapeDtypeStruct((t, HEAD_DIM), F32),
        ],
        scratch_shapes=[pltpu.VMEM((nres * nsub, N_HEADS, bq, HEAD_DIM), F32),
                        pltpu.VMEM((nres * nsub, bq, HEAD_DIM), F32), carry, carry],
        compiler_params=_params("arbitrary", "arbitrary"),
        name=f"band_attn_d{dil}",
    )(qkv, qkv, qkv)


def _sample_attn_kernel(q_ref, kn_ref, vn_ref, *rest):
    n_pat = len(PATTERNS)
    cache_refs = rest[: 2 * n_pat]
    o_ref = rest[2 * n_pat]
    scale = HEAD_DIM ** -0.5
    for b in range(q_ref.shape[0]):
        q = q_ref[b]
        kn = kn_ref[b]
        vn = vn_ref[b]
        s_new = jnp.sum(q * kn, axis=-1, keepdims=True) * scale
        outs, lses = [], []
        for p in range(n_pat):
            kc = cache_refs[2 * p][b]
            vc = cache_refs[2 * p + 1][b]
            s = jnp.sum(kc * q[None], axis=-1, keepdims=True) * scale
            m = jnp.maximum(jnp.max(s, axis=0), s_new)
            e = jnp.exp(s - m[None])
            e_new = jnp.exp(s_new - m)
            l = jnp.sum(e, axis=0) + e_new
            o = jnp.sum(e * vc, axis=0) + e_new * vn
            outs.append(o * (1.0 / l))
            lses.append(m + jnp.log(l))
        top = functools.reduce(jnp.maximum, lses)
        ws = [jnp.exp(x - top) for x in lses]
        den = functools.reduce(lambda a, b: a + b, ws)
        o_ref[b] = functools.reduce(lambda a, b: a + b, [w * o for w, o in zip(ws, outs)]) * (1.0 / den)


def _sample_attn(proj_s, cache_k, cache_v):
    b = proj_s.shape[0]
    past = cache_k.shape[1]
    assert past == max(w for w, _ in PATTERNS), "every dilated key must lie inside the cached window"
    nb = _row_tile(b, SAMPLES_PER_STEP)
    head_blk = (nb, N_HEADS, HEAD_DIM)
    in_specs = [pl.BlockSpec(head_blk, lambda i: (i, 0, 0)) for _ in range(3)]
    args = [proj_s[:, g * GROUP_WIDTH:(g + 1) * GROUP_WIDTH].reshape(b, N_HEADS, HEAD_DIM)
            for g in range(3)]
    for _, dil in PATTERNS:
        rows = past // dil
        last_blk = rows // BAND - 1
        for c in (cache_k, cache_v):
            args.append(c.reshape(b, rows, dil, N_HEADS, HEAD_DIM))
            in_specs.append(pl.BlockSpec((nb, BAND, None, N_HEADS, HEAD_DIM),
                                         lambda i, lb=last_blk: (i, lb, 0, 0, 0)))
    out = pl.pallas_call(
        _sample_attn_kernel,
        grid=(b // nb,),
        in_specs=in_specs,
        out_specs=pl.BlockSpec(head_blk, lambda i: (i, 0, 0)),
        out_shape=jax.ShapeDtypeStruct((b, N_HEADS, HEAD_DIM), F32),
        compiler_params=_params("parallel"),
        name="sample_attn",
    )(*args)
    return out.reshape(b, GROUP_WIDTH)


def _lower_bound(logits, layer):
    top = jnp.max(logits, axis=0, keepdims=True)
    e = jnp.exp(logits - top)
    return jnp.sum(e[: layer + 1], axis=0, keepdims=True) / jnp.sum(e, axis=0, keepdims=True)


def _split3(x):
    a = x.astype(BF16)
    r = x - a.astype(F32)
    b = r.astype(BF16)
    c = (r - b.astype(F32)).astype(BF16)
    return a, b, c


def _pair_weights(c, q, k, level_of_pair):
    n = len(c)
    sub = lax.broadcasted_iota(jnp.int32, c[0].shape, 0)
    n_levels = (n * SUBLANES).bit_length() - 1
    q_all = jnp.concatenate(q, axis=0).astype(BF16)
    k_all = jnp.concatenate(k, axis=0).astype(BF16)
    a = jnp.where(level_of_pair == n_levels, _dot_nt(q_all, k_all), 0.0)
    for lev in range(n_levels):
        half = 1 << lev
        xs = []
        for i in range(n):
            if half >= SUBLANES:
                nv = 2 * half // SUBLANES
                first = (i // nv) * nv
                upper = (i % nv) >= nv // 2
                ref = c[first + nv // 2 - 1][SUBLANES - 1:SUBLANES]
                arg = c[i] - ref if upper else ref - c[i]
                base = q[i] if upper else k[i]
            else:
                upper = (sub & half) != 0
                if half == SUBLANES // 2:
                    ref = c[i][half - 1:half]
                elif half == 1:
                    ref = jnp.where(upper, pltpu.roll(c[i], 1, 0), c[i])
                else:
                    odd = (sub & 1) != 0
                    z = jnp.where(odd, c[i], pltpu.roll(c[i], SUBLANES - 1, 0))
                    ref = jnp.where(upper, pltpu.roll(z, 2, 0), z)
                diff = c[i] - ref
                arg = jnp.where(upper, diff, -diff)
                base = jnp.where(upper, q[i], k[i])
            xs.append(base * jnp.exp2(arg))
        x = jnp.concatenate(xs, axis=0).astype(BF16)
        a = jnp.where(level_of_pair == lev, _dot_nt(x, x), a)
    return a


def _gated_norm(o, gain, gate):
    return (_rms(o, gain) * (gate * jax.nn.sigmoid(gate))).astype(BF16)


def _hgrn_kernel(layer, q_ref, f_ref, v_ref, gate_ref, logit_ref, gain_ref, tri_ref, lv_ref,
                 o_ref, state_ref, st_scr, qq_scr, kk_scr, c2_scr, qt_scr, kd_scr):
    step = pl.program_id(0)
    rows = q_ref.shape[0]

    @pl.when(step == 0)
    def _():
        st_scr[...] = jnp.zeros_like(st_scr)

    lb = _lower_bound(logit_ref[...], layer)
    f = lb + (1.0 - lb) * jax.nn.sigmoid(f_ref[...])
    g = jnp.log(f)
    qq = jax.nn.silu(q_ref[...])
    kk = 1.0 - f
    tri = tri_ref[...]
    sums = functools.reduce(
        lambda a, b: a + b,
        [jnp.dot(tri, piece, preferred_element_type=F32) for piece in _split3(g)])
    cum = sums[:rows]
    rev = sums[rows:]
    c2 = cum * LOG2E
    c2_scr[...] = c2
    qq_scr[...] = qq
    kk_scr[...] = kk
    qt_scr[...] = qq * jnp.exp2(c2)
    kd_scr[...] = kk * jnp.exp(rev)

    n_groups = HG_BLOCK // SUBLANES

    def block(b, carry):
        base = pl.multiple_of(b * HG_BLOCK, HG_BLOCK)
        level_of_pair = lv_ref[...]
        for h in range(N_HEADS):
            sl = slice(h * HEAD_DIM, (h + 1) * HEAD_DIM)
            groups = lambda ref: [ref[pl.ds(base + SUBLANES * i, SUBLANES), sl] for i in range(n_groups)]
            c_b = groups(c2_scr)
            st = st_scr[h]
            inter = _dot_nt(qt_scr[pl.ds(base, HG_BLOCK), sl].astype(BF16), st.astype(BF16))
            pair = _pair_weights(c_b, groups(qq_scr), groups(kk_scr), level_of_pair)
            v_b = v_ref[pl.ds(base, HG_BLOCK), sl].astype(BF16)
            o = inter + jnp.dot(pair.astype(BF16), v_b, preferred_element_type=F32)
            o_ref[pl.ds(base, HG_BLOCK), sl] = _gated_norm(
                o, gain_ref[...], gate_ref[pl.ds(base, HG_BLOCK), sl])
            decay = jnp.exp2(c_b[-1][SUBLANES - 1:SUBLANES])
            delta = _dot_tn(v_b, kd_scr[pl.ds(base, HG_BLOCK), sl].astype(BF16))
            st_scr[h] = st * decay + delta
        return carry

    for b in range(rows // HG_BLOCK):
        block(b, 0)

    @pl.when(step == pl.num_programs(0) - 1)
    def _():
        for h in range(N_HEADS):
            state_ref[h] = st_scr[h].T


def _block_tri(rows):
    idx = np.arange(rows)
    same = (idx[:, None] // HG_BLOCK) == (idx[None, :] // HG_BLOCK)
    lower = same & (idx[None, :] <= idx[:, None])
    upper = same & (idx[None, :] > idx[:, None])
    return jnp.asarray(np.concatenate([lower, upper], axis=0), dtype=BF16)


def _pair_levels():
    idx = np.arange(HG_BLOCK)
    t, s = idx[:, None], idx[None, :]
    n_levels = HG_BLOCK.bit_length() - 1
    level = np.floor(np.log2(np.maximum(t ^ s, 1))).astype(np.int32)
    return jnp.asarray(np.where(t == s, n_levels, np.where(s < t, level, -1)).astype(np.int32))


def _hgrn_prompt(proj, logits, layer, hg_gain, rows):
    t = proj.shape[0]
    assert t % rows == 0 and rows % HG_BLOCK == 0
    col = lambda g: pl.BlockSpec((rows, GROUP_WIDTH), lambda i, g=g: (i, g))
    n_l = logits.shape[0]
    return pl.pallas_call(
        functools.partial(_hgrn_kernel, layer),
        grid=(t // rows,),
        in_specs=[
            col(0), col(1), col(2), col(3),
            pl.BlockSpec((n_l, GROUP_WIDTH), lambda i: (0, 0)),
            pl.BlockSpec((1, HEAD_DIM), lambda i: (0, 0)),
            pl.BlockSpec((2 * rows, rows), lambda i: (0, 0)),
            pl.BlockSpec((HG_BLOCK, HG_BLOCK), lambda i: (0, 0)),
        ],
        out_specs=[
            pl.BlockSpec((rows, GROUP_WIDTH), lambda i: (i, 0)),
            pl.BlockSpec((N_HEADS, HEAD_DIM, HEAD_DIM), lambda i: (0, 0, 0)),
        ],
        out_shape=[
            jax.ShapeDtypeStruct((t, GROUP_WIDTH), BF16),
            jax.ShapeDtypeStruct((N_HEADS, HEAD_DIM, HEAD_DIM), F32),
        ],
        scratch_shapes=[pltpu.VMEM((N_HEADS, HEAD_DIM, HEAD_DIM), F32)]
        + [pltpu.VMEM((rows, GROUP_WIDTH), F32) for _ in range(5)],
        compiler_params=_params("arbitrary"),
        name="hgrn_prompt",
    )(proj, proj, proj, proj, logits, hg_gain.reshape(1, HEAD_DIM), _block_tri(rows), _pair_levels())


def _hgrn_step_kernel(layer, q_ref, f_ref, v_ref, gate_ref, logit_ref, gain_ref, s_ref, o_ref, snew_ref,
                      o_scr, gate_scr):
    lb = _lower_bound(logit_ref[...], layer)
    nb = q_ref.shape[0]
    eye = (lax.broadcasted_iota(jnp.int32, (HEAD_DIM, HEAD_DIM), 0)
           == lax.broadcasted_iota(jnp.int32, (HEAD_DIM, HEAD_DIM), 1))

    def column(row):
        return jnp.sum(jnp.where(eye, jnp.broadcast_to(row, (HEAD_DIM, HEAD_DIM)), 0.0),
                       axis=-1, keepdims=True)

    for b in range(nb):
        f = lb + (1.0 - lb) * jax.nn.sigmoid(f_ref[b])
        qq = jax.nn.silu(q_ref[b])
        vv = v_ref[b]
        gate_scr[b:b + 1, :] = gate_ref[b]
        for h in range(N_HEADS):
            sl = slice(h * HEAD_DIM, (h + 1) * HEAD_DIM)
            f_col = column(f[:, sl])
            q_col = column(qq[:, sl])
            s_new = f_col * s_ref[b, h] + (1.0 - f_col) * vv[:, sl]
            snew_ref[b, h] = s_new
            o_scr[b:b + 1, sl] = jnp.sum(q_col * s_new, axis=0, keepdims=True)
    for h in range(N_HEADS):
        sl = slice(h * HEAD_DIM, (h + 1) * HEAD_DIM)
        res = _gated_norm(o_scr[:, sl], gain_ref[...], gate_scr[:, sl]).astype(F32)
        for b in range(nb):
            o_ref[b, :, sl] = res[b:b + 1]


def _hgrn_step(proj_s, logits, layer, hg_gain, state):
    b = proj_s.shape[0]
    nb = _row_tile(b, SAMPLES_PER_STEP)
    p3 = proj_s.reshape(b, 1, proj_s.shape[1])
    row = (nb, 1, GROUP_WIDTH)
    n_l = logits.shape[0]
    st_blk = (nb, N_HEADS, HEAD_DIM, HEAD_DIM)
    o, s_new = pl.pallas_call(
        functools.partial(_hgrn_step_kernel, layer),
        grid=(b // nb,),
        in_specs=[pl.BlockSpec(row, lambda i, g=g: (i, 0, g)) for g in (0, 1, 2, 3)]
        + [pl.BlockSpec((n_l, GROUP_WIDTH), lambda i: (0, 0)),
           pl.BlockSpec((1, HEAD_DIM), lambda i: (0, 0)),
           pl.BlockSpec(st_blk, lambda i: (i, 0, 0, 0))],
        out_specs=[pl.BlockSpec(row, lambda i: (i, 0, 0)),
                   pl.BlockSpec(st_blk, lambda i: (i, 0, 0, 0))],
        out_shape=[jax.ShapeDtypeStruct((b, 1, GROUP_WIDTH), F32),
                   jax.ShapeDtypeStruct(state.shape, F32)],
        scratch_shapes=[pltpu.VMEM((nb, GROUP_WIDTH), F32), pltpu.VMEM((nb, GROUP_WIDTH), F32)],
        compiler_params=_params("parallel"),
        name="hgrn_step",
    )(p3, p3, p3, p3, logits, hg_gain.reshape(1, HEAD_DIM), state)
    return o.reshape(b, GROUP_WIDTH), s_new


def _merge_kernel(n_pat, *refs):
    o_refs = refs[:n_pat]
    lse_refs = refs[n_pat:2 * n_pat] if n_pat > 1 else ()
    k = 2 * n_pat if n_pat > 1 else n_pat
    (hg_ref, x_ref, again_ref, wout_ref, npost_ref, npre_ref,
     x1_ref, h2_ref, buf_a, buf_b) = refs[k:]
    tm = x_ref.shape[0]
    i = pl.program_id(0)

    def merge_stage(dst):
        if n_pat > 1:
            lses = [r[...] for r in lse_refs]
            top = functools.reduce(jnp.maximum, lses)
            ws = [jnp.exp(x - top) for x in lses]
            inv = 1.0 / functools.reduce(lambda a, b: a + b, ws)
            ws = [w * inv for w in ws]
        att = []
        sq = jnp.zeros((tm, 1), F32)
        for h in range(N_HEADS):
            sl = slice(h * HEAD_DIM, (h + 1) * HEAD_DIM)
            if n_pat > 1:
                a = functools.reduce(
                    lambda x, y: x + y,
                    [w[:, h:h + 1] * r[h] for w, r in zip(ws, o_refs)])
            else:
                a = o_refs[0][:, sl]
            att.append(a)
            sq = sq + jnp.sum(a * a, axis=-1, keepdims=True)
        dst[:, GROUP_WIDTH:] = hg_ref[...].astype(BF16)
        inv_rms = lax.rsqrt(sq * (1.0 / GROUP_WIDTH) + EPS)
        for h, a in enumerate(att):
            sl = slice(h * HEAD_DIM, (h + 1) * HEAD_DIM)
            dst[:, sl] = (a * inv_rms * again_ref[:, sl]).astype(BF16)

    def project_stage(src):
        mix = jnp.dot(src[...], wout_ref[...], preferred_element_type=F32)
        x1 = x_ref[...] + _rms(mix, npost_ref[...])
        x1_ref[...] = x1
        h2_ref[...] = _rms(x1, npre_ref[...]).astype(BF16)

    @pl.when(i == 0)
    def _():
        buf_b[...] = jnp.zeros_like(buf_b)

    @pl.when(i % 2 == 0)
    def _():
        project_stage(buf_b)
        merge_stage(buf_a)

    @pl.when(i % 2 == 1)
    def _():
        project_stage(buf_a)
        merge_stage(buf_b)


def _merge(o_list, lse_list, o_hg, x, attn_gain, w_out_bf16, n_post, n_pre, tm):
    t, d = x.shape
    n_pat = len(o_list)
    n = t // tm
    cur = lambda i: jnp.minimum(i, n - 1)
    prev = lambda i: jnp.maximum(i - 1, 0)
    row_cur = lambda w: pl.BlockSpec((tm, w), lambda i: (cur(i), 0))
    row_prev = lambda w: pl.BlockSpec((tm, w), lambda i: (prev(i), 0))
    full = lambda a: pl.BlockSpec(a.shape, lambda i: (0,) * a.ndim, pipeline_mode=pl.Buffered(1))
    args = list(o_list)
    if n_pat > 1:
        in_specs = [pl.BlockSpec((N_HEADS, tm, HEAD_DIM), lambda i: (0, cur(i), 0)) for _ in o_list]
        args += list(lse_list)
        in_specs += [row_cur(HEAD_DIM) for _ in lse_list]
    else:
        in_specs = [row_cur(GROUP_WIDTH)]
    small = [attn_gain.reshape(1, -1), w_out_bf16, n_post.reshape(1, -1), n_pre.reshape(1, -1)]
    args += [o_hg, x] + small
    in_specs += [row_cur(GROUP_WIDTH), row_prev(d)]
    in_specs += [full(a) for a in small]
    return pl.pallas_call(
        functools.partial(_merge_kernel, n_pat),
        grid=(n + 1,),
        in_specs=in_specs,
        out_specs=[row_prev(d), row_prev(d)],
        out_shape=[jax.ShapeDtypeStruct((t, d), F32), jax.ShapeDtypeStruct((t, d), BF16)],
        scratch_shapes=[pltpu.VMEM((tm, 2 * GROUP_WIDTH), BF16) for _ in range(2)],
        compiler_params=_params("arbitrary"),
        name="merge_outproj",
    )(*args)


def _ffn_kernel(h_ref, x1_ref, wg_ref, wu_ref, wd_ref, npost_ref, y_ref):
    j = pl.program_id(1)
    tf = wg_ref.shape[1]
    d = wd_ref.shape[1]
    acc_scr = y_ref

    @pl.when(j == 0)
    def _():
        acc_scr[...] = jnp.zeros_like(acc_scr)

    h = h_ref[...]
    acts = []
    for c in range(0, tf, FFN_CHUNK):
        gate = jnp.dot(h, wg_ref[:, c:c + FFN_CHUNK], preferred_element_type=F32)
        up = jnp.dot(h, wu_ref[:, c:c + FFN_CHUNK], preferred_element_type=F32)
        acts.append((gate * jax.nn.sigmoid(gate) * up).astype(BF16))
    act = jnp.concatenate(acts, axis=-1)
    for c in range(0, d, FFN_CHUNK):
        cols = slice(c, c + FFN_CHUNK)
        acc_scr[:, cols] += jnp.dot(act, wd_ref[:, cols], preferred_element_type=F32)

    @pl.when(j == pl.num_programs(1) - 1)
    def _():
        y_ref[...] = x1_ref[...] + _rms(acc_scr[...], npost_ref[...])


def _ffn(h2, x1, wg, wu, wd, n_post, tm, tf):
    t, d = x1.shape
    dff = wg.shape[1]
    assert dff % tf == 0
    return pl.pallas_call(
        _ffn_kernel,
        grid=(t // tm, dff // tf),
        in_specs=[
            pl.BlockSpec((tm, d), lambda i, j: (i, 0)),
            pl.BlockSpec((tm, d), lambda i, j: (i, 0)),
            pl.BlockSpec((d, tf), lambda i, j: (0, j)),
            pl.BlockSpec((d, tf), lambda i, j: (0, j)),
            pl.BlockSpec((tf, d), lambda i, j: (j, 0)),
            pl.BlockSpec((1, d), lambda i, j: (0, 0)),
        ],
        out_specs=pl.BlockSpec((tm, d), lambda i, j: (i, 0)),
        out_shape=jax.ShapeDtypeStruct((t, d), F32),
        compiler_params=_params("parallel", "arbitrary"),
        name="ffn",
    )(h2, x1, wg, wu, wd, n_post.reshape(1, d))


def _row_tile(t, want):
    tm = min(t, want)
    assert t % tm == 0
    return tm


INPROJ_ROWS, INPROJ_COLS = 1024, 512
HGRN_ROWS = 256
MERGE_ROWS = 512
FFN_ROWS, FFN_COLS = 1024, 256
SAMPLES_PER_STEP = 4
ATTN_BLOCKS_PER_STEP = 4


def kernel(x_prompt, x_sample, cache_win_k, cache_win_v, state_hgrn, norm_pre_mix, w_in,
           hg_lb_logits, attn_out_gain, hg_norm_gain, w_out, norm_post_mix, norm_pre_ffn,
           w_gate, w_up, w_down, norm_post_ffn):
    depth = w_in.shape[0]
    bp, t, d = x_prompt.shape
    bs, ts, _ = x_sample.shape
    assert bp == 1 and ts == 1
    keep = min(max(w for w, _ in PATTERNS), t)
    logits = hg_lb_logits.astype(F32)

    yp = x_prompt.reshape(t, d)
    ys = x_sample.reshape(bs, d)
    outs = [[] for _ in range(6)]
    for l in range(depth):
        w_in_b, w_out_b = w_in[l].astype(BF16), w_out[l].astype(BF16)
        wg_b, wu_b, wd_b = w_gate[l].astype(BF16), w_up[l].astype(BF16), w_down[l].astype(BF16)

        dils = tuple(dil for _, dil in PATTERNS)
        hg_cols, *dec = _inproj(yp, norm_pre_mix[l], w_in_b, _row_tile(t, INPROJ_ROWS), INPROJ_COLS, dils)
        o_list, lse_list = zip(*[_band_attn(qkv, dil) for qkv, dil in zip(dec, dils)])
        o_hg, s_fin = _hgrn_prompt(hg_cols, logits, l, hg_norm_gain[l], _row_tile(t, HGRN_ROWS))
        x1, h2 = _merge(o_list, lse_list, o_hg, yp, attn_out_gain[l],
                        w_out_b, norm_post_mix[l], norm_pre_ffn[l], _row_tile(t, MERGE_ROWS))
        (proj,) = _inproj(yp, norm_pre_mix[l], w_in_b, _row_tile(keep, INPROJ_ROWS), INPROJ_COLS,
                          rows=(t - keep, t), cols=(GROUP_WIDTH, 3 * GROUP_WIDTH))
        yp = _ffn(h2, x1, wg_b, wu_b, wd_b, norm_post_ffn[l], _row_tile(t, FFN_ROWS), FFN_COLS)
        outs[0].append(proj[:, :GROUP_WIDTH].reshape(1, keep, N_HEADS, HEAD_DIM))
        outs[1].append(proj[:, GROUP_WIDTH:].reshape(1, keep, N_HEADS, HEAD_DIM))
        outs[2].append(s_fin.reshape(1, N_HEADS, HEAD_DIM, HEAD_DIM))

        (proj_s,) = _inproj(ys, norm_pre_mix[l], w_in_b, bs, 2 * INPROJ_COLS)
        hg_cols_s = proj_s[:, 3 * GROUP_WIDTH:]
        o_att_s = _sample_attn(proj_s, cache_win_k[l], cache_win_v[l])
        o_hg_s, s_new = _hgrn_step(hg_cols_s, logits, l, hg_norm_gain[l], state_hgrn[l])
        x1s, h2s = _merge([o_att_s], [], o_hg_s, ys, attn_out_gain[l],
                          w_out_b, norm_post_mix[l], norm_pre_ffn[l], bs)
        ys = _ffn(h2s, x1s, wg_b, wu_b, wd_b, norm_post_ffn[l], bs, 2 * FFN_COLS)
        outs[3].append(proj_s[:, GROUP_WIDTH:2 * GROUP_WIDTH].reshape(bs, 1, N_HEADS, HEAD_DIM))
        outs[4].append(proj_s[:, 2 * GROUP_WIDTH:3 * GROUP_WIDTH].reshape(bs, 1, N_HEADS, HEAD_DIM))
        outs[5].append(s_new)

    st = lambda xs: jnp.stack(xs)
    return (yp.reshape(1, t, d), ys.reshape(bs, 1, d), st(outs[0]), st(outs[1]), st(outs[2]),
            st(outs[3]), st(outs[4]), st(outs[5]))
```

```python
import functools

import jax
import jax.numpy as jnp
import numpy as np
from jax import lax
from jax.experimental import pallas as pl
from jax.experimental.pallas import tpu as pltpu

F32 = jnp.float32
BF16 = jnp.bfloat16

EPS = 1e-6
HEAD_DIM = 128
N_HEADS = 8
GROUP_WIDTH = N_HEADS * HEAD_DIM
N_PROJ_GROUPS = 7
PATTERNS = ((128, 1), (512, 4), (2048, 16))
BAND = 128
NEG_BIG = -1e30
LOG2E = 1.4426950408889634
HG_BLOCK = 128
SUBLANES = 8
MXU_COLS = 256
X_PARTS = 4
FFN_CHUNK = MXU_COLS
VMEM_LIMIT = 56 * 1024 * 1024


def _params(*sem):
    return pltpu.CompilerParams(dimension_semantics=sem, vmem_limit_bytes=VMEM_LIMIT)


def _rms(x, gain):
    ms = jnp.mean(x * x, axis=-1, keepdims=True)
    return x * lax.rsqrt(ms + EPS) * gain


def _dot_nt(a, b):
    return lax.dot_general(a, b, (((1,), (1,)), ((), ())), preferred_element_type=F32)


def _dot_tn(a, b):
    return lax.dot_general(a, b, (((0,), (0,)), ((), ())), preferred_element_type=F32)


def _inproj_kernel(dils, n_parts, *refs):
    x_parts, (g_ref, w_ref, o_ref), rest = refs[:n_parts], refs[n_parts:n_parts + 3], refs[n_parts + 3:]
    dec_refs, h_scr = rest[:len(dils)], rest[len(dils)]
    j = pl.program_id(1)
    tm, part = x_parts[0].shape

    @pl.when(j == 0)
    def _():
        sq = functools.reduce(lambda a, b: a + b,
                              [jnp.sum(p[...] * p[...], axis=-1, keepdims=True) for p in x_parts])
        inv = lax.rsqrt(sq * (1.0 / (n_parts * part)) + EPS)
        for q, p in enumerate(x_parts):
            cols = slice(q * part, (q + 1) * part)
            h_scr[:, cols] = (p[...] * inv * g_ref[:, cols]).astype(BF16)

    if not dils:
        o_ref[...] = jnp.dot(h_scr[...], w_ref[...], preferred_element_type=F32)
        return

    head_scr = rest[len(dils) + 1]
    plane_scrs = dict(zip(dils[1:-1], rest[len(dils) + 2:]))
    n_att = 3 * GROUP_WIDTH // w_ref.shape[1]

    @pl.when(j >= n_att)
    def _():
        o_ref[...] = jnp.dot(h_scr[...], w_ref[...], preferred_element_type=F32)

    @pl.when(j < n_att)
    def _():
        h = h_scr[...]
        for c in range(0, w_ref.shape[1], MXU_COLS):
            res = jnp.dot(h, w_ref[:, c:c + MXU_COLS], preferred_element_type=F32)
            for hh in range(c // HEAD_DIM, (c + MXU_COLS) // HEAD_DIM):
                sl = slice(hh * HEAD_DIM, (hh + 1) * HEAD_DIM)
                head_res = res[:, hh * HEAD_DIM - c:(hh + 1) * HEAD_DIM - c]
                head_scr[hh] = head_res
                dec_refs[0][0, :, sl] = head_res.astype(BF16)
                prev, planes = 1, {0: head_scr.at[hh]}
                for dil, dec_ref in zip(dils[1:], dec_refs[1:]):
                    ratio, new_planes = dil // prev, {}
                    for rp, src in planes.items():
                        for q in range(ratio):
                            r = rp + prev * q
                            val = src[pl.ds(q, tm // dil, stride=ratio), :]
                            dec_ref[r, :, sl] = val.astype(BF16)
                            if dil in plane_scrs:
                                plane_scrs[dil][hh * dil + r] = val
                                new_planes[r] = plane_scrs[dil].at[hh * dil + r]
                    prev, planes = dil, new_planes


def _inproj(x, gain, w_bf16, tm, tn, dils=(), rows=None, cols=None):
    d = x.shape[1]
    r0, r1 = rows or (0, x.shape[0])
    c0, c1 = cols or (0, w_bf16.shape[1])
    t, n = r1 - r0, c1 - c0
    assert r0 % tm == 0 and t % tm == 0 and c0 % tn == 0 and n % tn == 0
    i0, j0 = r0 // tm, c0 // tn
    if dils:
        assert (r0, c0) == (0, 0)
        n_att = 3 * GROUP_WIDTH // tn
        assert n == N_PROJ_GROUPS * GROUP_WIDTH and tn % MXU_COLS == 0 and GROUP_WIDTH % tn == 0
        assert dils[0] == 1 and all(b % a == 0 for a, b in zip(dils, dils[1:]))
        out_specs = [pl.BlockSpec((tm, tn), lambda i, j: (i, jnp.maximum(j - n_att, 0)))]
        out_shape = [jax.ShapeDtypeStruct((t, n - 3 * GROUP_WIDTH), F32)]
        for dil in dils:
            assert tm % (16 * dil) == 0
            out_specs.append(pl.BlockSpec((dil, tm // dil, tn),
                                          lambda i, j: (0, i, jnp.minimum(j, n_att - 1))))
            out_shape.append(jax.ShapeDtypeStruct((dil, t // dil, 3 * GROUP_WIDTH), BF16))
    else:
        out_specs = [pl.BlockSpec((tm, tn), lambda i, j: (i, j))]
        out_shape = [jax.ShapeDtypeStruct((t, n), F32)]
    n_i, n_j = t // tm, n // tn
    part = d // X_PARTS

    def x_spec(q):
        switch = max(n_j - X_PARTS + q, 1)
        return pl.BlockSpec(
            (tm, part),
            lambda i, j: (jnp.minimum(i + jnp.where(j >= switch, 1, 0), n_i - 1) + i0, q))

    return pl.pallas_call(
        functools.partial(_inproj_kernel, tuple(dils), X_PARTS),
        grid=(n_i, n_j),
        in_specs=[x_spec(q) for q in range(X_PARTS)] + [
            pl.BlockSpec((1, d), lambda i, j: (0, 0)),
            pl.BlockSpec((d, tn), lambda i, j: (0, j + j0)),
        ],
        out_specs=out_specs,
        out_shape=out_shape,
        scratch_shapes=[pltpu.VMEM((tm, d), BF16)]
        + ([pltpu.VMEM((tn // HEAD_DIM, tm, HEAD_DIM), F32)] if dils else [])
        + [pltpu.VMEM((tn // HEAD_DIM * dil, tm // dil, HEAD_DIM), F32) for dil in dils[1:-1]],
        compiler_params=_params("arbitrary", "arbitrary"),
        name="inproj",
    )(*([x] * X_PARTS), gain.reshape(1, d), w_bf16)


def _band_attn_kernel(dil, nres, nsub, q_ref, k_ref, v_ref, o_ref, lse_ref,
                      o_scr, lse_scr, kprev_scr, vprev_scr):
    i = pl.program_id(0)
    rg = pl.program_id(1)
    bq = BAND
    row = lax.broadcasted_iota(jnp.int32, (bq, 2 * bq), 0)
    col = lax.broadcasted_iota(jnp.int32, (bq, 2 * bq), 1)
    dist = row + bq - col
    in_band = (dist >= 0) & (dist <= BAND)
    first_valid = in_band & (col >= jnp.where(i == 0, bq, 0))
    lane = lax.broadcasted_iota(jnp.int32, (bq, HEAD_DIM), 1)
    scale = HEAD_DIM ** -0.5
    for rr in range(nres):
        r = rg * nres + rr

        @pl.when(i == 0)
        def _():
            kprev_scr[r] = jnp.zeros(kprev_scr.shape[1:], kprev_scr.dtype)
            vprev_scr[r] = jnp.zeros(vprev_scr.shape[1:], vprev_scr.dtype)

        for sub in range(nsub):
            rows = slice(sub * bq, (sub + 1) * bq)
            before = slice((sub - 1) * bq, sub * bq)
            valid = first_valid if sub == 0 else in_band
            lse_tile = jnp.zeros((bq, HEAD_DIM), F32)
            for h in range(N_HEADS):
                sl = slice(h * HEAD_DIM, (h + 1) * HEAD_DIM)
                q = q_ref[rr, rows, sl]
                k_prev = kprev_scr[r, :, sl] if sub == 0 else k_ref[rr, before, sl]
                v_prev = vprev_scr[r, :, sl] if sub == 0 else v_ref[rr, before, sl]
                k = jnp.concatenate([k_prev, k_ref[rr, rows, sl]], axis=0)
                v = jnp.concatenate([v_prev, v_ref[rr, rows, sl]], axis=0)
                s = jnp.where(valid, _dot_nt(q, k), NEG_BIG)
                m = jnp.max(s, axis=-1, keepdims=True)
                p = jnp.exp2((s - m) * (scale * LOG2E))
                l = jnp.sum(p, axis=-1, keepdims=True)
                o = jnp.dot(p.astype(BF16), v, preferred_element_type=F32)
                o_scr[rr * nsub + sub, h] = o * (1.0 / l)
                lse_tile = jnp.where(lane == h, m * scale + jnp.log(l), lse_tile)
            lse_scr[rr * nsub + sub] = lse_tile
        last = slice((nsub - 1) * bq, nsub * bq)
        kprev_scr[r] = k_ref[rr, last, :]
        vprev_scr[r] = v_ref[rr, last, :]

    def write_out(first_residue):
        for rr in range(nres):
            for sub in range(nsub):
                start = dil * sub * bq + first_residue + rr
                dst = pl.ds(start, bq, stride=dil) if dil > 1 else pl.ds(start, bq)
                for h in range(N_HEADS):
                    o_ref[h, dst, :] = o_scr[rr * nsub + sub, h]
                lse_ref[dst, :] = lse_scr[rr * nsub + sub]

    if dil == nres:
        write_out(0)
    else:
        for g in range(dil // nres):
            pl.when(rg == g)(functools.partial(write_out, g * nres))


def _band_attn(qkv, dil):
    rows = qkv.shape[1]
    t = rows * dil
    bq = BAND
    nres = min(dil, ATTN_BLOCKS_PER_STEP)
    nsub = min(ATTN_BLOCKS_PER_STEP // nres, rows // bq)
    assert qkv.shape[0] == dil and rows % (nsub * bq) == 0 and dil % nres == 0
    blk = (nres, nsub * bq, GROUP_WIDTH)
    cur = lambda g: (lambda i, rg: (rg, i, g))
    carry = pltpu.VMEM((dil, bq, GROUP_WIDTH), qkv.dtype)
    return pl.pallas_call(
        functools.partial(_band_attn_kernel, dil, nres, nsub),
        grid=(rows // (nsub * bq), dil // nres),
        in_specs=[pl.BlockSpec(blk, cur(g)) for g in range(3)],
        out_specs=[
            pl.BlockSpec((N_HEADS, dil * nsub * bq, HEAD_DIM), lambda i, rg: (0, i, 0)),
            pl.BlockSpec((dil * nsub * bq, HEAD_DIM), lambda i, rg: (i, 0)),
        ],
        out_shape=[
            jax.ShapeDtypeStruct((N_HEADS, t, HEAD_DIM), F32),
            jax.ShapeDtypeStruct((t, HEAD_DIM), F32),
        ],
        scratch_shapes=[pltpu.VMEM((nres * nsub, N_HEADS, bq, HEAD_DIM), F32),
                        pltpu.VMEM((nres * nsub, bq, HEAD_DIM), F32), carry, carry],
        compiler_params=_params("arbitrary", "arbitrary"),
        name=f"band_attn_d{dil}",
    )(qkv, qkv, qkv)


def _sample_attn_kernel(q_ref, kn_ref, vn_ref, *rest):
    n_pat = len(PATTERNS)
    cache_refs = rest[: 2 * n_pat]
    o_ref = rest[2 * n_pat]
    scale = HEAD_DIM ** -0.5
    for b in range(q_ref.shape[0]):
        q = q_ref[b]
        kn = kn_ref[b]
        vn = vn_ref[b]
        s_new = jnp.sum(q * kn, axis=-1, keepdims=True) * scale
        outs, lses = [], []
        for p in range(n_pat):
            kc = cache_refs[2 * p][b]
            vc = cache_refs[2 * p + 1][b]
            s = jnp.sum(kc * q[None], axis=-1, keepdims=True) * scale
            m = jnp.maximum(jnp.max(s, axis=0), s_new)
            e = jnp.exp(s - m[None])
            e_new = jnp.exp(s_new - m)
            l = jnp.sum(e, axis=0) + e_new
            o = jnp.sum(e * vc, axis=0) + e_new * vn
            outs.append(o * (1.0 / l))
            lses.append(m + jnp.log(l))
        top = functools.reduce(jnp.maximum, lses)
        ws = [jnp.exp(x - top) for x in lses]
        den = functools.reduce(lambda a, b: a + b, ws)
        o_ref[b] = functools.reduce(lambda a, b: a + b, [w * o for w, o in zip(ws, outs)]) * (1.0 / den)


def _sample_attn(proj_s, cache_k, cache_v):
    b = proj_s.shape[0]
    past = cache_k.shape[1]
    assert past == max(w for w, _ in PATTERNS), "every dilated key must lie inside the cached window"
    nb = _row_tile(b, SAMPLES_PER_STEP)
    head_blk = (nb, N_HEADS, HEAD_DIM)
    in_specs = [pl.BlockSpec(head_blk, lambda i: (i, 0, 0)) for _ in range(3)]
    args = [proj_s[:, g * GROUP_WIDTH:(g + 1) * GROUP_WIDTH].reshape(b, N_HEADS, HEAD_DIM)
            for g in range(3)]
    for _, dil in PATTERNS:
        rows = past // dil
        last_blk = rows // BAND - 1
        for c in (cache_k, cache_v):
            args.append(c.reshape(b, rows, dil, N_HEADS, HEAD_DIM))
            in_specs.append(pl.BlockSpec((nb, BAND, None, N_HEADS, HEAD_DIM),
                                         lambda i, lb=last_blk: (i, lb, 0, 0, 0)))
    out = pl.pallas_call(
        _sample_attn_kernel,
        grid=(b // nb,),
        in_specs=in_specs,
        out_specs=pl.BlockSpec(head_blk, lambda i: (i, 0, 0)),
        out_shape=jax.ShapeDtypeStruct((b, N_HEADS, HEAD_DIM), F32),
        compiler_params=_params("parallel"),
        name="sample_attn",
    )(*args)
    return out.reshape(b, GROUP_WIDTH)


def _lower_bound(logits, layer):
    top = jnp.max(logits, axis=0, keepdims=True)
    e = jnp.exp(logits - top)
    return jnp.sum(e[: layer + 1], axis=0, keepdims=True) / jnp.sum(e, axis=0, keepdims=True)


def _split3(x):
    a = x.astype(BF16)
    r = x - a.astype(F32)
    b = r.astype(BF16)
    c = (r - b.astype(F32)).astype(BF16)
    return a, b, c


def _pair_weights(c, q, k, level_of_pair):
    n = len(c)
    sub = lax.broadcasted_iota(jnp.int32, c[0].shape, 0)
    n_levels = (n * SUBLANES).bit_length() - 1
    q_all = jnp.concatenate(q, axis=0).astype(BF16)
    k_all = jnp.concatenate(k, axis=0).astype(BF16)
    a = jnp.where(level_of_pair == n_levels, _dot_nt(q_all, k_all), 0.0)
    for lev in range(n_levels):
        half = 1 << lev
        xs = []
        for i in range(n):
            if half >= SUBLANES:
                nv = 2 * half // SUBLANES
                first = (i // nv) * nv
                upper = (i % nv) >= nv // 2
                ref = c[first + nv // 2 - 1][SUBLANES - 1:SUBLANES]
                arg = c[i] - ref if upper else ref - c[i]
                base = q[i] if upper else k[i]
            else:
                upper = (sub & half) != 0
                if half == SUBLANES // 2:
                    ref = c[i][half - 1:half]
                elif half == 1:
                    ref = jnp.where(upper, pltpu.roll(c[i], 1, 0), c[i])
                else:
                    odd = (sub & 1) != 0
                    z = jnp.where(odd, c[i], pltpu.roll(c[i], SUBLANES - 1, 0))
                    ref = jnp.where(upper, pltpu.roll(z, 2, 0), z)
                diff = c[i] - ref
                arg = jnp.where(upper, diff, -diff)
                base = jnp.where(upper, q[i], k[i])
            xs.append(base * jnp.exp2(arg))
        x = jnp.concatenate(xs, axis=0).astype(BF16)
        a = jnp.where(level_of_pair == lev, _dot_nt(x, x), a)
    return a


def _gated_norm(o, gain, gate):
    return (_rms(o, gain) * (gate * jax.nn.sigmoid(gate))).astype(BF16)


def _hgrn_kernel(layer, q_ref, f_ref, v_ref, gate_ref, logit_ref, gain_ref, tri_ref, lv_ref,
                 o_ref, state_ref, st_scr, qq_scr, kk_scr, c2_scr, qt_scr, kd_scr):
    step = pl.program_id(0)
    rows = q_ref.shape[0]

    @pl.when(step == 0)
    def _():
        st_scr[...] = jnp.zeros_like(st_scr)

    lb = _lower_bound(logit_ref[...], layer)
    f = lb + (1.0 - lb) * jax.nn.sigmoid(f_ref[...])
    g = jnp.log(f)
    qq = jax.nn.silu(q_ref[...])
    kk = 1.0 - f
    tri = tri_ref[...]
    sums = functools.reduce(
        lambda a, b: a + b,
        [jnp.dot(tri, piece, preferred_element_type=F32) for piece in _split3(g)])
    cum = sums[:rows]
    rev = sums[rows:]
    c2 = cum * LOG2E
    c2_scr[...] = c2
    qq_scr[...] = qq
    kk_scr[...] = kk
    qt_scr[...] = qq * jnp.exp2(c2)
    kd_scr[...] = kk * jnp.exp(rev)

    n_groups = HG_BLOCK // SUBLANES

    def block(b, carry):
        base = pl.multiple_of(b * HG_BLOCK, HG_BLOCK)
        level_of_pair = lv_ref[...]
        for h in range(N_HEADS):
            sl = slice(h * HEAD_DIM, (h + 1) * HEAD_DIM)
            groups = lambda ref: [ref[pl.ds(base + SUBLANES * i, SUBLANES), sl] for i in range(n_groups)]
            c_b = groups(c2_scr)
            st = st_scr[h]
            inter = _dot_nt(qt_scr[pl.ds(base, HG_BLOCK), sl].astype(BF16), st.astype(BF16))
            pair = _pair_weights(c_b, groups(qq_scr), groups(kk_scr), level_of_pair)
            v_b = v_ref[pl.ds(base, HG_BLOCK), sl].astype(BF16)
            o = inter + jnp.dot(pair.astype(BF16), v_b, preferred_element_type=F32)
            o_ref[pl.ds(base, HG_BLOCK), sl] = _gated_norm(
                o, gain_ref[...], gate_ref[pl.ds(base, HG_BLOCK), sl])
            decay = jnp.exp2(c_b[-1][SUBLANES - 1:SUBLANES])
            delta = _dot_tn(v_b, kd_scr[pl.ds(base, HG_BLOCK), sl].astype(BF16))
            st_scr[h] = st * decay + delta
        return carry

    for b in range(rows // HG_BLOCK):
        block(b, 0)

    @pl.when(step == pl.num_programs(0) - 1)
    def _():
        for h in range(N_HEADS):
            state_ref[h] = st_scr[h].T


def _block_tri(rows):
    idx = np.arange(rows)
    same = (idx[:, None] // HG_BLOCK) == (idx[None, :] // HG_BLOCK)
    lower = same & (idx[None, :] <= idx[:, None])
    upper = same & (idx[None, :] > idx[:, None])
    return jnp.asarray(np.concatenate([lower, upper], axis=0), dtype=BF16)


def _pair_levels():
    idx = np.arange(HG_BLOCK)
    t, s = idx[:, None], idx[None, :]
    n_levels = HG_BLOCK.bit_length() - 1
    level = np.floor(np.log2(np.maximum(t ^ s, 1))).astype(np.int32)
    return jnp.asarray(np.where(t == s, n_levels, np.where(s < t, level, -1)).astype(np.int32))


def _hgrn_prompt(proj, logits, layer, hg_gain, rows):
    t = proj.shape[0]
    assert t % rows == 0 and rows % HG_BLOCK == 0
    col = lambda g: pl.BlockSpec((rows, GROUP_WIDTH), lambda i, g=g: (i, g))
    n_l = logits.shape[0]
    return pl.pallas_call(
        functools.partial(_hgrn_kernel, layer),
        grid=(t // rows,),
        in_specs=[
            col(0), col(1), col(2), col(3),
            pl.BlockSpec((n_l, GROUP_WIDTH), lambda i: (0, 0)),
            pl.BlockSpec((1, HEAD_DIM), lambda i: (0, 0)),
            pl.BlockSpec((2 * rows, rows), lambda i: (0, 0)),
            pl.BlockSpec((HG_BLOCK, HG_BLOCK), lambda i: (0, 0)),
        ],
        out_specs=[
            pl.BlockSpec((rows, GROUP_WIDTH), lambda i: (i, 0)),
            pl.BlockSpec((N_HEADS, HEAD_DIM, HEAD_DIM), lambda i: (0, 0, 0)),
        ],
        out_shape=[
            jax.ShapeDtypeStruct((t, GROUP_WIDTH), BF16),
            jax.ShapeDtypeStruct((N_HEADS, HEAD_DIM, HEAD_DIM), F32),
        ],
        scratch_shapes=[pltpu.VMEM((N_HEADS, HEAD_DIM, HEAD_DIM), F32)]
        + [pltpu.VMEM((rows, GROUP_WIDTH), F32) for _ in range(5)],
        compiler_params=_params("arbitrary"),
        name="hgrn_prompt",
    )(proj, proj, proj, proj, logits, hg_gain.reshape(1, HEAD_DIM), _block_tri(rows), _pair_levels())


def _hgrn_step_kernel(layer, q_ref, f_ref, v_ref, gate_ref, logit_ref, gain_ref, s_ref, o_ref, snew_ref,
                      o_scr, gate_scr):
    lb = _lower_bound(logit_ref[...], layer)
    nb = q_ref.shape[0]
    eye = (lax.broadcasted_iota(jnp.int32, (HEAD_DIM, HEAD_DIM), 0)
           == lax.broadcasted_iota(jnp.int32, (HEAD_DIM, HEAD_DIM), 1))

    def column(row):
        return jnp.sum(jnp.where(eye, jnp.broadcast_to(row, (HEAD_DIM, HEAD_DIM)), 0.0),
                       axis=-1, keepdims=True)

    for b in range(nb):
        f = lb + (1.0 - lb) * jax.nn.sigmoid(f_ref[b])
        qq = jax.nn.silu(q_ref[b])
        vv = v_ref[b]
        gate_scr[b:b + 1, :] = gate_ref[b]
        for h in range(N_HEADS):
            sl = slice(h * HEAD_DIM, (h + 1) * HEAD_DIM)
            f_col = column(f[:, sl])
            q_col = column(qq[:, sl])
            s_new = f_col * s_ref[b, h] + (1.0 - f_col) * vv[:, sl]
            snew_ref[b, h] = s_new
            o_scr[b:b + 1, sl] = jnp.sum(q_col * s_new, axis=0, keepdims=True)
    for h in range(N_HEADS):
        sl = slice(h * HEAD_DIM, (h + 1) * HEAD_DIM)
        res = _gated_norm(o_scr[:, sl], gain_ref[...], gate_scr[:, sl]).astype(F32)
        for b in range(nb):
            o_ref[b, :, sl] = res[b:b + 1]


def _hgrn_step(proj_s, logits, layer, hg_gain, state):
    b = proj_s.shape[0]
    nb = _row_tile(b, SAMPLES_PER_STEP)
    p3 = proj_s.reshape(b, 1, proj_s.shape[1])
    row = (nb, 1, GROUP_WIDTH)
    n_l = logits.shape[0]
    st_blk = (nb, N_HEADS, HEAD_DIM, HEAD_DIM)
    o, s_new = pl.pallas_call(
        functools.partial(_hgrn_step_kernel, layer),
        grid=(b // nb,),
        in_specs=[pl.BlockSpec(row, lambda i, g=g: (i, 0, g)) for g in (0, 1, 2, 3)]
        + [pl.BlockSpec((n_l, GROUP_WIDTH), lambda i: (0, 0)),
           pl.BlockSpec((1, HEAD_DIM), lambda i: (0, 0)),
           pl.BlockSpec(st_blk, lambda i: (i, 0, 0, 0))],
        out_specs=[pl.BlockSpec(row, lambda i: (i, 0, 0)),
                   pl.BlockSpec(st_blk, lambda i: (i, 0, 0, 0))],
        out_shape=[jax.ShapeDtypeStruct((b, 1, GROUP_WIDTH), F32),
                   jax.ShapeDtypeStruct(state.shape, F32)],
        scratch_shapes=[pltpu.VMEM((nb, GROUP_WIDTH), F32), pltpu.VMEM((nb, GROUP_WIDTH), F32)],
        compiler_params=_params("parallel"),
        name="hgrn_step",
    )(p3, p3, p3, p3, logits, hg_gain.reshape(1, HEAD_DIM), state)
    return o.reshape(b, GROUP_WIDTH), s_new


def _merge_kernel(n_pat, *refs):
    o_refs = refs[:n_pat]
    lse_refs = refs[n_pat:2 * n_pat] if n_pat > 1 else ()
    k = 2 * n_pat if n_pat > 1 else n_pat
    (hg_ref, x_ref, again_ref, wout_ref, npost_ref, npre_ref,
     x1_ref, h2_ref, buf_a, buf_b) = refs[k:]
    tm = x_ref.shape[0]
    i = pl.program_id(0)

    def merge_stage(dst):
        if n_pat > 1:
            lses = [r[...] for r in lse_refs]
            top = functools.reduce(jnp.maximum, lses)
            ws = [jnp.exp(x - top) for x in lses]
            inv = 1.0 / functools.reduce(lambda a, b: a + b, ws)
            ws = [w * inv for w in ws]
        att = []
        sq = jnp.zeros((tm, 1), F32)
        for h in range(N_HEADS):
            sl = slice(h * HEAD_DIM, (h + 1) * HEAD_DIM)
            if n_pat > 1:
                a = functools.reduce(
                    lambda x, y: x + y,
                    [w[:, h:h + 1] * r[h] for w, r in zip(ws, o_refs)])
            else:
                a = o_refs[0][:, sl]
            att.append(a)
            sq = sq + jnp.sum(a * a, axis=-1, keepdims=True)
        dst[:, GROUP_WIDTH:] = hg_ref[...].astype(BF16)
        inv_rms = lax.rsqrt(sq * (1.0 / GROUP_WIDTH) + EPS)
        for h, a in enumerate(att):
            sl = slice(h * HEAD_DIM, (h + 1) * HEAD_DIM)
            dst[:, sl] = (a * inv_rms * again_ref[:, sl]).astype(BF16)

    def project_stage(src):
        mix = jnp.dot(src[...], wout_ref[...], preferred_element_type=F32)
        x1 = x_ref[...] + _rms(mix, npost_ref[...])
        x1_ref[...] = x1
        h2_ref[...] = _rms(x1, npre_ref[...]).astype(BF16)

    @pl.when(i == 0)
    def _():
        buf_b[...] = jnp.zeros_like(buf_b)

    @pl.when(i % 2 == 0)
    def _():
        project_stage(buf_b)
        merge_stage(buf_a)

    @pl.when(i % 2 == 1)
    def _():
        project_stage(buf_a)
        merge_stage(buf_b)


def _merge(o_list, lse_list, o_hg, x, attn_gain, w_out_bf16, n_post, n_pre, tm):
    t, d = x.shape
    n_pat = len(o_list)
    n = t // tm
    cur = lambda i: jnp.minimum(i, n - 1)
    prev = lambda i: jnp.maximum(i - 1, 0)
    row_cur = lambda w: pl.BlockSpec((tm, w), lambda i: (cur(i), 0))
    row_prev = lambda w: pl.BlockSpec((tm, w), lambda i: (prev(i), 0))
    full = lambda a: pl.BlockSpec(a.shape, lambda i: (0,) * a.ndim, pipeline_mode=pl.Buffered(1))
    args = list(o_list)
    if n_pat > 1:
        in_specs = [pl.BlockSpec((N_HEADS, tm, HEAD_DIM), lambda i: (0, cur(i), 0)) for _ in o_list]
        args += list(lse_list)
        in_specs += [row_cur(HEAD_DIM) for _ in lse_list]
    else:
        in_specs = [row_cur(GROUP_WIDTH)]
    small = [attn_gain.reshape(1, -1), w_out_bf16, n_post.reshape(1, -1), n_pre.reshape(1, -1)]
    args += [o_hg, x] + small
    in_specs += [row_cur(GROUP_WIDTH), row_prev(d)]
    in_specs += [full(a) for a in small]
    return pl.pallas_call(
        functools.partial(_merge_kernel, n_pat),
        grid=(n + 1,),
        in_specs=in_specs,
        out_specs=[row_prev(d), row_prev(d)],
        out_shape=[jax.ShapeDtypeStruct((t, d), F32), jax.ShapeDtypeStruct((t, d), BF16)],
        scratch_shapes=[pltpu.VMEM((tm, 2 * GROUP_WIDTH), BF16) for _ in range(2)],
        compiler_params=_params("arbitrary"),
        name="merge_outproj",
    )(*args)


def _ffn_kernel(h_ref, x1_ref, wg_ref, wu_ref, wd_ref, npost_ref, y_ref):
    j = pl.program_id(1)
    tf = wg_ref.shape[1]
    d = wd_ref.shape[1]
    acc_scr = y_ref

    @pl.when(j == 0)
    def _():
        acc_scr[...] = jnp.zeros_like(acc_scr)

    h = h_ref[...]
    acts = []
    for c in range(0, tf, FFN_CHUNK):
        gate = jnp.dot(h, wg_ref[:, c:c + FFN_CHUNK], preferred_element_type=F32)
        up = jnp.dot(h, wu_ref[:, c:c + FFN_CHUNK], preferred_element_type=F32)
        acts.append((gate * jax.nn.sigmoid(gate) * up).astype(BF16))
    act = jnp.concatenate(acts, axis=-1)
    for c in range(0, d, FFN_CHUNK):
        cols = slice(c, c + FFN_CHUNK)
        acc_scr[:, cols] += jnp.dot(act, wd_ref[:, cols], preferred_element_type=F32)

    @pl.when(j == pl.num_programs(1) - 1)
    def _():
        y_ref[...] = x1_ref[...] + _rms(acc_scr[...], npost_ref[...])


def _ffn(h2, x1, wg, wu, wd, n_post, tm, tf):
    t, d = x1.shape
    dff = wg.shape[1]
    assert dff % tf == 0 and dff // tf > 1
    return pl.pallas_call(
        _ffn_kernel,
        grid=(t // tm, dff // tf),
        in_specs=[
            pl.BlockSpec((tm, d), lambda i, j: (i, 0)),
            pl.BlockSpec((tm, d), lambda i, j: (jnp.where(j >= 1, i, jnp.maximum(i - 1, 0)), 0)),
            pl.BlockSpec((d, tf), lambda i, j: (0, j)),
            pl.BlockSpec((d, tf), lambda i, j: (0, j)),
            pl.BlockSpec((tf, d), lambda i, j: (j, 0)),
            pl.BlockSpec((1, d), lambda i, j: (0, 0)),
        ],
        out_specs=pl.BlockSpec((tm, d), lambda i, j: (i, 0)),
        out_shape=jax.ShapeDtypeStruct((t, d), F32),
        compiler_params=_params("parallel", "arbitrary"),
        name="ffn",
    )(h2, x1, wg, wu, wd, n_post.reshape(1, d))


def _row_tile(t, want):
    tm = min(t, want)
    assert t % tm == 0
    return tm


INPROJ_ROWS, INPROJ_COLS = 1024, 512
HGRN_ROWS = 256
MERGE_ROWS = 512
FFN_ROWS, FFN_COLS = 1024, 256
SAMPLES_PER_STEP = 4
ATTN_BLOCKS_PER_STEP = 8


def kernel(x_prompt, x_sample, cache_win_k, cache_win_v, state_hgrn, norm_pre_mix, w_in,
           hg_lb_logits, attn_out_gain, hg_norm_gain, w_out, norm_post_mix, norm_pre_ffn,
           w_gate, w_up, w_down, norm_post_ffn):
    depth = w_in.shape[0]
    bp, t, d = x_prompt.shape
    bs, ts, _ = x_sample.shape
    assert bp == 1 and ts == 1
    keep = min(max(w for w, _ in PATTERNS), t)
    logits = hg_lb_logits.astype(F32)

    yp = x_prompt.reshape(t, d)
    ys = x_sample.reshape(bs, d)
    outs = [[] for _ in range(6)]
    for l in range(depth):
        w_in_b, w_out_b = w_in[l].astype(BF16), w_out[l].astype(BF16)
        wg_b, wu_b, wd_b = w_gate[l].astype(BF16), w_up[l].astype(BF16), w_down[l].astype(BF16)

        dils = tuple(dil for _, dil in PATTERNS)
        hg_cols, *dec = _inproj(yp, norm_pre_mix[l], w_in_b, _row_tile(t, INPROJ_ROWS), INPROJ_COLS, dils)
        o_list, lse_list = zip(*[_band_attn(qkv, dil) for qkv, dil in zip(dec, dils)])
        o_hg, s_fin = _hgrn_prompt(hg_cols, logits, l, hg_norm_gain[l], _row_tile(t, HGRN_ROWS))
        x1, h2 = _merge(o_list, lse_list, o_hg, yp, attn_out_gain[l],
                        w_out_b, norm_post_mix[l], norm_pre_ffn[l], _row_tile(t, MERGE_ROWS))
        (proj,) = _inproj(yp, norm_pre_mix[l], w_in_b, _row_tile(keep, INPROJ_ROWS), INPROJ_COLS,
                          rows=(t - keep, t), cols=(GROUP_WIDTH, 3 * GROUP_WIDTH))
        yp = _ffn(h2, x1, wg_b, wu_b, wd_b, norm_post_ffn[l], _row_tile(t, FFN_ROWS), FFN_COLS)
        outs[0].append(proj[:, :GROUP_WIDTH].reshape(1, keep, N_HEADS, HEAD_DIM))
        outs[1].append(proj[:, GROUP_WIDTH:].reshape(1, keep, N_HEADS, HEAD_DIM))
        outs[2].append(s_fin.reshape(1, N_HEADS, HEAD_DIM, HEAD_DIM))

        (proj_s,) = _inproj(ys, norm_pre_mix[l], w_in_b, bs, 2 * INPROJ_COLS)
        hg_cols_s = proj_s[:, 3 * GROUP_WIDTH:]
        o_att_s = _sample_attn(proj_s, cache_win_k[l], cache_win_v[l])
        o_hg_s, s_new = _hgrn_step(hg_cols_s, logits, l, hg_norm_gain[l], state_hgrn[l])
        x1s, h2s = _merge([o_att_s], [], o_hg_s, ys, attn_out_gain[l],
                          w_out_b, norm_post_mix[l], norm_pre_ffn[l], bs)
        ys = _ffn(h2s, x1s, wg_b, wu_b, wd_b, norm_post_ffn[l], bs, 2 * FFN_COLS)
        outs[3].append(proj_s[:, GROUP_WIDTH:2 * GROUP_WIDTH].reshape(bs, 1, N_HEADS, HEAD_DIM))
        outs[4].append(proj_s[:, 2 * GROUP_WIDTH:3 * GROUP_WIDTH].reshape(bs, 1, N_HEADS, HEAD_DIM))
        outs[5].append(s_new)

    st = lambda xs: jnp.stack(xs)
    return (yp.reshape(1, t, d), ys.reshape(bs, 1, d), st(outs[0]), st(outs[1]), st(outs[2]),
            st(outs[3]), st(outs[4]), st(outs[5]))
```

```python
import functools

import jax
import jax.numpy as jnp
import numpy as np
from jax import lax
from jax.experimental import pallas as pl
from jax.experimental.pallas import tpu as pltpu

F32 = jnp.float32
BF16 = jnp.bfloat16

EPS = 1e-6
HEAD_DIM = 128
N_HEADS = 8
GROUP_WIDTH = N_HEADS * HEAD_DIM
N_PROJ_GROUPS = 7
PATTERNS = ((128, 1), (512, 4), (2048, 16))
BAND = 128
NEG_BIG = -1e30
LOG2E = 1.4426950408889634
HG_BLOCK = 128
SUBLANES = 8
MXU_COLS = 256
X_PARTS = 4
FFN_CHUNK = MXU_COLS
VMEM_LIMIT = 56 * 1024 * 1024


def _params(*sem):
    return pltpu.CompilerParams(dimension_semantics=sem, vmem_limit_bytes=VMEM_LIMIT)


def _rms(x, gain):
    ms = jnp.mean(x * x, axis=-1, keepdims=True)
    return x * lax.rsqrt(ms + EPS) * gain


def _dot_nt(a, b):
    return lax.dot_general(a, b, (((1,), (1,)), ((), ())), preferred_element_type=F32)


def _dot_tn(a, b):
    return lax.dot_general(a, b, (((0,), (0,)), ((), ())), preferred_element_type=F32)


def _inproj_kernel(dils, n_parts, *refs):
    x_parts, (g_ref, w_ref, o_ref), rest = refs[:n_parts], refs[n_parts:n_parts + 3], refs[n_parts + 3:]
    if dils:
        kv_ref, rest = rest[0], rest[1:]
    dec_refs, h_scr = rest[:len(dils)], rest[len(dils)]
    j = pl.program_id(1)
    tm, part = x_parts[0].shape

    @pl.when(j == 0)
    def _():
        sq = functools.reduce(lambda a, b: a + b,
                              [jnp.sum(p[...] * p[...], axis=-1, keepdims=True) for p in x_parts])
        inv = lax.rsqrt(sq * (1.0 / (n_parts * part)) + EPS)
        for q, p in enumerate(x_parts):
            cols = slice(q * part, (q + 1) * part)
            h_scr[:, cols] = (p[...] * inv * g_ref[:, cols]).astype(BF16)

    if not dils:
        o_ref[...] = jnp.dot(h_scr[...], w_ref[...], preferred_element_type=F32)
        return

    head_scr = rest[len(dils) + 1]
    plane_scrs = dict(zip(dils[1:-1], rest[len(dils) + 2:]))
    n_att = 3 * GROUP_WIDTH // w_ref.shape[1]

    @pl.when(j >= n_att)
    def _():
        o_ref[...] = jnp.dot(h_scr[...], w_ref[...], preferred_element_type=F32)

    @pl.when(j < n_att)
    def _():
        h = h_scr[...]
        for c in range(0, w_ref.shape[1], MXU_COLS):
            res = jnp.dot(h, w_ref[:, c:c + MXU_COLS], preferred_element_type=F32)
            kv_ref[:, c:c + MXU_COLS] = res
            for hh in range(c // HEAD_DIM, (c + MXU_COLS) // HEAD_DIM):
                sl = slice(hh * HEAD_DIM, (hh + 1) * HEAD_DIM)
                head_res = res[:, hh * HEAD_DIM - c:(hh + 1) * HEAD_DIM - c]
                head_scr[hh] = head_res
                dec_refs[0][0, :, sl] = head_res.astype(BF16)
                prev, planes = 1, {0: head_scr.at[hh]}
                for dil, dec_ref in zip(dils[1:], dec_refs[1:]):
                    ratio, new_planes = dil // prev, {}
                    for rp, src in planes.items():
                        for q in range(ratio):
                            r = rp + prev * q
                            val = src[pl.ds(q, tm // dil, stride=ratio), :]
                            dec_ref[r, :, sl] = val.astype(BF16)
                            if dil in plane_scrs:
                                plane_scrs[dil][hh * dil + r] = val
                                new_planes[r] = plane_scrs[dil].at[hh * dil + r]
                    prev, planes = dil, new_planes


def _inproj(x, gain, w_bf16, tm, tn, dils=(), keep=0):
    t, d = x.shape
    n = w_bf16.shape[1]
    assert t % tm == 0 and n % tn == 0
    if dils:
        n_att = 3 * GROUP_WIDTH // tn
        assert n == N_PROJ_GROUPS * GROUP_WIDTH and tn % MXU_COLS == 0 and GROUP_WIDTH % tn == 0
        assert dils[0] == 1 and all(b % a == 0 for a, b in zip(dils, dils[1:]))
        out_specs = [pl.BlockSpec((tm, tn), lambda i, j: (i, jnp.maximum(j - n_att, 0)))]
        out_shape = [jax.ShapeDtypeStruct((t, n - 3 * GROUP_WIDTH), F32)]
        assert keep % tm == 0 and 0 < keep <= t
        first_kept, k0, n_kv = (t - keep) // tm, GROUP_WIDTH // tn, 2 * GROUP_WIDTH // tn
        out_specs.append(pl.BlockSpec(
            (tm, tn),
            lambda i, j: (jnp.maximum(i - first_kept, 0),
                          jnp.where(i < first_kept, 0, jnp.clip(j - k0, 0, n_kv - 1)))))
        out_shape.append(jax.ShapeDtypeStruct((keep, 2 * GROUP_WIDTH), F32))
        for dil in dils:
            assert tm % (16 * dil) == 0
            out_specs.append(pl.BlockSpec((dil, tm // dil, tn),
                                          lambda i, j: (0, i, jnp.minimum(j, n_att - 1))))
            out_shape.append(jax.ShapeDtypeStruct((dil, t // dil, 3 * GROUP_WIDTH), BF16))
    else:
        out_specs = [pl.BlockSpec((tm, tn), lambda i, j: (i, j))]
        out_shape = [jax.ShapeDtypeStruct((t, n), F32)]
    n_i, n_j = t // tm, n // tn
    part = d // X_PARTS

    def x_spec(q):
        switch = max(n_j - X_PARTS + q, 1)
        return pl.BlockSpec(
            (tm, part),
            lambda i, j: (jnp.minimum(i + jnp.where(j >= switch, 1, 0), n_i - 1), q))

    return pl.pallas_call(
        functools.partial(_inproj_kernel, tuple(dils), X_PARTS),
        grid=(n_i, n_j),
        in_specs=[x_spec(q) for q in range(X_PARTS)] + [
            pl.BlockSpec((1, d), lambda i, j: (0, 0)),
            pl.BlockSpec((d, tn), lambda i, j: (0, j)),
        ],
        out_specs=out_specs,
        out_shape=out_shape,
        scratch_shapes=[pltpu.VMEM((tm, d), BF16)]
        + ([pltpu.VMEM((tn // HEAD_DIM, tm, HEAD_DIM), F32)] if dils else [])
        + [pltpu.VMEM((tn // HEAD_DIM * dil, tm // dil, HEAD_DIM), F32) for dil in dils[1:-1]],
        compiler_params=_params("arbitrary", "arbitrary"),
        name="inproj",
    )(*([x] * X_PARTS), gain.reshape(1, d), w_bf16)


def _band_attn_kernel(dil, nres, nsub, q_ref, k_ref, v_ref, o_ref, lse_ref,
                      o_scr, lse_scr, kprev_scr, vprev_scr):
    i = pl.program_id(0)
    rg = pl.program_id(1)
    bq = BAND
    row = lax.broadcasted_iota(jnp.int32, (bq, 2 * bq), 0)
    col = lax.broadcasted_iota(jnp.int32, (bq, 2 * bq), 1)
    dist = row + bq - col
    in_band = (dist >= 0) & (dist <= BAND)
    first_valid = in_band & (col >= jnp.where(i == 0, bq, 0))
    lane = lax.broadcasted_iota(jnp.int32, (bq, HEAD_DIM), 1)
    scale = HEAD_DIM ** -0.5
    for rr in range(nres):
        r = rg * nres + rr

        @pl.when(i == 0)
        def _():
            kprev_scr[r] = jnp.zeros(kprev_scr.shape[1:], kprev_scr.dtype)
            vprev_scr[r] = jnp.zeros(vprev_scr.shape[1:], vprev_scr.dtype)

        for sub in range(nsub):
            rows = slice(sub * bq, (sub + 1) * bq)
            before = slice((sub - 1) * bq, sub * bq)
            valid = first_valid if sub == 0 else in_band
            lse_tile = jnp.zeros((bq, HEAD_DIM), F32)
            for h in range(N_HEADS):
                sl = slice(h * HEAD_DIM, (h + 1) * HEAD_DIM)
                q = q_ref[rr, rows, sl]
                k_prev = kprev_scr[r, :, sl] if sub == 0 else k_ref[rr, before, sl]
                v_prev = vprev_scr[r, :, sl] if sub == 0 else v_ref[rr, before, sl]
                k = jnp.concatenate([k_prev, k_ref[rr, rows, sl]], axis=0)
                v = jnp.concatenate([v_prev, v_ref[rr, rows, sl]], axis=0)
                s = jnp.where(valid, _dot_nt(q, k), NEG_BIG)
                m = jnp.max(s, axis=-1, keepdims=True)
                p = jnp.exp2((s - m) * (scale * LOG2E))
                l = jnp.sum(p, axis=-1, keepdims=True)
                o = jnp.dot(p.astype(BF16), v, preferred_element_type=F32)
                o_scr[rr * nsub + sub, h] = o * (1.0 / l)
                lse_tile = jnp.where(lane == h, m * scale + jnp.log(l), lse_tile)
            lse_scr[rr * nsub + sub] = lse_tile
        last = slice((nsub - 1) * bq, nsub * bq)
        kprev_scr[r] = k_ref[rr, last, :]
        vprev_scr[r] = v_ref[rr, last, :]

    def write_out(first_residue):
        for rr in range(nres):
            for sub in range(nsub):
                start = dil * sub * bq + first_residue + rr
                dst = pl.ds(start, bq, stride=dil) if dil > 1 else pl.ds(start, bq)
                for h in range(N_HEADS):
                    o_ref[h, dst, :] = o_scr[rr * nsub + sub, h]
                lse_ref[dst, :] = lse_scr[rr * nsub + sub]

    if dil == nres:
        write_out(0)
    else:
        for g in range(dil // nres):
            pl.when(rg == g)(functools.partial(write_out, g * nres))


def _band_attn(qkv, dil):
    rows = qkv.shape[1]
    t = rows * dil
    bq = BAND
    nres = min(dil, ATTN_BLOCKS_PER_STEP)
    nsub = min(ATTN_BLOCKS_PER_STEP // nres, rows // bq)
    assert qkv.shape[0] == dil and rows % (nsub * bq) == 0 and dil % nres == 0
    blk = (nres, nsub * bq, GROUP_WIDTH)
    cur = lambda g: (lambda i, rg: (rg, i, g))
    carry = pltpu.VMEM((dil, bq, GROUP_WIDTH), qkv.dtype)
    return pl.pallas_call(
        functools.partial(_band_attn_kernel, dil, nres, nsub),
        grid=(rows // (nsub * bq), dil // nres),
        in_specs=[pl.BlockSpec(blk, cur(g)) for g in range(3)],
        out_specs=[
            pl.BlockSpec((N_HEADS, dil * nsub * bq, HEAD_DIM), lambda i, rg: (0, i, 0)),
            pl.BlockSpec((dil * nsub * bq, HEAD_DIM), lambda i, rg: (i, 0)),
        ],
        out_shape=[
            jax.ShapeDtypeStruct((N_HEADS, t, HEAD_DIM), F32),
            jax.ShapeDtypeStruct((t, HEAD_DIM), F32),
        ],
        scratch_shapes=[pltpu.VMEM((nres * nsub, N_HEADS, bq, HEAD_DIM), F32),
                        pltpu.VMEM((nres * nsub, bq, HEAD_DIM), F32), carry, carry],
        compiler_params=_params("arbitrary", "arbitrary"),
        name=f"band_attn_d{dil}",
    )(qkv, qkv, qkv)


def _sample_attn_kernel(q_ref, kn_ref, vn_ref, *rest):
    n_pat = len(PATTERNS)
    cache_refs = rest[: 2 * n_pat]
    o_ref = rest[2 * n_pat]
    scale = HEAD_DIM ** -0.5
    for b in range(q_ref.shape[0]):
        q = q_ref[b]
        kn = kn_ref[b]
        vn = vn_ref[b]
        s_new = jnp.sum(q * kn, axis=-1, keepdims=True) * scale
        outs, lses = [], []
        for p in range(n_pat):
            kc = cache_refs[2 * p][b]
            vc = cache_refs[2 * p + 1][b]
            s = jnp.sum(kc * q[None], axis=-1, keepdims=True) * scale
            m = jnp.maximum(jnp.max(s, axis=0), s_new)
            e = jnp.exp(s - m[None])
            e_new = jnp.exp(s_new - m)
            l = jnp.sum(e, axis=0) + e_new
            o = jnp.sum(e * vc, axis=0) + e_new * vn
            outs.append(o * (1.0 / l))
            lses.append(m + jnp.log(l))
        top = functools.reduce(jnp.maximum, lses)
        ws = [jnp.exp(x - top) for x in lses]
        den = functools.reduce(lambda a, b: a + b, ws)
        o_ref[b] = functools.reduce(lambda a, b: a + b, [w * o for w, o in zip(ws, outs)]) * (1.0 / den)


def _sample_attn(proj_s, cache_k, cache_v):
    b = proj_s.shape[0]
    past = cache_k.shape[1]
    assert past == max(w for w, _ in PATTERNS), "every dilated key must lie inside the cached window"
    nb = _row_tile(b, SAMPLES_PER_STEP)
    head_blk = (nb, N_HEADS, HEAD_DIM)
    in_specs = [pl.BlockSpec(head_blk, lambda i: (i, 0, 0)) for _ in range(3)]
    args = [proj_s[:, g * GROUP_WIDTH:(g + 1) * GROUP_WIDTH].reshape(b, N_HEADS, HEAD_DIM)
            for g in range(3)]
    for _, dil in PATTERNS:
        rows = past // dil
        last_blk = rows // BAND - 1
        for c in (cache_k, cache_v):
            args.append(c.reshape(b, rows, dil, N_HEADS, HEAD_DIM))
            in_specs.append(pl.BlockSpec((nb, BAND, None, N_HEADS, HEAD_DIM),
                                         lambda i, lb=last_blk: (i, lb, 0, 0, 0)))
    out = pl.pallas_call(
        _sample_attn_kernel,
        grid=(b // nb,),
        in_specs=in_specs,
        out_specs=pl.BlockSpec(head_blk, lambda i: (i, 0, 0)),
        out_shape=jax.ShapeDtypeStruct((b, N_HEADS, HEAD_DIM), F32),
        compiler_params=_params("parallel"),
        name="sample_attn",
    )(*args)
    return out.reshape(b, GROUP_WIDTH)


def _lower_bound(logits, layer):
    top = jnp.max(logits, axis=0, keepdims=True)
    e = jnp.exp(logits - top)
    return jnp.sum(e[: layer + 1], axis=0, keepdims=True) / jnp.sum(e, axis=0, keepdims=True)


def _split3(x):
    a = x.astype(BF16)
    r = x - a.astype(F32)
    b = r.astype(BF16)
    c = (r - b.astype(F32)).astype(BF16)
    return a, b, c


def _pair_weights(c, q, k, level_of_pair):
    n = len(c)
    sub = lax.broadcasted_iota(jnp.int32, c[0].shape, 0)
    n_levels = (n * SUBLANES).bit_length() - 1
    q_all = jnp.concatenate(q, axis=0).astype(BF16)
    k_all = jnp.concatenate(k, axis=0).astype(BF16)
    a = jnp.where(level_of_pair == n_levels, _dot_nt(q_all, k_all), 0.0)
    for lev in range(n_levels):
        half = 1 << lev
        xs = []
        for i in range(n):
            if half >= SUBLANES:
                nv = 2 * half // SUBLANES
                first = (i // nv) * nv
                upper = (i % nv) >= nv // 2
                ref = c[first + nv // 2 - 1][SUBLANES - 1:SUBLANES]
                arg = c[i] - ref if upper else ref - c[i]
                base = q[i] if upper else k[i]
            else:
                upper = (sub & half) != 0
                if half == SUBLANES // 2:
                    ref = c[i][half - 1:half]
                elif half == 1:
                    ref = jnp.where(upper, pltpu.roll(c[i], 1, 0), c[i])
                else:
                    odd = (sub & 1) != 0
                    z = jnp.where(odd, c[i], pltpu.roll(c[i], SUBLANES - 1, 0))
                    ref = jnp.where(upper, pltpu.roll(z, 2, 0), z)
                diff = c[i] - ref
                arg = jnp.where(upper, diff, -diff)
                base = jnp.where(upper, q[i], k[i])
            xs.append(base * jnp.exp2(arg))
        x = jnp.concatenate(xs, axis=0).astype(BF16)
        a = jnp.where(level_of_pair == lev, _dot_nt(x, x), a)
    return a


def _gated_norm(o, gain, gate):
    return (_rms(o, gain) * (gate * jax.nn.sigmoid(gate))).astype(BF16)


def _hgrn_kernel(layer, q_ref, f_ref, v_ref, gate_ref, logit_ref, gain_ref, tri_ref, lv_ref,
                 o_ref, state_ref, st_scr, qq_scr, kk_scr, c2_scr, qt_scr, kd_scr):
    step = pl.program_id(0)
    rows = q_ref.shape[0]

    @pl.when(step == 0)
    def _():
        st_scr[...] = jnp.zeros_like(st_scr)

    lb = _lower_bound(logit_ref[...], layer)
    f = lb + (1.0 - lb) * jax.nn.sigmoid(f_ref[...])
    g = jnp.log(f)
    qq = jax.nn.silu(q_ref[...])
    kk = 1.0 - f
    tri = tri_ref[...]
    pieces = _split3(g)
    sums = [functools.reduce(
        lambda a, b: a + b,
        [jnp.dot(tri, piece[c:c + HG_BLOCK], preferred_element_type=F32) for piece in pieces])
        for c in range(0, rows, HG_BLOCK)]
    cum = jnp.concatenate([s[:HG_BLOCK] for s in sums], axis=0)
    rev = jnp.concatenate([s[HG_BLOCK:] for s in sums], axis=0)
    c2 = cum * LOG2E
    c2_scr[...] = c2
    qq_scr[...] = qq
    kk_scr[...] = kk
    qt_scr[...] = qq * jnp.exp2(c2)
    kd_scr[...] = kk * jnp.exp(rev)

    n_groups = HG_BLOCK // SUBLANES

    def block(b, carry):
        base = pl.multiple_of(b * HG_BLOCK, HG_BLOCK)
        level_of_pair = lv_ref[...]
        for h in range(N_HEADS):
            sl = slice(h * HEAD_DIM, (h + 1) * HEAD_DIM)
            groups = lambda ref: [ref[pl.ds(base + SUBLANES * i, SUBLANES), sl] for i in range(n_groups)]
            c_b = groups(c2_scr)
            st = st_scr[h]
            inter = _dot_nt(qt_scr[pl.ds(base, HG_BLOCK), sl].astype(BF16), st.astype(BF16))
            pair = _pair_weights(c_b, groups(qq_scr), groups(kk_scr), level_of_pair)
            v_b = v_ref[pl.ds(base, HG_BLOCK), sl].astype(BF16)
            o = inter + jnp.dot(pair.astype(BF16), v_b, preferred_element_type=F32)
            o_ref[pl.ds(base, HG_BLOCK), sl] = _gated_norm(
                o, gain_ref[...], gate_ref[pl.ds(base, HG_BLOCK), sl])
            decay = jnp.exp2(c_b[-1][SUBLANES - 1:SUBLANES])
            delta = _dot_tn(v_b, kd_scr[pl.ds(base, HG_BLOCK), sl].astype(BF16))
            st_scr[h] = st * decay + delta
        return carry

    for b in range(rows // HG_BLOCK):
        block(b, 0)

    @pl.when(step == pl.num_programs(0) - 1)
    def _():
        for h in range(N_HEADS):
            state_ref[h] = st_scr[h].T


def _chunk_tri():
    idx = np.arange(HG_BLOCK)
    lower = idx[None, :] <= idx[:, None]
    return jnp.asarray(np.concatenate([lower, ~lower], axis=0), dtype=BF16)


def _pair_levels():
    idx = np.arange(HG_BLOCK)
    t, s = idx[:, None], idx[None, :]
    n_levels = HG_BLOCK.bit_length() - 1
    level = np.floor(np.log2(np.maximum(t ^ s, 1))).astype(np.int32)
    return jnp.asarray(np.where(t == s, n_levels, np.where(s < t, level, -1)).astype(np.int32))


def _hgrn_prompt(proj, logits, layer, hg_gain, rows):
    t = proj.shape[0]
    assert t % rows == 0 and rows % HG_BLOCK == 0
    col = lambda g: pl.BlockSpec((rows, GROUP_WIDTH), lambda i, g=g: (i, g))
    n_l = logits.shape[0]
    return pl.pallas_call(
        functools.partial(_hgrn_kernel, layer),
        grid=(t // rows,),
        in_specs=[
            col(0), col(1), col(2), col(3),
            pl.BlockSpec((n_l, GROUP_WIDTH), lambda i: (0, 0)),
            pl.BlockSpec((1, HEAD_DIM), lambda i: (0, 0)),
            pl.BlockSpec((2 * HG_BLOCK, HG_BLOCK), lambda i: (0, 0)),
            pl.BlockSpec((HG_BLOCK, HG_BLOCK), lambda i: (0, 0)),
        ],
        out_specs=[
            pl.BlockSpec((rows, GROUP_WIDTH), lambda i: (i, 0)),
            pl.BlockSpec((N_HEADS, HEAD_DIM, HEAD_DIM), lambda i: (0, 0, 0)),
        ],
        out_shape=[
            jax.ShapeDtypeStruct((t, GROUP_WIDTH), BF16),
            jax.ShapeDtypeStruct((N_HEADS, HEAD_DIM, HEAD_DIM), F32),
        ],
        scratch_shapes=[pltpu.VMEM((N_HEADS, HEAD_DIM, HEAD_DIM), F32)]
        + [pltpu.VMEM((rows, GROUP_WIDTH), F32) for _ in range(5)],
        compiler_params=_params("arbitrary"),
        name="hgrn_prompt",
    )(proj, proj, proj, proj, logits, hg_gain.reshape(1, HEAD_DIM), _chunk_tri(), _pair_levels())


def _hgrn_step_kernel(layer, q_ref, f_ref, v_ref, gate_ref, logit_ref, gain_ref, s_ref, o_ref, snew_ref,
                      o_scr, gate_scr):
    lb = _lower_bound(logit_ref[...], layer)
    nb = q_ref.shape[0]
    eye = (lax.broadcasted_iota(jnp.int32, (HEAD_DIM, HEAD_DIM), 0)
           == lax.broadcasted_iota(jnp.int32, (HEAD_DIM, HEAD_DIM), 1))

    def column(row):
        return jnp.sum(jnp.where(eye, jnp.broadcast_to(row, (HEAD_DIM, HEAD_DIM)), 0.0),
                       axis=-1, keepdims=True)

    for b in range(nb):
        f = lb + (1.0 - lb) * jax.nn.sigmoid(f_ref[b])
        qq = jax.nn.silu(q_ref[b])
        vv = v_ref[b]
        gate_scr[b:b + 1, :] = gate_ref[b]
        for h in range(N_HEADS):
            sl = slice(h * HEAD_DIM, (h + 1) * HEAD_DIM)
            f_col = column(f[:, sl])
            q_col = column(qq[:, sl])
            s_new = f_col * s_ref[b, h] + (1.0 - f_col) * vv[:, sl]
            snew_ref[b, h] = s_new
            o_scr[b:b + 1, sl] = jnp.sum(q_col * s_new, axis=0, keepdims=True)
    for h in range(N_HEADS):
        sl = slice(h * HEAD_DIM, (h + 1) * HEAD_DIM)
        res = _gated_norm(o_scr[:, sl], gain_ref[...], gate_scr[:, sl]).astype(F32)
        for b in range(nb):
            o_ref[b, :, sl] = res[b:b + 1]


def _hgrn_step(proj_s, logits, layer, hg_gain, state):
    b = proj_s.shape[0]
    nb = _row_tile(b, SAMPLES_PER_STEP)
    p3 = proj_s.reshape(b, 1, proj_s.shape[1])
    row = (nb, 1, GROUP_WIDTH)
    n_l = logits.shape[0]
    st_blk = (nb, N_HEADS, HEAD_DIM, HEAD_DIM)
    o, s_new = pl.pallas_call(
        functools.partial(_hgrn_step_kernel, layer),
        grid=(b // nb,),
        in_specs=[pl.BlockSpec(row, lambda i, g=g: (i, 0, g)) for g in (0, 1, 2, 3)]
        + [pl.BlockSpec((n_l, GROUP_WIDTH), lambda i: (0, 0)),
           pl.BlockSpec((1, HEAD_DIM), lambda i: (0, 0)),
           pl.BlockSpec(st_blk, lambda i: (i, 0, 0, 0))],
        out_specs=[pl.BlockSpec(row, lambda i: (i, 0, 0)),
                   pl.BlockSpec(st_blk, lambda i: (i, 0, 0, 0))],
        out_shape=[jax.ShapeDtypeStruct((b, 1, GROUP_WIDTH), F32),
                   jax.ShapeDtypeStruct(state.shape, F32)],
        scratch_shapes=[pltpu.VMEM((nb, GROUP_WIDTH), F32), pltpu.VMEM((nb, GROUP_WIDTH), F32)],
        compiler_params=_params("parallel"),
        name="hgrn_step",
    )(p3, p3, p3, p3, logits, hg_gain.reshape(1, HEAD_DIM), state)
    return o.reshape(b, GROUP_WIDTH), s_new


def _merge_kernel(n_pat, *refs):
    o_refs = refs[:n_pat]
    lse_refs = refs[n_pat:2 * n_pat] if n_pat > 1 else ()
    k = 2 * n_pat if n_pat > 1 else n_pat
    (hg_ref, x_ref, again_ref, wout_ref, npost_ref, npre_ref,
     x1_ref, h2_ref, buf_a, buf_b) = refs[k:]
    tm = x_ref.shape[0]
    i = pl.program_id(0)

    def merge_stage(dst):
        if n_pat > 1:
            lses = [r[...] for r in lse_refs]
            top = functools.reduce(jnp.maximum, lses)
            ws = [jnp.exp(x - top) for x in lses]
            inv = 1.0 / functools.reduce(lambda a, b: a + b, ws)
            ws = [w * inv for w in ws]
        att = []
        sq = jnp.zeros((tm, 1), F32)
        for h in range(N_HEADS):
            sl = slice(h * HEAD_DIM, (h + 1) * HEAD_DIM)
            if n_pat > 1:
                a = functools.reduce(
                    lambda x, y: x + y,
                    [w[:, h:h + 1] * r[h] for w, r in zip(ws, o_refs)])
            else:
                a = o_refs[0][:, sl]
            att.append(a)
            sq = sq + jnp.sum(a * a, axis=-1, keepdims=True)
        dst[:, GROUP_WIDTH:] = hg_ref[...].astype(BF16)
        inv_rms = lax.rsqrt(sq * (1.0 / GROUP_WIDTH) + EPS)
        for h, a in enumerate(att):
            sl = slice(h * HEAD_DIM, (h + 1) * HEAD_DIM)
            dst[:, sl] = (a * inv_rms * again_ref[:, sl]).astype(BF16)

    def project_stage(src):
        mix = jnp.dot(src[...], wout_ref[...], preferred_element_type=F32)
        x1 = x_ref[...] + _rms(mix, npost_ref[...])
        x1_ref[...] = x1
        h2_ref[...] = _rms(x1, npre_ref[...]).astype(BF16)

    @pl.when(i == 0)
    def _():
        buf_b[...] = jnp.zeros_like(buf_b)

    @pl.when(i % 2 == 0)
    def _():
        project_stage(buf_b)
        merge_stage(buf_a)

    @pl.when(i % 2 == 1)
    def _():
        project_stage(buf_a)
        merge_stage(buf_b)


def _merge(o_list, lse_list, o_hg, x, attn_gain, w_out_bf16, n_post, n_pre, tm):
    t, d = x.shape
    n_pat = len(o_list)
    n = t // tm
    cur = lambda i: jnp.minimum(i, n - 1)
    prev = lambda i: jnp.maximum(i - 1, 0)
    row_cur = lambda w: pl.BlockSpec((tm, w), lambda i: (cur(i), 0))
    row_prev = lambda w: pl.BlockSpec((tm, w), lambda i: (prev(i), 0))
    full = lambda a: pl.BlockSpec(a.shape, lambda i: (0,) * a.ndim, pipeline_mode=pl.Buffered(1))
    args = list(o_list)
    if n_pat > 1:
        in_specs = [pl.BlockSpec((N_HEADS, tm, HEAD_DIM), lambda i: (0, cur(i), 0)) for _ in o_list]
        args += list(lse_list)
        in_specs += [row_cur(HEAD_DIM) for _ in lse_list]
    else:
        in_specs = [row_cur(GROUP_WIDTH)]
    small = [attn_gain.reshape(1, -1), w_out_bf16, n_post.reshape(1, -1), n_pre.reshape(1, -1)]
    args += [o_hg, x] + small
    in_specs += [row_cur(GROUP_WIDTH), row_prev(d)]
    in_specs += [full(a) for a in small]
    return pl.pallas_call(
        functools.partial(_merge_kernel, n_pat),
        grid=(n + 1,),
        in_specs=in_specs,
        out_specs=[row_prev(d), row_prev(d)],
        out_shape=[jax.ShapeDtypeStruct((t, d), F32), jax.ShapeDtypeStruct((t, d), BF16)],
        scratch_shapes=[pltpu.VMEM((tm, 2 * GROUP_WIDTH), BF16) for _ in range(2)],
        compiler_params=_params("arbitrary"),
        name="merge_outproj",
    )(*args)


def _ffn_kernel(h_ref, x1_ref, wg_ref, wu_ref, wd_ref, npost_ref, y_ref):
    j = pl.program_id(1)
    tf = wg_ref.shape[1]
    d = wd_ref.shape[1]
    acc_scr = y_ref

    @pl.when(j == 0)
    def _():
        acc_scr[...] = jnp.zeros_like(acc_scr)

    h = h_ref[...]
    acts = []
    for c in range(0, tf, FFN_CHUNK):
        gate = jnp.dot(h, wg_ref[:, c:c + FFN_CHUNK], preferred_element_type=F32)
        up = jnp.dot(h, wu_ref[:, c:c + FFN_CHUNK], preferred_element_type=F32)
        acts.append((gate * jax.nn.sigmoid(gate) * up).astype(BF16))
    act = jnp.concatenate(acts, axis=-1)
    for c in range(0, d, FFN_CHUNK):
        cols = slice(c, c + FFN_CHUNK)
        acc_scr[:, cols] += jnp.dot(act, wd_ref[:, cols], preferred_element_type=F32)

    @pl.when(j == pl.num_programs(1) - 1)
    def _():
        y_ref[...] = x1_ref[...] + _rms(acc_scr[...], npost_ref[...])


def _ffn(h2, x1, wg, wu, wd, n_post, tm, tf):
    t, d = x1.shape
    dff = wg.shape[1]
    assert dff % tf == 0
    return pl.pallas_call(
        _ffn_kernel,
        grid=(t // tm, dff // tf),
        in_specs=[
            pl.BlockSpec((tm, d), lambda i, j: (i, 0)),
            pl.BlockSpec((tm, d), lambda i, j: (i, 0)),
            pl.BlockSpec((d, tf), lambda i, j: (0, j)),
            pl.BlockSpec((d, tf), lambda i, j: (0, j)),
            pl.BlockSpec((tf, d), lambda i, j: (j, 0)),
            pl.BlockSpec((1, d), lambda i, j: (0, 0)),
        ],
        out_specs=pl.BlockSpec((tm, d), lambda i, j: (i, 0)),
        out_shape=jax.ShapeDtypeStruct((t, d), F32),
        compiler_params=_params("parallel", "arbitrary"),
        name="ffn",
    )(h2, x1, wg, wu, wd, n_post.reshape(1, d))


def _row_tile(t, want):
    tm = min(t, want)
    assert t % tm == 0
    return tm


INPROJ_ROWS, INPROJ_COLS = 1024, 512
HGRN_ROWS = 512
MERGE_ROWS = 512
FFN_ROWS, FFN_COLS = 1024, 256
SAMPLE_FFN_COLS = 512
SAMPLES_PER_STEP = 4
ATTN_BLOCKS_PER_STEP = 8


def kernel(x_prompt, x_sample, cache_win_k, cache_win_v, state_hgrn, norm_pre_mix, w_in,
           hg_lb_logits, attn_out_gain, hg_norm_gain, w_out, norm_post_mix, norm_pre_ffn,
           w_gate, w_up, w_down, norm_post_ffn):
    depth = w_in.shape[0]
    bp, t, d = x_prompt.shape
    bs, ts, _ = x_sample.shape
    assert bp == 1 and ts == 1
    keep = min(max(w for w, _ in PATTERNS), t)
    logits = hg_lb_logits.astype(F32)

    yp = x_prompt.reshape(t, d)
    ys = x_sample.reshape(bs, d)
    outs = [[] for _ in range(6)]
    for l in range(depth):
        w_in_b, w_out_b = w_in[l].astype(BF16), w_out[l].astype(BF16)
        wg_b, wu_b, wd_b = w_gate[l].astype(BF16), w_up[l].astype(BF16), w_down[l].astype(BF16)

        dils = tuple(dil for _, dil in PATTERNS)
        hg_cols, proj, *dec = _inproj(yp, norm_pre_mix[l], w_in_b, _row_tile(t, INPROJ_ROWS),
                                      INPROJ_COLS, dils, keep)
        o_list, lse_list = zip(*[_band_attn(qkv, dil) for qkv, dil in zip(dec, dils)])
        o_hg, s_fin = _hgrn_prompt(hg_cols, logits, l, hg_norm_gain[l], _row_tile(t, HGRN_ROWS))
        x1, h2 = _merge(o_list, lse_list, o_hg, yp, attn_out_gain[l],
                        w_out_b, norm_post_mix[l], norm_pre_ffn[l], _row_tile(t, MERGE_ROWS))
        yp = _ffn(h2, x1, wg_b, wu_b, wd_b, norm_post_ffn[l], _row_tile(t, FFN_ROWS), FFN_COLS)
        outs[0].append(proj[:, :GROUP_WIDTH].reshape(1, keep, N_HEADS, HEAD_DIM))
        outs[1].append(proj[:, GROUP_WIDTH:].reshape(1, keep, N_HEADS, HEAD_DIM))
        outs[2].append(s_fin.reshape(1, N_HEADS, HEAD_DIM, HEAD_DIM))

        (proj_s,) = _inproj(ys, norm_pre_mix[l], w_in_b, bs, 2 * INPROJ_COLS)
        hg_cols_s = proj_s[:, 3 * GROUP_WIDTH:]
        o_att_s = _sample_attn(proj_s, cache_win_k[l], cache_win_v[l])
        o_hg_s, s_new = _hgrn_step(hg_cols_s, logits, l, hg_norm_gain[l], state_hgrn[l])
        x1s, h2s = _merge([o_att_s], [], o_hg_s, ys, attn_out_gain[l],
                          w_out_b, norm_post_mix[l], norm_pre_ffn[l], bs)
        ys = _ffn(h2s, x1s, wg_b, wu_b, wd_b, norm_post_ffn[l], bs, SAMPLE_FFN_COLS)
        outs[3].append(proj_s[:, GROUP_WIDTH:2 * GROUP_WIDTH].reshape(bs, 1, N_HEADS, HEAD_DIM))
        outs[4].append(proj_s[:, 2 * GROUP_WIDTH:3 * GROUP_WIDTH].reshape(bs, 1, N_HEADS, HEAD_DIM))
        outs[5].append(s_new)

    st = lambda xs: jnp.stack(xs)
    return (yp.reshape(1, t, d), ys.reshape(bs, 1, d), st(outs[0]), st(outs[1]), st(outs[2]),
            st(outs[3]), st(outs[4]), st(outs[5]))
```

```python
import functools

import jax
import jax.numpy as jnp
import numpy as np
from jax import lax
from jax.experimental import pallas as pl
from jax.experimental.pallas import tpu as pltpu

F32 = jnp.float32
BF16 = jnp.bfloat16

EPS = 1e-6
HEAD_DIM = 128
N_HEADS = 8
GROUP_WIDTH = N_HEADS * HEAD_DIM
N_PROJ_GROUPS = 7
PATTERNS = ((128, 1), (512, 4), (2048, 16))
BAND = 128
NEG_BIG = -1e30
LOG2E = 1.4426950408889634
HG_BLOCK = 128
SUBLANES = 8
MXU_COLS = 256
X_PARTS = 4
FFN_CHUNK = MXU_COLS
VMEM_LIMIT = 56 * 1024 * 1024


def _params(*sem):
    return pltpu.CompilerParams(dimension_semantics=sem, vmem_limit_bytes=VMEM_LIMIT)


def _rms(x, gain):
    ms = jnp.mean(x * x, axis=-1, keepdims=True)
    return x * lax.rsqrt(ms + EPS) * gain


def _dot_nt(a, b):
    return lax.dot_general(a, b, (((1,), (1,)), ((), ())), preferred_element_type=F32)


def _dot_tn(a, b):
    return lax.dot_general(a, b, (((0,), (0,)), ((), ())), preferred_element_type=F32)


def _inproj_kernel(dils, n_parts, *refs):
    x_parts, (g_ref, w_ref, o_ref), rest = refs[:n_parts], refs[n_parts:n_parts + 3], refs[n_parts + 3:]
    if dils:
        kv_ref, rest = rest[0], rest[1:]
    dec_refs, h_scr = rest[:len(dils)], rest[len(dils)]
    j = pl.program_id(1)
    tm, part = x_parts[0].shape

    @pl.when(j == 0)
    def _():
        sq = functools.reduce(lambda a, b: a + b,
                              [jnp.sum(p[...] * p[...], axis=-1, keepdims=True) for p in x_parts])
        inv = lax.rsqrt(sq * (1.0 / (n_parts * part)) + EPS)
        for q, p in enumerate(x_parts):
            cols = slice(q * part, (q + 1) * part)
            h_scr[:, cols] = (p[...] * inv * g_ref[:, cols]).astype(BF16)

    if not dils:
        o_ref[...] = jnp.dot(h_scr[...], w_ref[...], preferred_element_type=F32)
        return

    head_scr = rest[len(dils) + 1]
    plane_scrs = dict(zip(dils[1:-1], rest[len(dils) + 2:]))
    n_att = 3 * GROUP_WIDTH // w_ref.shape[1]

    @pl.when(j >= n_att)
    def _():
        o_ref[...] = jnp.dot(h_scr[...], w_ref[...], preferred_element_type=F32)

    @pl.when(j < n_att)
    def _():
        h = h_scr[...]
        for c in range(0, w_ref.shape[1], MXU_COLS):
            res = jnp.dot(h, w_ref[:, c:c + MXU_COLS], preferred_element_type=F32)
            kv_ref[:, c:c + MXU_COLS] = res
            for hh in range(c // HEAD_DIM, (c + MXU_COLS) // HEAD_DIM):
                sl = slice(hh * HEAD_DIM, (hh + 1) * HEAD_DIM)
                head_res = res[:, hh * HEAD_DIM - c:(hh + 1) * HEAD_DIM - c]
                head_scr[hh] = head_res
                dec_refs[0][0, :, sl] = head_res.astype(BF16)
                prev, planes = 1, {0: head_scr.at[hh]}
                for dil, dec_ref in zip(dils[1:], dec_refs[1:]):
                    ratio, new_planes = dil // prev, {}
                    for rp, src in planes.items():
                        for q in range(ratio):
                            r = rp + prev * q
                            val = src[pl.ds(q, tm // dil, stride=ratio), :]
                            dec_ref[r, :, sl] = val.astype(BF16)
                            if dil in plane_scrs:
                                plane_scrs[dil][hh * dil + r] = val
                                new_planes[r] = plane_scrs[dil].at[hh * dil + r]
                    prev, planes = dil, new_planes


def _inproj(x, gain, w_bf16, tm, tn, dils=(), keep=0):
    t, d = x.shape
    n = w_bf16.shape[1]
    assert t % tm == 0 and n % tn == 0
    if dils:
        n_att = 3 * GROUP_WIDTH // tn
        assert n == N_PROJ_GROUPS * GROUP_WIDTH and tn % MXU_COLS == 0 and GROUP_WIDTH % tn == 0
        assert dils[0] == 1 and all(b % a == 0 for a, b in zip(dils, dils[1:]))
        out_specs = [pl.BlockSpec((tm, tn), lambda i, j: (i, jnp.maximum(j - n_att, 0)))]
        out_shape = [jax.ShapeDtypeStruct((t, n - 3 * GROUP_WIDTH), F32)]
        assert keep % tm == 0 and 0 < keep <= t
        first_kept, k0, n_kv = (t - keep) // tm, GROUP_WIDTH // tn, 2 * GROUP_WIDTH // tn
        out_specs.append(pl.BlockSpec(
            (tm, tn),
            lambda i, j: (jnp.maximum(i - first_kept, 0),
                          jnp.where(i < first_kept, 0, jnp.clip(j - k0, 0, n_kv - 1)))))
        out_shape.append(jax.ShapeDtypeStruct((keep, 2 * GROUP_WIDTH), F32))
        for dil in dils:
            assert tm % (16 * dil) == 0
            out_specs.append(pl.BlockSpec((dil, tm // dil, tn),
                                          lambda i, j: (0, i, jnp.minimum(j, n_att - 1))))
            out_shape.append(jax.ShapeDtypeStruct((dil, t // dil, 3 * GROUP_WIDTH), BF16))
    else:
        out_specs = [pl.BlockSpec((tm, tn), lambda i, j: (i, j))]
        out_shape = [jax.ShapeDtypeStruct((t, n), F32)]
    n_i, n_j = t // tm, n // tn
    part = d // X_PARTS

    def x_spec(q):
        switch = max(min(n_j // 2 - 1 + q, n_j - 1), 1)
        return pl.BlockSpec(
            (tm, part),
            lambda i, j: (jnp.minimum(i + jnp.where(j >= switch, 1, 0), n_i - 1), q))

    return pl.pallas_call(
        functools.partial(_inproj_kernel, tuple(dils), X_PARTS),
        grid=(n_i, n_j),
        in_specs=[x_spec(q) for q in range(X_PARTS)] + [
            pl.BlockSpec((1, d), lambda i, j: (0, 0)),
            pl.BlockSpec((d, tn), lambda i, j: (0, j)),
        ],
        out_specs=out_specs,
        out_shape=out_shape,
        scratch_shapes=[pltpu.VMEM((tm, d), BF16)]
        + ([pltpu.VMEM((tn // HEAD_DIM, tm, HEAD_DIM), F32)] if dils else [])
        + [pltpu.VMEM((tn // HEAD_DIM * dil, tm // dil, HEAD_DIM), F32) for dil in dils[1:-1]],
        compiler_params=_params("arbitrary", "arbitrary"),
        name="inproj",
    )(*([x] * X_PARTS), gain.reshape(1, d), w_bf16)


def _band_attn_kernel(dil, nres, nsub, q_ref, k_ref, v_ref, o_ref, lse_ref,
                      o_scr, lse_scr, kprev_scr, vprev_scr):
    i = pl.program_id(0)
    rg = pl.program_id(1)
    bq = BAND
    row = lax.broadcasted_iota(jnp.int32, (bq, 2 * bq), 0)
    col = lax.broadcasted_iota(jnp.int32, (bq, 2 * bq), 1)
    dist = row + bq - col
    in_band = (dist >= 0) & (dist <= BAND)
    first_valid = in_band & (col >= jnp.where(i == 0, bq, 0))
    lane = lax.broadcasted_iota(jnp.int32, (bq, HEAD_DIM), 1)
    ones = jnp.ones((2 * bq, HEAD_DIM), BF16)
    scale = HEAD_DIM ** -0.5
    for rr in range(nres):
        r = rg * nres + rr

        @pl.when(i == 0)
        def _():
            kprev_scr[r] = jnp.zeros(kprev_scr.shape[1:], kprev_scr.dtype)
            vprev_scr[r] = jnp.zeros(vprev_scr.shape[1:], vprev_scr.dtype)

        for sub in range(nsub):
            rows = slice(sub * bq, (sub + 1) * bq)
            before = slice((sub - 1) * bq, sub * bq)
            valid = first_valid if sub == 0 else in_band
            lse_tile = jnp.zeros((bq, HEAD_DIM), F32)
            for h in range(N_HEADS):
                sl = slice(h * HEAD_DIM, (h + 1) * HEAD_DIM)
                q = q_ref[rr, rows, sl]
                k_prev = kprev_scr[r, :, sl] if sub == 0 else k_ref[rr, before, sl]
                v_prev = vprev_scr[r, :, sl] if sub == 0 else v_ref[rr, before, sl]
                k = jnp.concatenate([k_prev, k_ref[rr, rows, sl]], axis=0)
                v = jnp.concatenate([v_prev, v_ref[rr, rows, sl]], axis=0)
                s = jnp.where(valid, _dot_nt(q, k), NEG_BIG)
                m = jnp.max(s, axis=-1, keepdims=True)
                p = jnp.exp2((s - m) * (scale * LOG2E))
                ov = jnp.dot(p.astype(BF16), jnp.concatenate([v, ones], axis=1),
                             preferred_element_type=F32)
                l = ov[:, HEAD_DIM:]
                o_scr[rr * nsub + sub, h] = ov[:, :HEAD_DIM] * (1.0 / l)
                lse_tile = jnp.where(lane == h, m * scale + jnp.log(l), lse_tile)
            lse_scr[rr * nsub + sub] = lse_tile
        last = slice((nsub - 1) * bq, nsub * bq)
        kprev_scr[r] = k_ref[rr, last, :]
        vprev_scr[r] = v_ref[rr, last, :]

    def write_out(first_residue):
        for rr in range(nres):
            for sub in range(nsub):
                start = dil * sub * bq + first_residue + rr
                dst = pl.ds(start, bq, stride=dil) if dil > 1 else pl.ds(start, bq)
                for h in range(N_HEADS):
                    o_ref[h, dst, :] = o_scr[rr * nsub + sub, h]
                lse_ref[dst, :] = lse_scr[rr * nsub + sub]

    if dil == nres:
        write_out(0)
    else:
        for g in range(dil // nres):
            pl.when(rg == g)(functools.partial(write_out, g * nres))


def _band_attn(qkv, dil):
    rows = qkv.shape[1]
    t = rows * dil
    bq = BAND
    nres = min(dil, ATTN_BLOCKS_PER_STEP)
    nsub = min(ATTN_BLOCKS_PER_STEP // nres, rows // bq)
    assert qkv.shape[0] == dil and rows % (nsub * bq) == 0 and dil % nres == 0
    blk = (nres, nsub * bq, GROUP_WIDTH)
    cur = lambda g: (lambda i, rg: (rg, i, g))
    carry = pltpu.VMEM((dil, bq, GROUP_WIDTH), qkv.dtype)
    return pl.pallas_call(
        functools.partial(_band_attn_kernel, dil, nres, nsub),
        grid=(rows // (nsub * bq), dil // nres),
        in_specs=[pl.BlockSpec(blk, cur(g)) for g in range(3)],
        out_specs=[
            pl.BlockSpec((N_HEADS, dil * nsub * bq, HEAD_DIM), lambda i, rg: (0, i, 0)),
            pl.BlockSpec((dil * nsub * bq, HEAD_DIM), lambda i, rg: (i, 0)),
        ],
        out_shape=[
            jax.ShapeDtypeStruct((N_HEADS, t, HEAD_DIM), F32),
            jax.ShapeDtypeStruct((t, HEAD_DIM), F32),
        ],
        scratch_shapes=[pltpu.VMEM((nres * nsub, N_HEADS, bq, HEAD_DIM), F32),
                        pltpu.VMEM((nres * nsub, bq, HEAD_DIM), F32), carry, carry],
        compiler_params=_params("arbitrary", "arbitrary"),
        name=f"band_attn_d{dil}",
    )(qkv, qkv, qkv)


def _sample_attn_kernel(q_ref, kn_ref, vn_ref, *rest):
    n_pat = len(PATTERNS)
    cache_refs = rest[: 2 * n_pat]
    o_ref = rest[2 * n_pat]
    scale = HEAD_DIM ** -0.5
    for b in range(q_ref.shape[0]):
        q = q_ref[b]
        kn = kn_ref[b]
        vn = vn_ref[b]
        s_new = jnp.sum(q * kn, axis=-1, keepdims=True) * scale
        outs, lses = [], []
        for p in range(n_pat):
            kc = cache_refs[2 * p][b]
            vc = cache_refs[2 * p + 1][b]
            s = jnp.sum(kc * q[None], axis=-1, keepdims=True) * scale
            m = jnp.maximum(jnp.max(s, axis=0), s_new)
            e = jnp.exp(s - m[None])
            e_new = jnp.exp(s_new - m)
            l = jnp.sum(e, axis=0) + e_new
            o = jnp.sum(e * vc, axis=0) + e_new * vn
            outs.append(o * (1.0 / l))
            lses.append(m + jnp.log(l))
        top = functools.reduce(jnp.maximum, lses)
        ws = [jnp.exp(x - top) for x in lses]
        den = functools.reduce(lambda a, b: a + b, ws)
        o_ref[b] = functools.reduce(lambda a, b: a + b, [w * o for w, o in zip(ws, outs)]) * (1.0 / den)


def _sample_attn(proj_s, cache_k, cache_v):
    b = proj_s.shape[0]
    past = cache_k.shape[1]
    assert past == max(w for w, _ in PATTERNS), "every dilated key must lie inside the cached window"
    nb = _row_tile(b, SAMPLES_PER_STEP)
    head_blk = (nb, N_HEADS, HEAD_DIM)
    in_specs = [pl.BlockSpec(head_blk, lambda i: (i, 0, 0)) for _ in range(3)]
    args = [proj_s[:, g * GROUP_WIDTH:(g + 1) * GROUP_WIDTH].reshape(b, N_HEADS, HEAD_DIM)
            for g in range(3)]
    for _, dil in PATTERNS:
        rows = past // dil
        last_blk = rows // BAND - 1
        for c in (cache_k, cache_v):
            args.append(c.reshape(b, rows, dil, N_HEADS, HEAD_DIM))
            in_specs.append(pl.BlockSpec((nb, BAND, None, N_HEADS, HEAD_DIM),
                                         lambda i, lb=last_blk: (i, lb, 0, 0, 0)))
    out = pl.pallas_call(
        _sample_attn_kernel,
        grid=(b // nb,),
        in_specs=in_specs,
        out_specs=pl.BlockSpec(head_blk, lambda i: (i, 0, 0)),
        out_shape=jax.ShapeDtypeStruct((b, N_HEADS, HEAD_DIM), F32),
        compiler_params=_params("parallel"),
        name="sample_attn",
    )(*args)
    return out.reshape(b, GROUP_WIDTH)


def _lower_bound(logits, layer):
    top = jnp.max(logits, axis=0, keepdims=True)
    e = jnp.exp(logits - top)
    return jnp.sum(e[: layer + 1], axis=0, keepdims=True) / jnp.sum(e, axis=0, keepdims=True)


def _split3(x):
    a = x.astype(BF16)
    r = x - a.astype(F32)
    b = r.astype(BF16)
    c = (r - b.astype(F32)).astype(BF16)
    return a, b, c


def _pair_weights(c, q, k, level_of_pair):
    n = len(c)
    sub = lax.broadcasted_iota(jnp.int32, c[0].shape, 0)
    n_levels = (n * SUBLANES).bit_length() - 1
    q_all = jnp.concatenate(q, axis=0).astype(BF16)
    k_all = jnp.concatenate(k, axis=0).astype(BF16)
    a = jnp.where(level_of_pair == n_levels, _dot_nt(q_all, k_all), 0.0)
    for lev in range(n_levels):
        half = 1 << lev
        xs = []
        for i in range(n):
            if half >= SUBLANES:
                nv = 2 * half // SUBLANES
                first = (i // nv) * nv
                upper = (i % nv) >= nv // 2
                ref = c[first + nv // 2 - 1][SUBLANES - 1:SUBLANES]
                arg = c[i] - ref if upper else ref - c[i]
                base = q[i] if upper else k[i]
            else:
                upper = (sub & half) != 0
                if half == SUBLANES // 2:
                    ref = c[i][half - 1:half]
                elif half == 1:
                    ref = jnp.where(upper, pltpu.roll(c[i], 1, 0), c[i])
                else:
                    odd = (sub & 1) != 0
                    z = jnp.where(odd, c[i], pltpu.roll(c[i], SUBLANES - 1, 0))
                    ref = jnp.where(upper, pltpu.roll(z, 2, 0), z)
                diff = c[i] - ref
                arg = jnp.where(upper, diff, -diff)
                base = jnp.where(upper, q[i], k[i])
            xs.append(base * jnp.exp2(arg))
        x = jnp.concatenate(xs, axis=0).astype(BF16)
        a = jnp.where(level_of_pair == lev, _dot_nt(x, x), a)
    return a


def _gated_norm(o, gain, gate):
    return (_rms(o, gain) * (gate * jax.nn.sigmoid(gate))).astype(BF16)


def _hgrn_kernel(layer, q_ref, f_ref, v_ref, gate_ref, logit_ref, gain_ref, tri_ref, lv_ref,
                 o_ref, state_ref, st_scr, qq_scr, kk_scr, c2_scr, qt_scr, kd_scr):
    step = pl.program_id(0)
    rows = q_ref.shape[0]

    @pl.when(step == 0)
    def _():
        st_scr[...] = jnp.zeros_like(st_scr)

    lb = _lower_bound(logit_ref[...], layer)
    f = lb + (1.0 - lb) * jax.nn.sigmoid(f_ref[...])
    g = jnp.log(f)
    qq = jax.nn.silu(q_ref[...])
    kk = 1.0 - f
    tri = tri_ref[...]
    pieces = _split3(g)
    sums = [functools.reduce(
        lambda a, b: a + b,
        [jnp.dot(tri, piece[c:c + HG_BLOCK], preferred_element_type=F32) for piece in pieces])
        for c in range(0, rows, HG_BLOCK)]
    cum = jnp.concatenate([s[:HG_BLOCK] for s in sums], axis=0)
    rev = jnp.concatenate([s[HG_BLOCK:] for s in sums], axis=0)
    c2 = cum * LOG2E
    c2_scr[...] = c2
    qq_scr[...] = qq
    kk_scr[...] = kk
    qt_scr[...] = qq * jnp.exp2(c2)
    kd_scr[...] = kk * jnp.exp(rev)

    n_groups = HG_BLOCK // SUBLANES

    def block(b, carry):
        base = pl.multiple_of(b * HG_BLOCK, HG_BLOCK)
        level_of_pair = lv_ref[...]
        for h in range(N_HEADS):
            sl = slice(h * HEAD_DIM, (h + 1) * HEAD_DIM)
            groups = lambda ref: [ref[pl.ds(base + SUBLANES * i, SUBLANES), sl] for i in range(n_groups)]
            c_b = groups(c2_scr)
            st = st_scr[h]
            inter = _dot_nt(qt_scr[pl.ds(base, HG_BLOCK), sl].astype(BF16), st.astype(BF16))
            pair = _pair_weights(c_b, groups(qq_scr), groups(kk_scr), level_of_pair)
            v_b = v_ref[pl.ds(base, HG_BLOCK), sl].astype(BF16)
            o = inter + jnp.dot(pair.astype(BF16), v_b, preferred_element_type=F32)
            o_ref[pl.ds(base, HG_BLOCK), sl] = _gated_norm(
                o, gain_ref[...], gate_ref[pl.ds(base, HG_BLOCK), sl])
            decay = jnp.exp2(c_b[-1][SUBLANES - 1:SUBLANES])
            delta = _dot_tn(v_b, kd_scr[pl.ds(base, HG_BLOCK), sl].astype(BF16))
            st_scr[h] = st * decay + delta
        return carry

    for b in range(rows // HG_BLOCK):
        block(b, 0)

    @pl.when(step == pl.num_programs(0) - 1)
    def _():
        for h in range(N_HEADS):
            state_ref[h] = st_scr[h].T


def _chunk_tri():
    idx = np.arange(HG_BLOCK)
    lower = idx[None, :] <= idx[:, None]
    return jnp.asarray(np.concatenate([lower, ~lower], axis=0), dtype=BF16)


def _pair_levels():
    idx = np.arange(HG_BLOCK)
    t, s = idx[:, None], idx[None, :]
    n_levels = HG_BLOCK.bit_length() - 1
    level = np.floor(np.log2(np.maximum(t ^ s, 1))).astype(np.int32)
    return jnp.asarray(np.where(t == s, n_levels, np.where(s < t, level, -1)).astype(np.int32))


def _hgrn_prompt(proj, logits, layer, hg_gain, rows):
    t = proj.shape[0]
    assert t % rows == 0 and rows % HG_BLOCK == 0
    col = lambda g: pl.BlockSpec((rows, GROUP_WIDTH), lambda i, g=g: (i, g))
    n_l = logits.shape[0]
    return pl.pallas_call(
        functools.partial(_hgrn_kernel, layer),
        grid=(t // rows,),
        in_specs=[
            col(0), col(1), col(2), col(3),
            pl.BlockSpec((n_l, GROUP_WIDTH), lambda i: (0, 0)),
            pl.BlockSpec((1, HEAD_DIM), lambda i: (0, 0)),
            pl.BlockSpec((2 * HG_BLOCK, HG_BLOCK), lambda i: (0, 0)),
            pl.BlockSpec((HG_BLOCK, HG_BLOCK), lambda i: (0, 0)),
        ],
        out_specs=[
            pl.BlockSpec((rows, GROUP_WIDTH), lambda i: (i, 0)),
            pl.BlockSpec((N_HEADS, HEAD_DIM, HEAD_DIM), lambda i: (0, 0, 0)),
        ],
        out_shape=[
            jax.ShapeDtypeStruct((t, GROUP_WIDTH), BF16),
            jax.ShapeDtypeStruct((N_HEADS, HEAD_DIM, HEAD_DIM), F32),
        ],
        scratch_shapes=[pltpu.VMEM((N_HEADS, HEAD_DIM, HEAD_DIM), F32)]
        + [pltpu.VMEM((rows, GROUP_WIDTH), F32) for _ in range(5)],
        compiler_params=_params("arbitrary"),
        name="hgrn_prompt",
    )(proj, proj, proj, proj, logits, hg_gain.reshape(1, HEAD_DIM), _chunk_tri(), _pair_levels())


def _hgrn_step_kernel(layer, q_ref, f_ref, v_ref, gate_ref, logit_ref, gain_ref, s_ref, o_ref, snew_ref,
                      o_scr, gate_scr):
    lb = _lower_bound(logit_ref[...], layer)
    nb = q_ref.shape[0]
    eye = (lax.broadcasted_iota(jnp.int32, (HEAD_DIM, HEAD_DIM), 0)
           == lax.broadcasted_iota(jnp.int32, (HEAD_DIM, HEAD_DIM), 1))

    def column(row):
        return jnp.sum(jnp.where(eye, jnp.broadcast_to(row, (HEAD_DIM, HEAD_DIM)), 0.0),
                       axis=-1, keepdims=True)

    for b in range(nb):
        f = lb + (1.0 - lb) * jax.nn.sigmoid(f_ref[b])
        qq = jax.nn.silu(q_ref[b])
        vv = v_ref[b]
        gate_scr[b:b + 1, :] = gate_ref[b]
        for h in range(N_HEADS):
            sl = slice(h * HEAD_DIM, (h + 1) * HEAD_DIM)
            f_col = column(f[:, sl])
            q_col = column(qq[:, sl])
            s_new = f_col * s_ref[b, h] + (1.0 - f_col) * vv[:, sl]
            snew_ref[b, h] = s_new
            o_scr[b:b + 1, sl] = jnp.sum(q_col * s_new, axis=0, keepdims=True)
    for h in range(N_HEADS):
        sl = slice(h * HEAD_DIM, (h + 1) * HEAD_DIM)
        res = _gated_norm(o_scr[:, sl], gain_ref[...], gate_scr[:, sl]).astype(F32)
        for b in range(nb):
            o_ref[b, :, sl] = res[b:b + 1]


def _hgrn_step(proj_s, logits, layer, hg_gain, state):
    b = proj_s.shape[0]
    nb = _row_tile(b, SAMPLES_PER_STEP)
    p3 = proj_s.reshape(b, 1, proj_s.shape[1])
    row = (nb, 1, GROUP_WIDTH)
    n_l = logits.shape[0]
    st_blk = (nb, N_HEADS, HEAD_DIM, HEAD_DIM)
    o, s_new = pl.pallas_call(
        functools.partial(_hgrn_step_kernel, layer),
        grid=(b // nb,),
        in_specs=[pl.BlockSpec(row, lambda i, g=g: (i, 0, g)) for g in (0, 1, 2, 3)]
        + [pl.BlockSpec((n_l, GROUP_WIDTH), lambda i: (0, 0)),
           pl.BlockSpec((1, HEAD_DIM), lambda i: (0, 0)),
           pl.BlockSpec(st_blk, lambda i: (i, 0, 0, 0))],
        out_specs=[pl.BlockSpec(row, lambda i: (i, 0, 0)),
                   pl.BlockSpec(st_blk, lambda i: (i, 0, 0, 0))],
        out_shape=[jax.ShapeDtypeStruct((b, 1, GROUP_WIDTH), F32),
                   jax.ShapeDtypeStruct(state.shape, F32)],
        scratch_shapes=[pltpu.VMEM((nb, GROUP_WIDTH), F32), pltpu.VMEM((nb, GROUP_WIDTH), F32)],
        compiler_params=_params("parallel"),
        name="hgrn_step",
    )(p3, p3, p3, p3, logits, hg_gain.reshape(1, HEAD_DIM), state)
    return o.reshape(b, GROUP_WIDTH), s_new


def _merge_kernel(n_pat, *refs):
    o_refs = refs[:n_pat]
    lse_refs = refs[n_pat:2 * n_pat] if n_pat > 1 else ()
    k = 2 * n_pat if n_pat > 1 else n_pat
    (hg_ref, x_ref, again_ref, wout_ref, npost_ref, npre_ref,
     x1_ref, h2_ref, buf_a, buf_b) = refs[k:]
    tm = x_ref.shape[0]
    i = pl.program_id(0)

    def merge_stage(dst):
        if n_pat > 1:
            lses = [r[...] for r in lse_refs]
            top = functools.reduce(jnp.maximum, lses)
            ws = [jnp.exp(x - top) for x in lses]
            inv = 1.0 / functools.reduce(lambda a, b: a + b, ws)
            ws = [w * inv for w in ws]
        att = []
        sq_lanes = jnp.zeros((tm, HEAD_DIM), F32)
        for h in range(N_HEADS):
            sl = slice(h * HEAD_DIM, (h + 1) * HEAD_DIM)
            if n_pat > 1:
                a = functools.reduce(
                    lambda x, y: x + y,
                    [w[:, h:h + 1] * r[h] for w, r in zip(ws, o_refs)])
            else:
                a = o_refs[0][:, sl]
            att.append(a)
            sq_lanes = sq_lanes + a * a
        sq = jnp.sum(sq_lanes, axis=-1, keepdims=True)
        dst[:, GROUP_WIDTH:] = hg_ref[...].astype(BF16)
        inv_rms = lax.rsqrt(sq * (1.0 / GROUP_WIDTH) + EPS)
        for h, a in enumerate(att):
            sl = slice(h * HEAD_DIM, (h + 1) * HEAD_DIM)
            dst[:, sl] = (a * inv_rms * again_ref[:, sl]).astype(BF16)

    def project_stage(src):
        mix = jnp.dot(src[...], wout_ref[...], preferred_element_type=F32)
        x1 = x_ref[...] + _rms(mix, npost_ref[...])
        x1_ref[...] = x1
        h2_ref[...] = _rms(x1, npre_ref[...]).astype(BF16)

    @pl.when(i == 0)
    def _():
        buf_b[...] = jnp.zeros_like(buf_b)

    @pl.when(i % 2 == 0)
    def _():
        project_stage(buf_b)
        merge_stage(buf_a)

    @pl.when(i % 2 == 1)
    def _():
        project_stage(buf_a)
        merge_stage(buf_b)


def _merge(o_list, lse_list, o_hg, x, attn_gain, w_out_bf16, n_post, n_pre, tm):
    t, d = x.shape
    n_pat = len(o_list)
    n = t // tm
    cur = lambda i: jnp.minimum(i, n - 1)
    prev = lambda i: jnp.maximum(i - 1, 0)
    row_cur = lambda w: pl.BlockSpec((tm, w), lambda i: (cur(i), 0))
    row_prev = lambda w: pl.BlockSpec((tm, w), lambda i: (prev(i), 0))
    full = lambda a: pl.BlockSpec(a.shape, lambda i: (0,) * a.ndim, pipeline_mode=pl.Buffered(1))
    args = list(o_list)
    if n_pat > 1:
        in_specs = [pl.BlockSpec((N_HEADS, tm, HEAD_DIM), lambda i: (0, cur(i), 0)) for _ in o_list]
        args += list(lse_list)
        in_specs += [row_cur(HEAD_DIM) for _ in lse_list]
    else:
        in_specs = [row_cur(GROUP_WIDTH)]
    small = [attn_gain.reshape(1, -1), w_out_bf16, n_post.reshape(1, -1), n_pre.reshape(1, -1)]
    args += [o_hg, x] + small
    in_specs += [row_cur(GROUP_WIDTH), row_prev(d)]
    in_specs += [full(a) for a in small]
    return pl.pallas_call(
        functools.partial(_merge_kernel, n_pat),
        grid=(n + 1,),
        in_specs=in_specs,
        out_specs=[row_prev(d), row_prev(d)],
        out_shape=[jax.ShapeDtypeStruct((t, d), F32), jax.ShapeDtypeStruct((t, d), BF16)],
        scratch_shapes=[pltpu.VMEM((tm, 2 * GROUP_WIDTH), BF16) for _ in range(2)],
        compiler_params=_params("arbitrary"),
        name="merge_outproj",
    )(*args)


def _ffn_kernel(h_ref, x1_ref, wg_ref, wu_ref, wd_ref, npost_ref, y_ref):
    j = pl.program_id(1)
    last = pl.num_programs(1) - 1
    tm = h_ref.shape[0]
    tf = wg_ref.shape[1]
    d = wd_ref.shape[1]

    def step(first, final):
        h = h_ref[...]
        acts = []
        for c in range(0, tf, FFN_CHUNK):
            gate = jnp.dot(h, wg_ref[:, c:c + FFN_CHUNK], preferred_element_type=F32)
            up = jnp.dot(h, wu_ref[:, c:c + FFN_CHUNK], preferred_element_type=F32)
            acts.append((gate * jax.nn.sigmoid(gate) * up).astype(BF16))
        act = jnp.concatenate(acts, axis=-1)
        sq_lanes = jnp.zeros((tm, HEAD_DIM), F32)
        for c in range(0, d, FFN_CHUNK):
            cols = slice(c, c + FFN_CHUNK)
            total = jnp.dot(act, wd_ref[:, cols], preferred_element_type=F32)
            if not first:
                total = total + y_ref[:, cols]
            y_ref[:, cols] = total
            if final:
                for l0 in range(0, FFN_CHUNK, HEAD_DIM):
                    sq_lanes = sq_lanes + total[:, l0:l0 + HEAD_DIM] * total[:, l0:l0 + HEAD_DIM]
        if final:
            sq = jnp.sum(sq_lanes, axis=-1, keepdims=True)
            inv = lax.rsqrt(sq * (1.0 / d) + EPS)
            for c in range(0, d, FFN_CHUNK):
                cols = slice(c, c + FFN_CHUNK)
                y_ref[:, cols] = x1_ref[:, cols] + y_ref[:, cols] * inv * npost_ref[:, cols]

    pl.when(j == 0)(functools.partial(step, True, False))
    pl.when((j > 0) & (j < last))(functools.partial(step, False, False))
    pl.when(j == last)(functools.partial(step, False, True))


def _ffn(h2, x1, wg, wu, wd, n_post, tm, tf):
    t, d = x1.shape
    dff = wg.shape[1]
    assert dff % tf == 0 and dff // tf >= 2
    return pl.pallas_call(
        _ffn_kernel,
        grid=(t // tm, dff // tf),
        in_specs=[
            pl.BlockSpec((tm, d), lambda i, j: (i, 0)),
            pl.BlockSpec((tm, d), lambda i, j: (i, 0)),
            pl.BlockSpec((d, tf), lambda i, j: (0, j)),
            pl.BlockSpec((d, tf), lambda i, j: (0, j)),
            pl.BlockSpec((tf, d), lambda i, j: (j, 0)),
            pl.BlockSpec((1, d), lambda i, j: (0, 0)),
        ],
        out_specs=pl.BlockSpec((tm, d), lambda i, j: (i, 0)),
        out_shape=jax.ShapeDtypeStruct((t, d), F32),
        compiler_params=_params("parallel", "arbitrary"),
        name="ffn",
    )(h2, x1, wg, wu, wd, n_post.reshape(1, d))


def _row_tile(t, want):
    tm = min(t, want)
    assert t % tm == 0
    return tm


INPROJ_ROWS, INPROJ_COLS = 1024, 512
HGRN_ROWS = 512
MERGE_ROWS = 512
FFN_ROWS, FFN_COLS = 1024, 256
SAMPLE_FFN_COLS = 512
SAMPLES_PER_STEP = 4
ATTN_BLOCKS_PER_STEP = 8


def kernel(x_prompt, x_sample, cache_win_k, cache_win_v, state_hgrn, norm_pre_mix, w_in,
           hg_lb_logits, attn_out_gain, hg_norm_gain, w_out, norm_post_mix, norm_pre_ffn,
           w_gate, w_up, w_down, norm_post_ffn):
    depth = w_in.shape[0]
    bp, t, d = x_prompt.shape
    bs, ts, _ = x_sample.shape
    assert bp == 1 and ts == 1
    keep = min(max(w for w, _ in PATTERNS), t)
    logits = hg_lb_logits.astype(F32)

    yp = x_prompt.reshape(t, d)
    ys = x_sample.reshape(bs, d)
    outs = [[] for _ in range(6)]
    for l in range(depth):
        w_in_b, w_out_b = w_in[l].astype(BF16), w_out[l].astype(BF16)
        wg_b, wu_b, wd_b = w_gate[l].astype(BF16), w_up[l].astype(BF16), w_down[l].astype(BF16)

        dils = tuple(dil for _, dil in PATTERNS)
        hg_cols, proj, *dec = _inproj(yp, norm_pre_mix[l], w_in_b, _row_tile(t, INPROJ_ROWS),
                                      INPROJ_COLS, dils, keep)
        o_list, lse_list = zip(*[_band_attn(qkv, dil) for qkv, dil in zip(dec, dils)])
        o_hg, s_fin = _hgrn_prompt(hg_cols, logits, l, hg_norm_gain[l], _row_tile(t, HGRN_ROWS))
        x1, h2 = _merge(o_list, lse_list, o_hg, yp, attn_out_gain[l],
                        w_out_b, norm_post_mix[l], norm_pre_ffn[l], _row_tile(t, MERGE_ROWS))
        yp = _ffn(h2, x1, wg_b, wu_b, wd_b, norm_post_ffn[l], _row_tile(t, FFN_ROWS), FFN_COLS)
        outs[0].append(proj[:, :GROUP_WIDTH].reshape(1, keep, N_HEADS, HEAD_DIM))
        outs[1].append(proj[:, GROUP_WIDTH:].reshape(1, keep, N_HEADS, HEAD_DIM))
        outs[2].append(s_fin.reshape(1, N_HEADS, HEAD_DIM, HEAD_DIM))

        (proj_s,) = _inproj(ys, norm_pre_mix[l], w_in_b, bs, 2 * INPROJ_COLS)
        hg_cols_s = proj_s[:, 3 * GROUP_WIDTH:]
        o_att_s = _sample_attn(proj_s, cache_win_k[l], cache_win_v[l])
        o_hg_s, s_new = _hgrn_step(hg_cols_s, logits, l, hg_norm_gain[l], state_hgrn[l])
        x1s, h2s = _merge([o_att_s], [], o_hg_s, ys, attn_out_gain[l],
                          w_out_b, norm_post_mix[l], norm_pre_ffn[l], bs)
        ys = _ffn(h2s, x1s, wg_b, wu_b, wd_b, norm_post_ffn[l], bs, SAMPLE_FFN_COLS)
        outs[3].append(proj_s[:, GROUP_WIDTH:2 * GROUP_WIDTH].reshape(bs, 1, N_HEADS, HEAD_DIM))
        outs[4].append(proj_s[:, 2 * GROUP_WIDTH:3 * GROUP_WIDTH].reshape(bs, 1, N_HEADS, HEAD_DIM))
        outs[5].append(s_new)

    st = lambda xs: jnp.stack(xs)
    return (yp.reshape(1, t, d), ys.reshape(bs, 1, d), st(outs[0]), st(outs[1]), st(outs[2]),
            st(outs[3]), st(outs[4]), st(outs[5]))
```

```python
import functools

import jax
import jax.numpy as jnp
import numpy as np
from jax import lax
from jax.experimental import pallas as pl
from jax.experimental.pallas import tpu as pltpu

F32 = jnp.float32
BF16 = jnp.bfloat16

EPS = 1e-6
HEAD_DIM = 128
N_HEADS = 8
GROUP_WIDTH = N_HEADS * HEAD_DIM
N_PROJ_GROUPS = 7
PATTERNS = ((128, 1), (512, 4), (2048, 16))
BAND = 128
NEG_BIG = -1e30
LOG2E = 1.4426950408889634
HG_BLOCK = 128
SUBLANES = 8
MXU_COLS = 256
X_PARTS = 4
FFN_CHUNK = MXU_COLS
VMEM_LIMIT = 56 * 1024 * 1024


def _params(*sem):
    return pltpu.CompilerParams(dimension_semantics=sem, vmem_limit_bytes=VMEM_LIMIT)


def _rms(x, gain):
    ms = jnp.mean(x * x, axis=-1, keepdims=True)
    return x * lax.rsqrt(ms + EPS) * gain


def _sigmoid(x):
    return 0.5 * jnp.tanh(0.5 * x) + 0.5


def _dot_nt(a, b):
    return lax.dot_general(a, b, (((1,), (1,)), ((), ())), preferred_element_type=F32)


def _dot_tn(a, b):
    return lax.dot_general(a, b, (((0,), (0,)), ((), ())), preferred_element_type=F32)


def _inproj_kernel(dils, n_parts, *refs):
    x_parts, (g_ref, w_ref, o_ref), rest = refs[:n_parts], refs[n_parts:n_parts + 3], refs[n_parts + 3:]
    if dils:
        kv_ref, rest = rest[0], rest[1:]
    dec_refs, h_scr = rest[:len(dils)], rest[len(dils)]
    j = pl.program_id(1)
    tm, part = x_parts[0].shape

    @pl.when(j == 0)
    def _():
        sq = functools.reduce(lambda a, b: a + b,
                              [jnp.sum(p[...] * p[...], axis=-1, keepdims=True) for p in x_parts])
        inv = lax.rsqrt(sq * (1.0 / (n_parts * part)) + EPS)
        for q, p in enumerate(x_parts):
            cols = slice(q * part, (q + 1) * part)
            h_scr[:, cols] = (p[...] * inv * g_ref[:, cols]).astype(BF16)

    if not dils:
        o_ref[...] = jnp.dot(h_scr[...], w_ref[...], preferred_element_type=F32)
        return

    head_scr = rest[len(dils) + 1]
    plane_scrs = dict(zip(dils[1:-1], rest[len(dils) + 2:]))
    n_att = 3 * GROUP_WIDTH // w_ref.shape[1]

    @pl.when(j >= n_att)
    def _():
        o_ref[...] = jnp.dot(h_scr[...], w_ref[...], preferred_element_type=F32)

    @pl.when(j < n_att)
    def _():
        h = h_scr[...]
        for c in range(0, w_ref.shape[1], MXU_COLS):
            res = jnp.dot(h, w_ref[:, c:c + MXU_COLS], preferred_element_type=F32)
            kv_ref[:, c:c + MXU_COLS] = res
            for hh in range(c // HEAD_DIM, (c + MXU_COLS) // HEAD_DIM):
                sl = slice(hh * HEAD_DIM, (hh + 1) * HEAD_DIM)
                head_res = res[:, hh * HEAD_DIM - c:(hh + 1) * HEAD_DIM - c]
                head_scr[hh] = head_res
                dec_refs[0][0, :, sl] = head_res.astype(BF16)
                prev, planes = 1, {0: head_scr.at[hh]}
                for dil, dec_ref in zip(dils[1:], dec_refs[1:]):
                    ratio, new_planes = dil // prev, {}
                    for rp, src in planes.items():
                        for q in range(ratio):
                            r = rp + prev * q
                            val = src[pl.ds(q, tm // dil, stride=ratio), :]
                            dec_ref[r, :, sl] = val.astype(BF16)
                            if dil in plane_scrs:
                                plane_scrs[dil][hh * dil + r] = val
                                new_planes[r] = plane_scrs[dil].at[hh * dil + r]
                    prev, planes = dil, new_planes


def _inproj(x, gain, w_bf16, tm, tn, dils=(), keep=0):
    t, d = x.shape
    n = w_bf16.shape[1]
    assert t % tm == 0 and n % tn == 0
    if dils:
        n_att = 3 * GROUP_WIDTH // tn
        assert n == N_PROJ_GROUPS * GROUP_WIDTH and tn % MXU_COLS == 0 and GROUP_WIDTH % tn == 0
        assert dils[0] == 1 and all(b % a == 0 for a, b in zip(dils, dils[1:]))
        out_specs = [pl.BlockSpec((tm, tn), lambda i, j: (i, jnp.maximum(j - n_att, 0)))]
        out_shape = [jax.ShapeDtypeStruct((t, n - 3 * GROUP_WIDTH), F32)]
        assert keep % tm == 0 and 0 < keep <= t
        first_kept, k0, n_kv = (t - keep) // tm, GROUP_WIDTH // tn, 2 * GROUP_WIDTH // tn
        out_specs.append(pl.BlockSpec(
            (tm, tn),
            lambda i, j: (jnp.maximum(i - first_kept, 0),
                          jnp.where(i < first_kept, 0, jnp.clip(j - k0, 0, n_kv - 1)))))
        out_shape.append(jax.ShapeDtypeStruct((keep, 2 * GROUP_WIDTH), F32))
        for dil in dils:
            assert tm % (16 * dil) == 0
            out_specs.append(pl.BlockSpec((dil, tm // dil, tn),
                                          lambda i, j: (0, i, jnp.minimum(j, n_att - 1))))
            out_shape.append(jax.ShapeDtypeStruct((dil, t // dil, 3 * GROUP_WIDTH), BF16))
    else:
        out_specs = [pl.BlockSpec((tm, tn), lambda i, j: (i, j))]
        out_shape = [jax.ShapeDtypeStruct((t, n), F32)]
    n_i, n_j = t // tm, n // tn
    part = d // X_PARTS

    def x_spec(q):
        switch = max(min(n_j // 2 - 1 + q, n_j - 1), 1)
        return pl.BlockSpec(
            (tm, part),
            lambda i, j: (jnp.minimum(i + jnp.where(j >= switch, 1, 0), n_i - 1), q))

    return pl.pallas_call(
        functools.partial(_inproj_kernel, tuple(dils), X_PARTS),
        grid=(n_i, n_j),
        in_specs=[x_spec(q) for q in range(X_PARTS)] + [
            pl.BlockSpec((1, d), lambda i, j: (0, 0)),
            pl.BlockSpec((d, tn), lambda i, j: (0, j)),
        ],
        out_specs=out_specs,
        out_shape=out_shape,
        scratch_shapes=[pltpu.VMEM((tm, d), BF16)]
        + ([pltpu.VMEM((tn // HEAD_DIM, tm, HEAD_DIM), F32)] if dils else [])
        + [pltpu.VMEM((tn // HEAD_DIM * dil, tm // dil, HEAD_DIM), F32) for dil in dils[1:-1]],
        compiler_params=_params("arbitrary", "arbitrary"),
        name="inproj",
    )(*([x] * X_PARTS), gain.reshape(1, d), w_bf16)


def _band_attn_kernel(dil, nres, nsub, q_ref, k_ref, v_ref, o_ref, lse_ref,
                      o_scr, lse_scr, kprev_scr, vprev_scr):
    i = pl.program_id(0)
    rg = pl.program_id(1)
    bq = BAND
    row = lax.broadcasted_iota(jnp.int32, (bq, 2 * bq), 0)
    col = lax.broadcasted_iota(jnp.int32, (bq, 2 * bq), 1)
    dist = row + bq - col
    in_band = (dist >= 0) & (dist <= BAND)
    first_valid = in_band & (col >= jnp.where(i == 0, bq, 0))
    lane = lax.broadcasted_iota(jnp.int32, (bq, HEAD_DIM), 1)
    ones = jnp.ones((2 * bq, HEAD_DIM), BF16)
    scale = HEAD_DIM ** -0.5
    for rr in range(nres):
        r = rg * nres + rr

        @pl.when(i == 0)
        def _():
            kprev_scr[r] = jnp.zeros(kprev_scr.shape[1:], kprev_scr.dtype)
            vprev_scr[r] = jnp.zeros(vprev_scr.shape[1:], vprev_scr.dtype)

        for sub in range(nsub):
            rows = slice(sub * bq, (sub + 1) * bq)
            before = slice((sub - 1) * bq, sub * bq)
            valid = first_valid if sub == 0 else in_band
            lse_tile = jnp.zeros((bq, HEAD_DIM), F32)
            for h in range(N_HEADS):
                sl = slice(h * HEAD_DIM, (h + 1) * HEAD_DIM)
                q = q_ref[rr, rows, sl]
                k_prev = kprev_scr[r, :, sl] if sub == 0 else k_ref[rr, before, sl]
                v_prev = vprev_scr[r, :, sl] if sub == 0 else v_ref[rr, before, sl]
                k = jnp.concatenate([k_prev, k_ref[rr, rows, sl]], axis=0)
                v = jnp.concatenate([v_prev, v_ref[rr, rows, sl]], axis=0)
                s = jnp.where(valid, _dot_nt(q, k), NEG_BIG)
                m = jnp.max(s, axis=-1, keepdims=True)
                p = jnp.exp2((s - m) * (scale * LOG2E))
                ov = jnp.dot(p.astype(BF16), jnp.concatenate([v, ones], axis=1),
                             preferred_element_type=F32)
                l = ov[:, HEAD_DIM:]
                o_scr[rr * nsub + sub, h] = ov[:, :HEAD_DIM] * (1.0 / l)
                lse_tile = jnp.where(lane == h, m * scale + jnp.log(l), lse_tile)
            lse_scr[rr * nsub + sub] = lse_tile
        last = slice((nsub - 1) * bq, nsub * bq)
        kprev_scr[r] = k_ref[rr, last, :]
        vprev_scr[r] = v_ref[rr, last, :]

    def write_out(first_residue):
        for rr in range(nres):
            for sub in range(nsub):
                start = dil * sub * bq + first_residue + rr
                dst = pl.ds(start, bq, stride=dil) if dil > 1 else pl.ds(start, bq)
                for h in range(N_HEADS):
                    o_ref[h, dst, :] = o_scr[rr * nsub + sub, h]
                lse_ref[dst, :] = lse_scr[rr * nsub + sub]

    if dil == nres:
        write_out(0)
    else:
        for g in range(dil // nres):
            pl.when(rg == g)(functools.partial(write_out, g * nres))


def _band_attn(qkv, dil):
    rows = qkv.shape[1]
    t = rows * dil
    bq = BAND
    nres = min(dil, ATTN_BLOCKS_PER_STEP)
    nsub = min(ATTN_BLOCKS_PER_STEP // nres, rows // bq)
    assert qkv.shape[0] == dil and rows % (nsub * bq) == 0 and dil % nres == 0
    blk = (nres, nsub * bq, GROUP_WIDTH)
    cur = lambda g: (lambda i, rg: (rg, i, g))
    carry = pltpu.VMEM((dil, bq, GROUP_WIDTH), qkv.dtype)
    return pl.pallas_call(
        functools.partial(_band_attn_kernel, dil, nres, nsub),
        grid=(rows // (nsub * bq), dil // nres),
        in_specs=[pl.BlockSpec(blk, cur(g)) for g in range(3)],
        out_specs=[
            pl.BlockSpec((N_HEADS, dil * nsub * bq, HEAD_DIM), lambda i, rg: (0, i, 0)),
            pl.BlockSpec((dil * nsub * bq, HEAD_DIM), lambda i, rg: (i, 0)),
        ],
        out_shape=[
            jax.ShapeDtypeStruct((N_HEADS, t, HEAD_DIM), F32),
            jax.ShapeDtypeStruct((t, HEAD_DIM), F32),
        ],
        scratch_shapes=[pltpu.VMEM((nres * nsub, N_HEADS, bq, HEAD_DIM), F32),
                        pltpu.VMEM((nres * nsub, bq, HEAD_DIM), F32), carry, carry],
        compiler_params=_params("arbitrary", "arbitrary"),
        name=f"band_attn_d{dil}",
    )(qkv, qkv, qkv)


def _sample_attn_kernel(q_ref, kn_ref, vn_ref, *rest):
    n_pat = len(PATTERNS)
    cache_refs = rest[: 2 * n_pat]
    o_ref = rest[2 * n_pat]
    scale = HEAD_DIM ** -0.5
    for b in range(q_ref.shape[0]):
        q = q_ref[b]
        kn = kn_ref[b]
        vn = vn_ref[b]
        s_new = jnp.sum(q * kn, axis=-1, keepdims=True) * scale
        outs, lses = [], []
        for p in range(n_pat):
            kc = cache_refs[2 * p][b]
            vc = cache_refs[2 * p + 1][b]
            s = jnp.sum(kc * q[None], axis=-1, keepdims=True) * scale
            m = jnp.maximum(jnp.max(s, axis=0), s_new)
            e = jnp.exp(s - m[None])
            e_new = jnp.exp(s_new - m)
            l = jnp.sum(e, axis=0) + e_new
            o = jnp.sum(e * vc, axis=0) + e_new * vn
            outs.append(o * (1.0 / l))
            lses.append(m + jnp.log(l))
        top = functools.reduce(jnp.maximum, lses)
        ws = [jnp.exp(x - top) for x in lses]
        den = functools.reduce(lambda a, b: a + b, ws)
        o_ref[b] = functools.reduce(lambda a, b: a + b, [w * o for w, o in zip(ws, outs)]) * (1.0 / den)


def _sample_attn(proj_s, cache_k, cache_v):
    b = proj_s.shape[0]
    past = cache_k.shape[1]
    assert past == max(w for w, _ in PATTERNS), "every dilated key must lie inside the cached window"
    nb = _row_tile(b, SAMPLES_PER_STEP)
    head_blk = (nb, N_HEADS, HEAD_DIM)
    in_specs = [pl.BlockSpec(head_blk, lambda i: (i, 0, 0)) for _ in range(3)]
    args = [proj_s[:, g * GROUP_WIDTH:(g + 1) * GROUP_WIDTH].reshape(b, N_HEADS, HEAD_DIM)
            for g in range(3)]
    for _, dil in PATTERNS:
        rows = past // dil
        last_blk = rows // BAND - 1
        for c in (cache_k, cache_v):
            args.append(c.reshape(b, rows, dil, N_HEADS, HEAD_DIM))
            in_specs.append(pl.BlockSpec((nb, BAND, None, N_HEADS, HEAD_DIM),
                                         lambda i, lb=last_blk: (i, lb, 0, 0, 0)))
    out = pl.pallas_call(
        _sample_attn_kernel,
        grid=(b // nb,),
        in_specs=in_specs,
        out_specs=pl.BlockSpec(head_blk, lambda i: (i, 0, 0)),
        out_shape=jax.ShapeDtypeStruct((b, N_HEADS, HEAD_DIM), F32),
        compiler_params=_params("parallel"),
        name="sample_attn",
    )(*args)
    return out.reshape(b, GROUP_WIDTH)


def _lower_bound(logits, layer):
    top = jnp.max(logits, axis=0, keepdims=True)
    e = jnp.exp(logits - top)
    return jnp.sum(e[: layer + 1], axis=0, keepdims=True) / jnp.sum(e, axis=0, keepdims=True)


def _split3(x):
    a = x.astype(BF16)
    r = x - a.astype(F32)
    b = r.astype(BF16)
    c = (r - b.astype(F32)).astype(BF16)
    return a, b, c


def _pair_weights(c, q, k, level_of_pair):
    n = len(c)
    sub = lax.broadcasted_iota(jnp.int32, c[0].shape, 0)
    n_levels = (n * SUBLANES).bit_length() - 1
    own = jnp.sum(jnp.concatenate(q, axis=0) * jnp.concatenate(k, axis=0), axis=-1, keepdims=True)
    a = jnp.where(level_of_pair == n_levels, own, 0.0)
    odd = (sub & 1) != 0
    prev = jnp.concatenate(
        [jnp.sum(q[i] * pltpu.roll(k[i], 1, 0)
                 * jnp.exp2(jnp.where(odd, c[i] - pltpu.roll(c[i], 1, 0), 0.0)), axis=-1, keepdims=True)
         for i in range(n)], axis=0)
    a = jnp.where(level_of_pair == 0, prev, a)
    for lev in range(1, n_levels):
        half = 1 << lev
        xs = []
        for i in range(n):
            if half >= SUBLANES:
                nv = 2 * half // SUBLANES
                first = (i // nv) * nv
                upper = (i % nv) >= nv // 2
                ref = c[first + nv // 2 - 1][SUBLANES - 1:SUBLANES]
                arg = c[i] - ref if upper else ref - c[i]
                base = q[i] if upper else k[i]
            else:
                upper = (sub & half) != 0
                if half == SUBLANES // 2:
                    ref = c[i][half - 1:half]
                else:
                    z = jnp.where(odd, c[i], pltpu.roll(c[i], SUBLANES - 1, 0))
                    ref = jnp.where(upper, pltpu.roll(z, 2, 0), z)
                diff = c[i] - ref
                arg = jnp.where(upper, diff, -diff)
                base = jnp.where(upper, q[i], k[i])
            xs.append(base * jnp.exp2(arg))
        x = jnp.concatenate(xs, axis=0).astype(BF16)
        a = jnp.where(level_of_pair == lev, _dot_nt(x, x), a)
    return a


def _gated_norm(o, gain, gate):
    return (_rms(o, gain) * (gate * _sigmoid(gate))).astype(BF16)


def _hgrn_kernel(layer, q_ref, f_ref, v_ref, gate_ref, logit_ref, gain_ref, tri_ref, lv_ref,
                 o_ref, state_ref, st_scr, qq_scr, kk_scr, c2_scr, qt_scr, kd_scr):
    step = pl.program_id(0)
    rows = q_ref.shape[0]

    @pl.when(step == 0)
    def _():
        st_scr[...] = jnp.zeros_like(st_scr)

    lb = _lower_bound(logit_ref[...], layer)
    f = lb + (1.0 - lb) * _sigmoid(f_ref[...])
    g = jnp.log(f)
    qq = q_ref[...] * _sigmoid(q_ref[...])
    kk = 1.0 - f
    tri = tri_ref[...]
    pieces = _split3(g)
    sums = [functools.reduce(
        lambda a, b: a + b,
        [jnp.dot(tri, piece[c:c + HG_BLOCK], preferred_element_type=F32) for piece in pieces])
        for c in range(0, rows, HG_BLOCK)]
    cum = jnp.concatenate([s[:HG_BLOCK] for s in sums], axis=0)
    rev = jnp.concatenate([s[HG_BLOCK:] for s in sums], axis=0)
    c2 = cum * LOG2E
    c2_scr[...] = c2
    qq_scr[...] = qq
    kk_scr[...] = kk
    qt_scr[...] = qq * jnp.exp2(c2)
    kd_scr[...] = kk * jnp.exp(rev)

    n_groups = HG_BLOCK // SUBLANES

    def block(b, carry):
        base = pl.multiple_of(b * HG_BLOCK, HG_BLOCK)
        level_of_pair = lv_ref[...]
        for h in range(N_HEADS):
            sl = slice(h * HEAD_DIM, (h + 1) * HEAD_DIM)
            groups = lambda ref: [ref[pl.ds(base + SUBLANES * i, SUBLANES), sl] for i in range(n_groups)]
            c_b = groups(c2_scr)
            st = st_scr[h]
            inter = _dot_nt(qt_scr[pl.ds(base, HG_BLOCK), sl].astype(BF16), st.astype(BF16))
            pair = _pair_weights(c_b, groups(qq_scr), groups(kk_scr), level_of_pair)
            v_b = v_ref[pl.ds(base, HG_BLOCK), sl].astype(BF16)
            o = inter + jnp.dot(pair.astype(BF16), v_b, preferred_element_type=F32)
            o_ref[pl.ds(base, HG_BLOCK), sl] = _gated_norm(
                o, gain_ref[...], gate_ref[pl.ds(base, HG_BLOCK), sl])
            decay = jnp.exp2(c_b[-1][SUBLANES - 1:SUBLANES])
            delta = _dot_tn(v_b, kd_scr[pl.ds(base, HG_BLOCK), sl].astype(BF16))
            st_scr[h] = st * decay + delta
        return carry

    for b in range(rows // HG_BLOCK):
        block(b, 0)

    @pl.when(step == pl.num_programs(0) - 1)
    def _():
        for h in range(N_HEADS):
            state_ref[h] = st_scr[h].T


def _chunk_tri():
    idx = np.arange(HG_BLOCK)
    lower = idx[None, :] <= idx[:, None]
    return jnp.asarray(np.concatenate([lower, ~lower], axis=0), dtype=BF16)


def _pair_levels():
    idx = np.arange(HG_BLOCK)
    t, s = idx[:, None], idx[None, :]
    n_levels = HG_BLOCK.bit_length() - 1
    level = np.floor(np.log2(np.maximum(t ^ s, 1))).astype(np.int32)
    return jnp.asarray(np.where(t == s, n_levels, np.where(s < t, level, -1)).astype(np.int32))


def _hgrn_prompt(proj, logits, layer, hg_gain, rows):
    t = proj.shape[0]
    assert t % rows == 0 and rows % HG_BLOCK == 0
    col = lambda g: pl.BlockSpec((rows, GROUP_WIDTH), lambda i, g=g: (i, g))
    n_l = logits.shape[0]
    return pl.pallas_call(
        functools.partial(_hgrn_kernel, layer),
        grid=(t // rows,),
        in_specs=[
            col(0), col(1), col(2), col(3),
            pl.BlockSpec((n_l, GROUP_WIDTH), lambda i: (0, 0)),
            pl.BlockSpec((1, HEAD_DIM), lambda i: (0, 0)),
            pl.BlockSpec((2 * HG_BLOCK, HG_BLOCK), lambda i: (0, 0)),
            pl.BlockSpec((HG_BLOCK, HG_BLOCK), lambda i: (0, 0)),
        ],
        out_specs=[
            pl.BlockSpec((rows, GROUP_WIDTH), lambda i: (i, 0)),
            pl.BlockSpec((N_HEADS, HEAD_DIM, HEAD_DIM), lambda i: (0, 0, 0)),
        ],
        out_shape=[
            jax.ShapeDtypeStruct((t, GROUP_WIDTH), BF16),
            jax.ShapeDtypeStruct((N_HEADS, HEAD_DIM, HEAD_DIM), F32),
        ],
        scratch_shapes=[pltpu.VMEM((N_HEADS, HEAD_DIM, HEAD_DIM), F32)]
        + [pltpu.VMEM((rows, GROUP_WIDTH), F32) for _ in range(5)],
        compiler_params=_params("arbitrary"),
        name="hgrn_prompt",
    )(proj, proj, proj, proj, logits, hg_gain.reshape(1, HEAD_DIM), _chunk_tri(), _pair_levels())


def _hgrn_step_kernel(layer, q_ref, f_ref, v_ref, gate_ref, logit_ref, gain_ref, s_ref, o_ref, snew_ref,
                      o_scr, gate_scr):
    lb = _lower_bound(logit_ref[...], layer)
    nb = q_ref.shape[0]
    eye = (lax.broadcasted_iota(jnp.int32, (HEAD_DIM, HEAD_DIM), 0)
           == lax.broadcasted_iota(jnp.int32, (HEAD_DIM, HEAD_DIM), 1))

    def column(row):
        return jnp.sum(jnp.where(eye, jnp.broadcast_to(row, (HEAD_DIM, HEAD_DIM)), 0.0),
                       axis=-1, keepdims=True)

    for b in range(nb):
        f = lb + (1.0 - lb) * _sigmoid(f_ref[b])
        qq = q_ref[b] * _sigmoid(q_ref[b])
        vv = v_ref[b]
        gate_scr[b:b + 1, :] = gate_ref[b]
        for h in range(N_HEADS):
            sl = slice(h * HEAD_DIM, (h + 1) * HEAD_DIM)
            f_col = column(f[:, sl])
            q_col = column(qq[:, sl])
            s_new = f_col * s_ref[b, h] + (1.0 - f_col) * vv[:, sl]
            snew_ref[b, h] = s_new
            o_scr[b:b + 1, sl] = jnp.sum(q_col * s_new, axis=0, keepdims=True)
    for h in range(N_HEADS):
        sl = slice(h * HEAD_DIM, (h + 1) * HEAD_DIM)
        res = _gated_norm(o_scr[:, sl], gain_ref[...], gate_scr[:, sl]).astype(F32)
        for b in range(nb):
            o_ref[b, :, sl] = res[b:b + 1]


def _hgrn_step(proj_s, logits, layer, hg_gain, state):
    b = proj_s.shape[0]
    nb = _row_tile(b, SAMPLES_PER_STEP)
    p3 = proj_s.reshape(b, 1, proj_s.shape[1])
    row = (nb, 1, GROUP_WIDTH)
    n_l = logits.shape[0]
    st_blk = (nb, N_HEADS, HEAD_DIM, HEAD_DIM)
    o, s_new = pl.pallas_call(
        functools.partial(_hgrn_step_kernel, layer),
        grid=(b // nb,),
        in_specs=[pl.BlockSpec(row, lambda i, g=g: (i, 0, g)) for g in (0, 1, 2, 3)]
        + [pl.BlockSpec((n_l, GROUP_WIDTH), lambda i: (0, 0)),
           pl.BlockSpec((1, HEAD_DIM), lambda i: (0, 0)),
           pl.BlockSpec(st_blk, lambda i: (i, 0, 0, 0))],
        out_specs=[pl.BlockSpec(row, lambda i: (i, 0, 0)),
                   pl.BlockSpec(st_blk, lambda i: (i, 0, 0, 0))],
        out_shape=[jax.ShapeDtypeStruct((b, 1, GROUP_WIDTH), F32),
                   jax.ShapeDtypeStruct(state.shape, F32)],
        scratch_shapes=[pltpu.VMEM((nb, GROUP_WIDTH), F32), pltpu.VMEM((nb, GROUP_WIDTH), F32)],
        compiler_params=_params("parallel"),
        name="hgrn_step",
    )(p3, p3, p3, p3, logits, hg_gain.reshape(1, HEAD_DIM), state)
    return o.reshape(b, GROUP_WIDTH), s_new


def _merge_kernel(n_pat, *refs):
    o_refs = refs[:n_pat]
    lse_refs = refs[n_pat:2 * n_pat] if n_pat > 1 else ()
    k = 2 * n_pat if n_pat > 1 else n_pat
    (hg_ref, x_ref, again_ref, wout_ref, npost_ref, npre_ref,
     x1_ref, h2_ref, buf_a, buf_b) = refs[k:]
    tm = x_ref.shape[0]
    i = pl.program_id(0)

    def merge_stage(dst):
        if n_pat > 1:
            lses = [r[...] for r in lse_refs]
            top = functools.reduce(jnp.maximum, lses)
            ws = [jnp.exp(x - top) for x in lses]
            inv = 1.0 / functools.reduce(lambda a, b: a + b, ws)
            ws = [w * inv for w in ws]
        att = []
        sq_lanes = jnp.zeros((tm, HEAD_DIM), F32)
        for h in range(N_HEADS):
            sl = slice(h * HEAD_DIM, (h + 1) * HEAD_DIM)
            if n_pat > 1:
                a = functools.reduce(
                    lambda x, y: x + y,
                    [w[:, h:h + 1] * r[h] for w, r in zip(ws, o_refs)])
            else:
                a = o_refs[0][:, sl]
            att.append(a)
            sq_lanes = sq_lanes + a * a
        sq = jnp.sum(sq_lanes, axis=-1, keepdims=True)
        dst[:, GROUP_WIDTH:] = hg_ref[...].astype(BF16)
        inv_rms = lax.rsqrt(sq * (1.0 / GROUP_WIDTH) + EPS)
        for h, a in enumerate(att):
            sl = slice(h * HEAD_DIM, (h + 1) * HEAD_DIM)
            dst[:, sl] = (a * inv_rms * again_ref[:, sl]).astype(BF16)

    def project_stage(src):
        mix = jnp.dot(src[...], wout_ref[...], preferred_element_type=F32)
        x1 = x_ref[...] + _rms(mix, npost_ref[...])
        x1_ref[...] = x1
        h2_ref[...] = _rms(x1, npre_ref[...]).astype(BF16)

    @pl.when(i == 0)
    def _():
        buf_b[...] = jnp.zeros_like(buf_b)

    @pl.when(i % 2 == 0)
    def _():
        project_stage(buf_b)
        merge_stage(buf_a)

    @pl.when(i % 2 == 1)
    def _():
        project_stage(buf_a)
        merge_stage(buf_b)


def _merge(o_list, lse_list, o_hg, x, attn_gain, w_out_bf16, n_post, n_pre, tm):
    t, d = x.shape
    n_pat = len(o_list)
    n = t // tm
    cur = lambda i: jnp.minimum(i, n - 1)
    prev = lambda i: jnp.maximum(i - 1, 0)
    row_cur = lambda w: pl.BlockSpec((tm, w), lambda i: (cur(i), 0))
    row_prev = lambda w: pl.BlockSpec((tm, w), lambda i: (prev(i), 0))
    full = lambda a: pl.BlockSpec(a.shape, lambda i: (0,) * a.ndim, pipeline_mode=pl.Buffered(1))
    args = list(o_list)
    if n_pat > 1:
        in_specs = [pl.BlockSpec((N_HEADS, tm, HEAD_DIM), lambda i: (0, cur(i), 0)) for _ in o_list]
        args += list(lse_list)
        in_specs += [row_cur(HEAD_DIM) for _ in lse_list]
    else:
        in_specs = [row_cur(GROUP_WIDTH)]
    small = [attn_gain.reshape(1, -1), w_out_bf16, n_post.reshape(1, -1), n_pre.reshape(1, -1)]
    args += [o_hg, x] + small
    in_specs += [row_cur(GROUP_WIDTH), row_prev(d)]
    in_specs += [full(a) for a in small]
    return pl.pallas_call(
        functools.partial(_merge_kernel, n_pat),
        grid=(n + 1,),
        in_specs=in_specs,
        out_specs=[row_prev(d), row_prev(d)],
        out_shape=[jax.ShapeDtypeStruct((t, d), F32), jax.ShapeDtypeStruct((t, d), BF16)],
        scratch_shapes=[pltpu.VMEM((tm, 2 * GROUP_WIDTH), BF16) for _ in range(2)],
        compiler_params=_params("arbitrary"),
        name="merge_outproj",
    )(*args)


def _ffn_kernel(h_ref, x1_ref, wg_ref, wu_ref, wd_ref, npost_ref, y_ref):
    j = pl.program_id(1)
    last = pl.num_programs(1) - 1
    tm = h_ref.shape[0]
    tf = wg_ref.shape[1]
    d = wd_ref.shape[1]

    def step(first, final):
        h = h_ref[...]
        acts = []
        for c in range(0, tf, FFN_CHUNK):
            gate = jnp.dot(h, wg_ref[:, c:c + FFN_CHUNK], preferred_element_type=F32)
            up = jnp.dot(h, wu_ref[:, c:c + FFN_CHUNK], preferred_element_type=F32)
            acts.append((gate * _sigmoid(gate) * up).astype(BF16))
        act = jnp.concatenate(acts, axis=-1)
        sq_lanes = jnp.zeros((tm, HEAD_DIM), F32)
        for c in range(0, d, FFN_CHUNK):
            cols = slice(c, c + FFN_CHUNK)
            total = jnp.dot(act, wd_ref[:, cols], preferred_element_type=F32)
            if not first:
                total = total + y_ref[:, cols]
            y_ref[:, cols] = total
            if final:
                for l0 in range(0, FFN_CHUNK, HEAD_DIM):
                    sq_lanes = sq_lanes + total[:, l0:l0 + HEAD_DIM] * total[:, l0:l0 + HEAD_DIM]
        if final:
            sq = jnp.sum(sq_lanes, axis=-1, keepdims=True)
            inv = lax.rsqrt(sq * (1.0 / d) + EPS)
            for c in range(0, d, FFN_CHUNK):
                cols = slice(c, c + FFN_CHUNK)
                y_ref[:, cols] = x1_ref[:, cols] + y_ref[:, cols] * inv * npost_ref[:, cols]

    pl.when(j == 0)(functools.partial(step, True, False))
    pl.when((j > 0) & (j < last))(functools.partial(step, False, False))
    pl.when(j == last)(functools.partial(step, False, True))


def _ffn(h2, x1, wg, wu, wd, n_post, tm, tf):
    t, d = x1.shape
    dff = wg.shape[1]
    assert dff % tf == 0 and dff // tf >= 2
    return pl.pallas_call(
        _ffn_kernel,
        grid=(t // tm, dff // tf),
        in_specs=[
            pl.BlockSpec((tm, d), lambda i, j: (i, 0)),
            pl.BlockSpec((tm, d), lambda i, j: (i, 0)),
            pl.BlockSpec((d, tf), lambda i, j: (0, j)),
            pl.BlockSpec((d, tf), lambda i, j: (0, j)),
            pl.BlockSpec((tf, d), lambda i, j: (j, 0)),
            pl.BlockSpec((1, d), lambda i, j: (0, 0)),
        ],
        out_specs=pl.BlockSpec((tm, d), lambda i, j: (i, 0)),
        out_shape=jax.ShapeDtypeStruct((t, d), F32),
        compiler_params=_params("parallel", "arbitrary"),
        name="ffn",
    )(h2, x1, wg, wu, wd, n_post.reshape(1, d))


def _row_tile(t, want):
    tm = min(t, want)
    assert t % tm == 0
    return tm


INPROJ_ROWS, INPROJ_COLS = 1024, 512
HGRN_ROWS = 512
MERGE_ROWS = 512
FFN_ROWS, FFN_COLS = 1024, 256
SAMPLE_FFN_COLS = 512
SAMPLES_PER_STEP = 4
ATTN_BLOCKS_PER_STEP = 8


def kernel(x_prompt, x_sample, cache_win_k, cache_win_v, state_hgrn, norm_pre_mix, w_in,
           hg_lb_logits, attn_out_gain, hg_norm_gain, w_out, norm_post_mix, norm_pre_ffn,
           w_gate, w_up, w_down, norm_post_ffn):
    depth = w_in.shape[0]
    bp, t, d = x_prompt.shape
    bs, ts, _ = x_sample.shape
    assert bp == 1 and ts == 1
    keep = min(max(w for w, _ in PATTERNS), t)
    logits = hg_lb_logits.astype(F32)

    yp = x_prompt.reshape(t, d)
    ys = x_sample.reshape(bs, d)
    outs = [[] for _ in range(6)]
    for l in range(depth):
        w_in_b, w_out_b = w_in[l].astype(BF16), w_out[l].astype(BF16)
        wg_b, wu_b, wd_b = w_gate[l].astype(BF16), w_up[l].astype(BF16), w_down[l].astype(BF16)

        dils = tuple(dil for _, dil in PATTERNS)
        hg_cols, proj, *dec = _inproj(yp, norm_pre_mix[l], w_in_b, _row_tile(t, INPROJ_ROWS),
                                      INPROJ_COLS, dils, keep)
        o_list, lse_list = zip(*[_band_attn(qkv, dil) for qkv, dil in zip(dec, dils)])
        o_hg, s_fin = _hgrn_prompt(hg_cols, logits, l, hg_norm_gain[l], _row_tile(t, HGRN_ROWS))
        x1, h2 = _merge(o_list, lse_list, o_hg, yp, attn_out_gain[l],
                        w_out_b, norm_post_mix[l], norm_pre_ffn[l], _row_tile(t, MERGE_ROWS))
        yp = _ffn(h2, x1, wg_b, wu_b, wd_b, norm_post_ffn[l], _row_tile(t, FFN_ROWS), FFN_COLS)
        outs[0].append(proj[:, :GROUP_WIDTH].reshape(1, keep, N_HEADS, HEAD_DIM))
        outs[1].append(proj[:, GROUP_WIDTH:].reshape(1, keep, N_HEADS, HEAD_DIM))
        outs[2].append(s_fin.reshape(1, N_HEADS, HEAD_DIM, HEAD_DIM))

        (proj_s,) = _inproj(ys, norm_pre_mix[l], w_in_b, bs, 2 * INPROJ_COLS)
        hg_cols_s = proj_s[:, 3 * GROUP_WIDTH:]
        o_att_s = _sample_attn(proj_s, cache_win_k[l], cache_win_v[l])
        o_hg_s, s_new = _hgrn_step(hg_cols_s, logits, l, hg_norm_gain[l], state_hgrn[l])
        x1s, h2s = _merge([o_att_s], [], o_hg_s, ys, attn_out_gain[l],
                          w_out_b, norm_post_mix[l], norm_pre_ffn[l], bs)
        ys = _ffn(h2s, x1s, wg_b, wu_b, wd_b, norm_post_ffn[l], bs, SAMPLE_FFN_COLS)
        outs[3].append(proj_s[:, GROUP_WIDTH:2 * GROUP_WIDTH].reshape(bs, 1, N_HEADS, HEAD_DIM))
        outs[4].append(proj_s[:, 2 * GROUP_WIDTH:3 * GROUP_WIDTH].reshape(bs, 1, N_HEADS, HEAD_DIM))
        outs[5].append(s_new)

    st = lambda xs: jnp.stack(xs)
    return (yp.reshape(1, t, d), ys.reshape(bs, 1, d), st(outs[0]), st(outs[1]), st(outs[2]),
            st(outs[3]), st(outs[4]), st(outs[5]))
```

```python
import functools

import jax
import jax.numpy as jnp
import numpy as np
from jax import lax
from jax.experimental import pallas as pl
from jax.experimental.pallas import tpu as pltpu

F32 = jnp.float32
BF16 = jnp.bfloat16

EPS = 1e-6
HEAD_DIM = 128
N_HEADS = 8
GROUP_WIDTH = N_HEADS * HEAD_DIM
N_PROJ_GROUPS = 7
PATTERNS = ((128, 1), (512, 4), (2048, 16))
BAND = 128
NEG_BIG = -1e30
LOG2E = 1.4426950408889634
HG_BLOCK = 128
SUBLANES = 8
MXU_COLS = 256
X_PARTS = 4
FFN_CHUNK = MXU_COLS
VMEM_LIMIT = 56 * 1024 * 1024


def _params(*sem):
    return pltpu.CompilerParams(dimension_semantics=sem, vmem_limit_bytes=VMEM_LIMIT)


def _rms(x, gain):
    ms = jnp.mean(x * x, axis=-1, keepdims=True)
    return x * lax.rsqrt(ms + EPS) * gain


def _sigmoid(x):
    return 0.5 * jnp.tanh(0.5 * x) + 0.5


def _dot_nt(a, b):
    return lax.dot_general(a, b, (((1,), (1,)), ((), ())), preferred_element_type=F32)


def _dot_tn(a, b):
    return lax.dot_general(a, b, (((0,), (0,)), ((), ())), preferred_element_type=F32)


def _inproj_kernel(dils, n_parts, *refs):
    x_parts, (g_ref, w_ref, o_ref), rest = refs[:n_parts], refs[n_parts:n_parts + 3], refs[n_parts + 3:]
    if dils:
        kv_ref, rest = rest[0], rest[1:]
    dec_refs, h_scr = rest[:len(dils)], rest[len(dils)]
    j = pl.program_id(1)
    tm, part = x_parts[0].shape

    @pl.when(j == 0)
    def _():
        sq = functools.reduce(lambda a, b: a + b,
                              [jnp.sum(p[...] * p[...], axis=-1, keepdims=True) for p in x_parts])
        inv = lax.rsqrt(sq * (1.0 / (n_parts * part)) + EPS)
        for q, p in enumerate(x_parts):
            cols = slice(q * part, (q + 1) * part)
            h_scr[:, cols] = (p[...] * inv * g_ref[:, cols]).astype(BF16)

    if not dils:
        o_ref[...] = jnp.dot(h_scr[...], w_ref[...], preferred_element_type=F32)
        return

    head_scr = rest[len(dils) + 1]
    plane_scrs = dict(zip(dils[1:-1], rest[len(dils) + 2:]))
    n_att = 3 * GROUP_WIDTH // w_ref.shape[1]

    @pl.when(j >= n_att)
    def _():
        o_ref[...] = jnp.dot(h_scr[...], w_ref[...], preferred_element_type=F32)

    @pl.when(j < n_att)
    def _():
        h = h_scr[...]
        for c in range(0, w_ref.shape[1], MXU_COLS):
            res = jnp.dot(h, w_ref[:, c:c + MXU_COLS], preferred_element_type=F32)
            kv_ref[:, c:c + MXU_COLS] = res
            for hh in range(c // HEAD_DIM, (c + MXU_COLS) // HEAD_DIM):
                sl = slice(hh * HEAD_DIM, (hh + 1) * HEAD_DIM)
                head_res = res[:, hh * HEAD_DIM - c:(hh + 1) * HEAD_DIM - c]
                head_scr[hh] = head_res
                dec_refs[0][0, :, sl] = head_res.astype(BF16)
                prev, planes = 1, {0: head_scr.at[hh]}
                for dil, dec_ref in zip(dils[1:], dec_refs[1:]):
                    ratio, new_planes = dil // prev, {}
                    for rp, src in planes.items():
                        for q in range(ratio):
                            r = rp + prev * q
                            val = src[pl.ds(q, tm // dil, stride=ratio), :]
                            dec_ref[r, :, sl] = val.astype(BF16)
                            if dil in plane_scrs:
                                plane_scrs[dil][hh * dil + r] = val
                                new_planes[r] = plane_scrs[dil].at[hh * dil + r]
                    prev, planes = dil, new_planes


def _inproj(x, gain, w_bf16, tm, tn, dils=(), keep=0):
    t, d = x.shape
    n = w_bf16.shape[1]
    assert t % tm == 0 and n % tn == 0
    if dils:
        n_att = 3 * GROUP_WIDTH // tn
        assert n == N_PROJ_GROUPS * GROUP_WIDTH and tn % MXU_COLS == 0 and GROUP_WIDTH % tn == 0
        assert dils[0] == 1 and all(b % a == 0 for a, b in zip(dils, dils[1:]))
        out_specs = [pl.BlockSpec((tm, tn), lambda i, j: (i, jnp.maximum(j - n_att, 0)))]
        out_shape = [jax.ShapeDtypeStruct((t, n - 3 * GROUP_WIDTH), F32)]
        assert keep % tm == 0 and 0 < keep <= t
        first_kept, k0, n_kv = (t - keep) // tm, GROUP_WIDTH // tn, 2 * GROUP_WIDTH // tn
        out_specs.append(pl.BlockSpec(
            (tm, tn),
            lambda i, j: (jnp.maximum(i - first_kept, 0),
                          jnp.where(i < first_kept, 0, jnp.clip(j - k0, 0, n_kv - 1)))))
        out_shape.append(jax.ShapeDtypeStruct((keep, 2 * GROUP_WIDTH), F32))
        for dil in dils:
            assert tm % (16 * dil) == 0
            out_specs.append(pl.BlockSpec((dil, tm // dil, tn),
                                          lambda i, j: (0, i, jnp.minimum(j, n_att - 1))))
            out_shape.append(jax.ShapeDtypeStruct((dil, t // dil, 3 * GROUP_WIDTH), BF16))
    else:
        out_specs = [pl.BlockSpec((tm, tn), lambda i, j: (i, j))]
        out_shape = [jax.ShapeDtypeStruct((t, n), F32)]
    n_i, n_j = t // tm, n // tn
    part = d // X_PARTS

    def x_spec(q):
        switch = max(min(n_j // 2 - 1 + q, n_j - 1), 1)
        return pl.BlockSpec(
            (tm, part),
            lambda i, j: (jnp.minimum(i + jnp.where(j >= switch, 1, 0), n_i - 1), q))

    return pl.pallas_call(
        functools.partial(_inproj_kernel, tuple(dils), X_PARTS),
        grid=(n_i, n_j),
        in_specs=[x_spec(q) for q in range(X_PARTS)] + [
            pl.BlockSpec((1, d), lambda i, j: (0, 0)),
            pl.BlockSpec((d, tn), lambda i, j: (0, j)),
        ],
        out_specs=out_specs,
        out_shape=out_shape,
        scratch_shapes=[pltpu.VMEM((tm, d), BF16)]
        + ([pltpu.VMEM((tn // HEAD_DIM, tm, HEAD_DIM), F32)] if dils else [])
        + [pltpu.VMEM((tn // HEAD_DIM * dil, tm // dil, HEAD_DIM), F32) for dil in dils[1:-1]],
        compiler_params=_params("arbitrary", "arbitrary"),
        name="inproj",
    )(*([x] * X_PARTS), gain.reshape(1, d), w_bf16)


def _band_attn_kernel(dil, nres, nsub, q_ref, k_ref, v_ref, o_ref, lse_ref,
                      o_scr, lse_scr, kprev_scr, vprev_scr):
    i = pl.program_id(0)
    rg = pl.program_id(1)
    bq = BAND
    row = lax.broadcasted_iota(jnp.int32, (bq, 2 * bq), 0)
    col = lax.broadcasted_iota(jnp.int32, (bq, 2 * bq), 1)
    dist = row + bq - col
    in_band = (dist >= 0) & (dist <= BAND)
    first_valid = in_band & (col >= jnp.where(i == 0, bq, 0))
    lane = lax.broadcasted_iota(jnp.int32, (bq, HEAD_DIM), 1)
    ones = jnp.ones((2 * bq, HEAD_DIM), BF16)
    scale = HEAD_DIM ** -0.5
    for rr in range(nres):
        r = rg * nres + rr

        @pl.when(i == 0)
        def _():
            kprev_scr[r] = jnp.zeros(kprev_scr.shape[1:], kprev_scr.dtype)
            vprev_scr[r] = jnp.zeros(vprev_scr.shape[1:], vprev_scr.dtype)

        for sub in range(nsub):
            rows = slice(sub * bq, (sub + 1) * bq)
            before = slice((sub - 1) * bq, sub * bq)
            valid = first_valid if sub == 0 else in_band
            lse_tile = jnp.zeros((bq, HEAD_DIM), F32)
            for h in range(N_HEADS):
                sl = slice(h * HEAD_DIM, (h + 1) * HEAD_DIM)
                q = q_ref[rr, rows, sl]
                k_prev = kprev_scr[r, :, sl] if sub == 0 else k_ref[rr, before, sl]
                v_prev = vprev_scr[r, :, sl] if sub == 0 else v_ref[rr, before, sl]
                k = jnp.concatenate([k_prev, k_ref[rr, rows, sl]], axis=0)
                v = jnp.concatenate([v_prev, v_ref[rr, rows, sl]], axis=0)
                s = jnp.where(valid, _dot_nt(q, k), NEG_BIG)
                m = jnp.max(s, axis=-1, keepdims=True)
                p = jnp.exp2((s - m) * (scale * LOG2E))
                ov = jnp.dot(p.astype(BF16), jnp.concatenate([v, ones], axis=1),
                             preferred_element_type=F32)
                l = ov[:, HEAD_DIM:]
                o_scr[rr * nsub + sub, h] = ov[:, :HEAD_DIM] * (1.0 / l)
                lse_tile = jnp.where(lane == h, m * scale + jnp.log(l), lse_tile)
            lse_scr[rr * nsub + sub] = lse_tile
        last = slice((nsub - 1) * bq, nsub * bq)
        kprev_scr[r] = k_ref[rr, last, :]
        vprev_scr[r] = v_ref[rr, last, :]

    def write_out(first_residue):
        for rr in range(nres):
            for sub in range(nsub):
                start = dil * sub * bq + first_residue + rr
                dst = pl.ds(start, bq, stride=dil) if dil > 1 else pl.ds(start, bq)
                for h in range(N_HEADS):
                    o_ref[h, dst, :] = o_scr[rr * nsub + sub, h]
                lse_ref[dst, :] = lse_scr[rr * nsub + sub]

    if dil == nres:
        write_out(0)
    else:
        for g in range(dil // nres):
            pl.when(rg == g)(functools.partial(write_out, g * nres))


def _band_attn(qkv, dil):
    rows = qkv.shape[1]
    t = rows * dil
    bq = BAND
    nres = min(dil, ATTN_BLOCKS_PER_STEP)
    nsub = min(ATTN_BLOCKS_PER_STEP // nres, rows // bq)
    assert qkv.shape[0] == dil and rows % (nsub * bq) == 0 and dil % nres == 0
    blk = (nres, nsub * bq, GROUP_WIDTH)
    cur = lambda g: (lambda i, rg: (rg, i, g))
    carry = pltpu.VMEM((dil, bq, GROUP_WIDTH), qkv.dtype)
    return pl.pallas_call(
        functools.partial(_band_attn_kernel, dil, nres, nsub),
        grid=(rows // (nsub * bq), dil // nres),
        in_specs=[pl.BlockSpec(blk, cur(g)) for g in range(3)],
        out_specs=[
            pl.BlockSpec((N_HEADS, dil * nsub * bq, HEAD_DIM), lambda i, rg: (0, i, 0)),
            pl.BlockSpec((dil * nsub * bq, HEAD_DIM), lambda i, rg: (i, 0)),
        ],
        out_shape=[
            jax.ShapeDtypeStruct((N_HEADS, t, HEAD_DIM), F32),
            jax.ShapeDtypeStruct((t, HEAD_DIM), F32),
        ],
        scratch_shapes=[pltpu.VMEM((nres * nsub, N_HEADS, bq, HEAD_DIM), F32),
                        pltpu.VMEM((nres * nsub, bq, HEAD_DIM), F32), carry, carry],
        compiler_params=_params("arbitrary", "arbitrary"),
        name=f"band_attn_d{dil}",
    )(qkv, qkv, qkv)


def _sample_attn_kernel(q_ref, kn_ref, vn_ref, *rest):
    n_pat = len(PATTERNS)
    cache_refs = rest[: 2 * n_pat]
    o_ref = rest[2 * n_pat]
    scale = HEAD_DIM ** -0.5
    for b in range(q_ref.shape[0]):
        q = q_ref[b]
        kn = kn_ref[b]
        vn = vn_ref[b]
        s_new = jnp.sum(q * kn, axis=-1, keepdims=True) * scale
        outs, lses = [], []
        for p in range(n_pat):
            kc = cache_refs[2 * p][b]
            vc = cache_refs[2 * p + 1][b]
            s = jnp.sum(kc * q[None], axis=-1, keepdims=True) * scale
            m = jnp.maximum(jnp.max(s, axis=0), s_new)
            e = jnp.exp(s - m[None])
            e_new = jnp.exp(s_new - m)
            l = jnp.sum(e, axis=0) + e_new
            o = jnp.sum(e * vc, axis=0) + e_new * vn
            outs.append(o * (1.0 / l))
            lses.append(m + jnp.log(l))
        top = functools.reduce(jnp.maximum, lses)
        ws = [jnp.exp(x - top) for x in lses]
        den = functools.reduce(lambda a, b: a + b, ws)
        o_ref[b] = functools.reduce(lambda a, b: a + b, [w * o for w, o in zip(ws, outs)]) * (1.0 / den)


def _sample_attn(proj_s, cache_k, cache_v):
    b = proj_s.shape[0]
    past = cache_k.shape[1]
    assert past == max(w for w, _ in PATTERNS), "every dilated key must lie inside the cached window"
    nb = _row_tile(b, SAMPLES_PER_STEP)
    head_blk = (nb, N_HEADS, HEAD_DIM)
    in_specs = [pl.BlockSpec(head_blk, lambda i: (i, 0, 0)) for _ in range(3)]
    args = [proj_s[:, g * GROUP_WIDTH:(g + 1) * GROUP_WIDTH].reshape(b, N_HEADS, HEAD_DIM)
            for g in range(3)]
    for _, dil in PATTERNS:
        rows = past // dil
        last_blk = rows // BAND - 1
        for c in (cache_k, cache_v):
            args.append(c.reshape(b, rows, dil, N_HEADS, HEAD_DIM))
            in_specs.append(pl.BlockSpec((nb, BAND, None, N_HEADS, HEAD_DIM),
                                         lambda i, lb=last_blk: (i, lb, 0, 0, 0)))
    out = pl.pallas_call(
        _sample_attn_kernel,
        grid=(b // nb,),
        in_specs=in_specs,
        out_specs=pl.BlockSpec(head_blk, lambda i: (i, 0, 0)),
        out_shape=jax.ShapeDtypeStruct((b, N_HEADS, HEAD_DIM), F32),
        compiler_params=_params("parallel"),
        name="sample_attn",
    )(*args)
    return out.reshape(b, GROUP_WIDTH)


def _lower_bound(logits, layer):
    top = jnp.max(logits, axis=0, keepdims=True)
    e = jnp.exp(logits - top)
    return jnp.sum(e[: layer + 1], axis=0, keepdims=True) / jnp.sum(e, axis=0, keepdims=True)


def _split3(x):
    a = x.astype(BF16)
    r = x - a.astype(F32)
    b = r.astype(BF16)
    c = (r - b.astype(F32)).astype(BF16)
    return a, b, c


def _pair_weights(c, q, k, level_of_pair):
    n = len(c)
    sub = lax.broadcasted_iota(jnp.int32, c[0].shape, 0)
    n_levels = (n * SUBLANES).bit_length() - 1
    own = jnp.sum(jnp.concatenate(q, axis=0) * jnp.concatenate(k, axis=0), axis=-1, keepdims=True)
    a = jnp.where(level_of_pair == n_levels, own, 0.0)
    odd = (sub & 1) != 0
    prev = jnp.concatenate(
        [jnp.sum(q[i] * pltpu.roll(k[i], 1, 0)
                 * jnp.exp2(jnp.where(odd, c[i] - pltpu.roll(c[i], 1, 0), 0.0)), axis=-1, keepdims=True)
         for i in range(n)], axis=0)
    a = jnp.where(level_of_pair == 0, prev, a)
    for lev in range(1, n_levels):
        half = 1 << lev
        xs = []
        for i in range(n):
            if half >= SUBLANES:
                nv = 2 * half // SUBLANES
                first = (i // nv) * nv
                upper = (i % nv) >= nv // 2
                ref = c[first + nv // 2 - 1][SUBLANES - 1:SUBLANES]
                arg = c[i] - ref if upper else ref - c[i]
                base = q[i] if upper else k[i]
            else:
                upper = (sub & half) != 0
                if half == SUBLANES // 2:
                    ref = c[i][half - 1:half]
                else:
                    z = jnp.where(odd, c[i], pltpu.roll(c[i], SUBLANES - 1, 0))
                    ref = jnp.where(upper, pltpu.roll(z, 2, 0), z)
                diff = c[i] - ref
                arg = jnp.where(upper, diff, -diff)
                base = jnp.where(upper, q[i], k[i])
            xs.append(base * jnp.exp2(arg))
        x = jnp.concatenate(xs, axis=0).astype(BF16)
        a = jnp.where(level_of_pair == lev, _dot_nt(x, x), a)
    return a


def _gated_norm(o, gain, gate):
    return (_rms(o, gain) * (gate * _sigmoid(gate))).astype(BF16)


def _hgrn_kernel(layer, q_ref, f_ref, v_ref, gate_ref, logit_ref, gain_ref, tri_ref, lv_ref,
                 o_ref, state_ref, st_scr, qq_scr, kk_scr, c2_scr, qt_scr, kd_scr):
    step = pl.program_id(0)
    rows = q_ref.shape[0]

    @pl.when(step == 0)
    def _():
        st_scr[...] = jnp.zeros_like(st_scr)

    lb = _lower_bound(logit_ref[...], layer)
    f = lb + (1.0 - lb) * _sigmoid(f_ref[...])
    g = jnp.log(f)
    qq = q_ref[...] * _sigmoid(q_ref[...])
    kk = 1.0 - f
    tri = tri_ref[...]
    pieces = _split3(g)
    sums = [functools.reduce(
        lambda a, b: a + b,
        [jnp.dot(tri, piece[c:c + HG_BLOCK], preferred_element_type=F32) for piece in pieces])
        for c in range(0, rows, HG_BLOCK)]
    cum = jnp.concatenate([s[:HG_BLOCK] for s in sums], axis=0)
    rev = jnp.concatenate([s[HG_BLOCK:] for s in sums], axis=0)
    c2 = cum * LOG2E
    c2_scr[...] = c2
    qq_scr[...] = qq
    kk_scr[...] = kk
    qt_scr[...] = qq * jnp.exp2(c2)
    kd_scr[...] = kk * jnp.exp(rev)

    n_groups = HG_BLOCK // SUBLANES

    def block(b, carry):
        base = pl.multiple_of(b * HG_BLOCK, HG_BLOCK)
        level_of_pair = lv_ref[...]
        for h in range(N_HEADS):
            sl = slice(h * HEAD_DIM, (h + 1) * HEAD_DIM)
            groups = lambda ref: [ref[pl.ds(base + SUBLANES * i, SUBLANES), sl] for i in range(n_groups)]
            c_b = groups(c2_scr)
            st = st_scr[h]
            inter = _dot_nt(qt_scr[pl.ds(base, HG_BLOCK), sl].astype(BF16), st.astype(BF16))
            pair = _pair_weights(c_b, groups(qq_scr), groups(kk_scr), level_of_pair)
            v_b = v_ref[pl.ds(base, HG_BLOCK), sl].astype(BF16)
            o = inter + jnp.dot(pair.astype(BF16), v_b, preferred_element_type=F32)
            o_ref[pl.ds(base, HG_BLOCK), sl] = _gated_norm(
                o, gain_ref[...], gate_ref[pl.ds(base, HG_BLOCK), sl])
            decay = jnp.exp2(c_b[-1][SUBLANES - 1:SUBLANES])
            delta = _dot_tn(v_b, kd_scr[pl.ds(base, HG_BLOCK), sl].astype(BF16))
            st_scr[h] = st * decay + delta
        return carry

    for b in range(rows // HG_BLOCK):
        block(b, 0)

    @pl.when(step == pl.num_programs(0) - 1)
    def _():
        for h in range(N_HEADS):
            state_ref[h] = st_scr[h].T


def _chunk_tri():
    idx = np.arange(HG_BLOCK)
    lower = idx[None, :] <= idx[:, None]
    return jnp.asarray(np.concatenate([lower, ~lower], axis=0), dtype=BF16)


def _pair_levels():
    idx = np.arange(HG_BLOCK)
    t, s = idx[:, None], idx[None, :]
    n_levels = HG_BLOCK.bit_length() - 1
    level = np.floor(np.log2(np.maximum(t ^ s, 1))).astype(np.int32)
    return jnp.asarray(np.where(t == s, n_levels, np.where(s < t, level, -1)).astype(np.int32))


def _hgrn_prompt(proj, logits, layer, hg_gain, rows):
    t = proj.shape[0]
    assert t % rows == 0 and rows % HG_BLOCK == 0
    col = lambda g: pl.BlockSpec((rows, GROUP_WIDTH), lambda i, g=g: (i, g))
    n_l = logits.shape[0]
    return pl.pallas_call(
        functools.partial(_hgrn_kernel, layer),
        grid=(t // rows,),
        in_specs=[
            col(0), col(1), col(2), col(3),
            pl.BlockSpec((n_l, GROUP_WIDTH), lambda i: (0, 0)),
            pl.BlockSpec((1, HEAD_DIM), lambda i: (0, 0)),
            pl.BlockSpec((2 * HG_BLOCK, HG_BLOCK), lambda i: (0, 0)),
            pl.BlockSpec((HG_BLOCK, HG_BLOCK), lambda i: (0, 0)),
        ],
        out_specs=[
            pl.BlockSpec((rows, GROUP_WIDTH), lambda i: (i, 0)),
            pl.BlockSpec((N_HEADS, HEAD_DIM, HEAD_DIM), lambda i: (0, 0, 0)),
        ],
        out_shape=[
            jax.ShapeDtypeStruct((t, GROUP_WIDTH), BF16),
            jax.ShapeDtypeStruct((N_HEADS, HEAD_DIM, HEAD_DIM), F32),
        ],
        scratch_shapes=[pltpu.VMEM((N_HEADS, HEAD_DIM, HEAD_DIM), F32)]
        + [pltpu.VMEM((rows, GROUP_WIDTH), F32) for _ in range(5)],
        compiler_params=_params("arbitrary"),
        name="hgrn_prompt",
    )(proj, proj, proj, proj, logits, hg_gain.reshape(1, HEAD_DIM), _chunk_tri(), _pair_levels())


def _hgrn_step_kernel(layer, q_ref, f_ref, v_ref, gate_ref, logit_ref, gain_ref, s_ref, o_ref, snew_ref,
                      o_scr, gate_scr):
    lb = _lower_bound(logit_ref[...], layer)
    nb = q_ref.shape[0]
    eye = (lax.broadcasted_iota(jnp.int32, (HEAD_DIM, HEAD_DIM), 0)
           == lax.broadcasted_iota(jnp.int32, (HEAD_DIM, HEAD_DIM), 1))

    def column(row):
        return jnp.sum(jnp.where(eye, jnp.broadcast_to(row, (HEAD_DIM, HEAD_DIM)), 0.0),
                       axis=-1, keepdims=True)

    for b in range(nb):
        f = lb + (1.0 - lb) * _sigmoid(f_ref[b])
        qq = q_ref[b] * _sigmoid(q_ref[b])
        vv = v_ref[b]
        gate_scr[b:b + 1, :] = gate_ref[b]
        for h in range(N_HEADS):
            sl = slice(h * HEAD_DIM, (h + 1) * HEAD_DIM)
            f_col = column(f[:, sl])
            q_col = column(qq[:, sl])
            s_new = f_col * s_ref[b, h] + (1.0 - f_col) * vv[:, sl]
            snew_ref[b, h] = s_new
            o_scr[b:b + 1, sl] = jnp.sum(q_col * s_new, axis=0, keepdims=True)
    for h in range(N_HEADS):
        sl = slice(h * HEAD_DIM, (h + 1) * HEAD_DIM)
        res = _gated_norm(o_scr[:, sl], gain_ref[...], gate_scr[:, sl]).astype(F32)
        for b in range(nb):
            o_ref[b, :, sl] = res[b:b + 1]


def _hgrn_step(proj_s, logits, layer, hg_gain, state):
    b = proj_s.shape[0]
    nb = _row_tile(b, SAMPLES_PER_STEP)
    p3 = proj_s.reshape(b, 1, proj_s.shape[1])
    row = (nb, 1, GROUP_WIDTH)
    n_l = logits.shape[0]
    st_blk = (nb, N_HEADS, HEAD_DIM, HEAD_DIM)
    o, s_new = pl.pallas_call(
        functools.partial(_hgrn_step_kernel, layer),
        grid=(b // nb,),
        in_specs=[pl.BlockSpec(row, lambda i, g=g: (i, 0, g)) for g in (0, 1, 2, 3)]
        + [pl.BlockSpec((n_l, GROUP_WIDTH), lambda i: (0, 0)),
           pl.BlockSpec((1, HEAD_DIM), lambda i: (0, 0)),
           pl.BlockSpec(st_blk, lambda i: (i, 0, 0, 0))],
        out_specs=[pl.BlockSpec(row, lambda i: (i, 0, 0)),
                   pl.BlockSpec(st_blk, lambda i: (i, 0, 0, 0))],
        out_shape=[jax.ShapeDtypeStruct((b, 1, GROUP_WIDTH), F32),
                   jax.ShapeDtypeStruct(state.shape, F32)],
        scratch_shapes=[pltpu.VMEM((nb, GROUP_WIDTH), F32), pltpu.VMEM((nb, GROUP_WIDTH), F32)],
        compiler_params=_params("parallel"),
        name="hgrn_step",
    )(p3, p3, p3, p3, logits, hg_gain.reshape(1, HEAD_DIM), state)
    return o.reshape(b, GROUP_WIDTH), s_new


def _merge_kernel(n_pat, *refs):
    o_refs = refs[:n_pat]
    lse_refs = refs[n_pat:2 * n_pat] if n_pat > 1 else ()
    k = 2 * n_pat if n_pat > 1 else n_pat
    (hg_ref, x_ref, again_ref, wout_ref, npost_ref, npre_ref,
     x1_ref, h2_ref, buf_a, buf_b) = refs[k:]
    tm = x_ref.shape[0]
    i = pl.program_id(0)

    def merge_stage(dst):
        if n_pat > 1:
            lses = [r[...] for r in lse_refs]
            top = functools.reduce(jnp.maximum, lses)
            ws = [jnp.exp(x - top) for x in lses]
            inv = 1.0 / functools.reduce(lambda a, b: a + b, ws)
            ws = [w * inv for w in ws]
        att = []
        sq_lanes = jnp.zeros((tm, HEAD_DIM), F32)
        for h in range(N_HEADS):
            sl = slice(h * HEAD_DIM, (h + 1) * HEAD_DIM)
            if n_pat > 1:
                a = functools.reduce(
                    lambda x, y: x + y,
                    [w[:, h:h + 1] * r[h] for w, r in zip(ws, o_refs)])
            else:
                a = o_refs[0][:, sl]
            att.append(a)
            sq_lanes = sq_lanes + a * a
        sq = jnp.sum(sq_lanes, axis=-1, keepdims=True)
        dst[:, GROUP_WIDTH:] = hg_ref[...].astype(BF16)
        inv_rms = lax.rsqrt(sq * (1.0 / GROUP_WIDTH) + EPS)
        for h, a in enumerate(att):
            sl = slice(h * HEAD_DIM, (h + 1) * HEAD_DIM)
            dst[:, sl] = (a * inv_rms * again_ref[:, sl]).astype(BF16)

    def project_stage(src):
        mix = jnp.dot(src[...], wout_ref[...], preferred_element_type=F32)
        x1 = x_ref[...] + _rms(mix, npost_ref[...])
        x1_ref[...] = x1
        h2_ref[...] = _rms(x1, npre_ref[...]).astype(BF16)

    @pl.when(i == 0)
    def _():
        buf_b[...] = jnp.zeros_like(buf_b)

    @pl.when(i % 2 == 0)
    def _():
        project_stage(buf_b)
        merge_stage(buf_a)

    @pl.when(i % 2 == 1)
    def _():
        project_stage(buf_a)
        merge_stage(buf_b)


def _merge(o_list, lse_list, o_hg, x, attn_gain, w_out_bf16, n_post, n_pre, tm):
    t, d = x.shape
    n_pat = len(o_list)
    n = t // tm
    cur = lambda i: jnp.minimum(i, n - 1)
    prev = lambda i: jnp.maximum(i - 1, 0)
    row_cur = lambda w: pl.BlockSpec((tm, w), lambda i: (cur(i), 0))
    row_prev = lambda w: pl.BlockSpec((tm, w), lambda i: (prev(i), 0))
    full = lambda a: pl.BlockSpec(a.shape, lambda i: (0,) * a.ndim, pipeline_mode=pl.Buffered(1))
    args = list(o_list)
    if n_pat > 1:
        in_specs = [pl.BlockSpec((N_HEADS, tm, HEAD_DIM), lambda i: (0, cur(i), 0)) for _ in o_list]
        args += list(lse_list)
        in_specs += [row_cur(HEAD_DIM) for _ in lse_list]
    else:
        in_specs = [row_cur(GROUP_WIDTH)]
    small = [attn_gain.reshape(1, -1), w_out_bf16, n_post.reshape(1, -1), n_pre.reshape(1, -1)]
    args += [o_hg, x] + small
    in_specs += [row_cur(GROUP_WIDTH), row_prev(d)]
    in_specs += [full(a) for a in small]
    return pl.pallas_call(
        functools.partial(_merge_kernel, n_pat),
        grid=(n + 1,),
        in_specs=in_specs,
        out_specs=[row_prev(d), row_prev(d)],
        out_shape=[jax.ShapeDtypeStruct((t, d), F32), jax.ShapeDtypeStruct((t, d), BF16)],
        scratch_shapes=[pltpu.VMEM((tm, 2 * GROUP_WIDTH), BF16) for _ in range(2)],
        compiler_params=_params("arbitrary"),
        name="merge_outproj",
    )(*args)


def _ffn_kernel(h_ref, x1_ref, wg_ref, wu_ref, wd_ref, npost_ref, y_ref):
    j = pl.program_id(1)
    last = pl.num_programs(1) - 1
    tm = h_ref.shape[0]
    tf = wg_ref.shape[1]
    d = wd_ref.shape[1]

    def step(first, final):
        h = h_ref[...]
        acts = []
        for c in range(0, tf, FFN_CHUNK):
            gate = jnp.dot(h, wg_ref[:, c:c + FFN_CHUNK], preferred_element_type=F32)
            up = jnp.dot(h, wu_ref[:, c:c + FFN_CHUNK], preferred_element_type=F32)
            acts.append((gate * _sigmoid(gate) * up).astype(BF16))
        act = jnp.concatenate(acts, axis=-1)
        sq_lanes = jnp.zeros((tm, HEAD_DIM), F32)
        for c in range(0, d, FFN_CHUNK):
            cols = slice(c, c + FFN_CHUNK)
            total = jnp.dot(act, wd_ref[:, cols].astype(BF16), preferred_element_type=F32)
            if not first:
                total = total + y_ref[:, cols]
            y_ref[:, cols] = total
            if final:
                for l0 in range(0, FFN_CHUNK, HEAD_DIM):
                    sq_lanes = sq_lanes + total[:, l0:l0 + HEAD_DIM] * total[:, l0:l0 + HEAD_DIM]
        if final:
            sq = jnp.sum(sq_lanes, axis=-1, keepdims=True)
            inv = lax.rsqrt(sq * (1.0 / d) + EPS)
            for c in range(0, d, FFN_CHUNK):
                cols = slice(c, c + FFN_CHUNK)
                y_ref[:, cols] = x1_ref[:, cols] + y_ref[:, cols] * inv * npost_ref[:, cols]

    pl.when(j == 0)(functools.partial(step, True, False))
    pl.when((j > 0) & (j < last))(functools.partial(step, False, False))
    pl.when(j == last)(functools.partial(step, False, True))


def _ffn(h2, x1, wg, wu, wd, n_post, tm, tf):
    t, d = x1.shape
    dff = wg.shape[1]
    assert dff % tf == 0 and dff // tf >= 2
    return pl.pallas_call(
        _ffn_kernel,
        grid=(t // tm, dff // tf),
        in_specs=[
            pl.BlockSpec((tm, d), lambda i, j: (i, 0)),
            pl.BlockSpec((tm, d), lambda i, j: (i, 0)),
            pl.BlockSpec((d, tf), lambda i, j: (0, j)),
            pl.BlockSpec((d, tf), lambda i, j: (0, j)),
            pl.BlockSpec((tf, d), lambda i, j: (j, 0)),
            pl.BlockSpec((1, d), lambda i, j: (0, 0)),
        ],
        out_specs=pl.BlockSpec((tm, d), lambda i, j: (i, 0)),
        out_shape=jax.ShapeDtypeStruct((t, d), F32),
        compiler_params=_params("parallel", "arbitrary"),
        name="ffn",
    )(h2, x1, wg, wu, wd, n_post.reshape(1, d))


def _row_tile(t, want):
    tm = min(t, want)
    assert t % tm == 0
    return tm


INPROJ_ROWS, INPROJ_COLS = 1024, 512
HGRN_ROWS = 512
MERGE_ROWS = 512
FFN_ROWS, FFN_COLS = 1024, 256
SAMPLE_FFN_COLS = 512
SAMPLES_PER_STEP = 4
ATTN_BLOCKS_PER_STEP = 8


def kernel(x_prompt, x_sample, cache_win_k, cache_win_v, state_hgrn, norm_pre_mix, w_in,
           hg_lb_logits, attn_out_gain, hg_norm_gain, w_out, norm_post_mix, norm_pre_ffn,
           w_gate, w_up, w_down, norm_post_ffn):
    depth = w_in.shape[0]
    bp, t, d = x_prompt.shape
    bs, ts, _ = x_sample.shape
    assert bp == 1 and ts == 1
    keep = min(max(w for w, _ in PATTERNS), t)
    logits = hg_lb_logits.astype(F32)

    yp = x_prompt.reshape(t, d)
    ys = x_sample.reshape(bs, d)
    outs = [[] for _ in range(6)]
    for l in range(depth):
        w_in_b, w_out_b = w_in[l].astype(BF16), w_out[l].astype(BF16)
        wg_b, wu_b = w_gate[l].astype(BF16), w_up[l].astype(BF16)
        wd = w_down[l]

        dils = tuple(dil for _, dil in PATTERNS)
        hg_cols, proj, *dec = _inproj(yp, norm_pre_mix[l], w_in_b, _row_tile(t, INPROJ_ROWS),
                                      INPROJ_COLS, dils, keep)
        o_list, lse_list = zip(*[_band_attn(qkv, dil) for qkv, dil in zip(dec, dils)])
        o_hg, s_fin = _hgrn_prompt(hg_cols, logits, l, hg_norm_gain[l], _row_tile(t, HGRN_ROWS))
        x1, h2 = _merge(o_list, lse_list, o_hg, yp, attn_out_gain[l],
                        w_out_b, norm_post_mix[l], norm_pre_ffn[l], _row_tile(t, MERGE_ROWS))
        yp = _ffn(h2, x1, wg_b, wu_b, wd, norm_post_ffn[l], _row_tile(t, FFN_ROWS), FFN_COLS)
        outs[0].append(proj[:, :GROUP_WIDTH].reshape(1, keep, N_HEADS, HEAD_DIM))
        outs[1].append(proj[:, GROUP_WIDTH:].reshape(1, keep, N_HEADS, HEAD_DIM))
        outs[2].append(s_fin.reshape(1, N_HEADS, HEAD_DIM, HEAD_DIM))

        (proj_s,) = _inproj(ys, norm_pre_mix[l], w_in_b, bs, 2 * INPROJ_COLS)
        hg_cols_s = proj_s[:, 3 * GROUP_WIDTH:]
        o_att_s = _sample_attn(proj_s, cache_win_k[l], cache_win_v[l])
        o_hg_s, s_new = _hgrn_step(hg_cols_s, logits, l, hg_norm_gain[l], state_hgrn[l])
        x1s, h2s = _merge([o_att_s], [], o_hg_s, ys, attn_out_gain[l],
                          w_out_b, norm_post_mix[l], norm_pre_ffn[l], bs)
        ys = _ffn(h2s, x1s, wg_b, wu_b, wd, norm_post_ffn[l], bs, SAMPLE_FFN_COLS)
        outs[3].append(proj_s[:, GROUP_WIDTH:2 * GROUP_WIDTH].reshape(bs, 1, N_HEADS, HEAD_DIM))
        outs[4].append(proj_s[:, 2 * GROUP_WIDTH:3 * GROUP_WIDTH].reshape(bs, 1, N_HEADS, HEAD_DIM))
        outs[5].append(s_new)

    st = lambda xs: jnp.stack(xs)
    return (yp.reshape(1, t, d), ys.reshape(bs, 1, d), st(outs[0]), st(outs[1]), st(outs[2]),
            st(outs[3]), st(outs[4]), st(outs[5]))
```
